```python
import jax, jax.numpy as jnp
from jax import lax
import numpy as np

D_MODEL = 1024
BATCH = 16
SEQ = 4096
DEPTH = 4

MEM_LEN = 256
EPS = 1e-6
RET_HEADS = 4
RET_HEAD_DIM = D_MODEL // RET_HEADS
RET_WIDTH = RET_HEADS * RET_HEAD_DIM
RET_CHUNK = 128
ROPE_BASE = 10000.0
FOX_HEADS = 16
FOX_HEAD_DIM = D_MODEL // FOX_HEADS
FOX_WIDTH = FOX_HEADS * FOX_HEAD_DIM
Q_BLOCK = 128
MEM_HEADS = 4
MEM_HEAD_DIM = D_MODEL // MEM_HEADS
MEM_WIDTH = MEM_HEADS * MEM_HEAD_DIM
N_BRANCHES = 3
SPLIT_SIZES = (RET_WIDTH, RET_WIDTH, RET_WIDTH, RET_WIDTH,
               FOX_WIDTH, FOX_WIDTH, FOX_WIDTH, FOX_HEADS,
               MEM_WIDTH,
               D_MODEL, D_MODEL, D_MODEL)
IN_COLS = sum(SPLIT_SIZES)
N_EXPERTS = 16
N_GROUPS = 4
EXPERTS_PER_GROUP = N_EXPERTS // N_GROUPS
TOP_K = 2
D_FF_EXPERT = D_MODEL // 2

kernel_name = "hybrid_retention_fox_memory_grouped_moe"


def rms_norm(x, gain):
    xf = x.astype(jnp.float32)
    y = xf * lax.rsqrt(jnp.mean(xf * xf, axis=-1, keepdims=True) + EPS)
    return (y * gain.astype(jnp.float32)).astype(x.dtype)


def head_rms(x):
    xf = x.astype(jnp.float32)
    return (xf * lax.rsqrt(jnp.mean(xf * xf, axis=-1, keepdims=True) + EPS)).astype(x.dtype)


def rotary(x):
    s, d = x.shape[1], x.shape[-1]
    inv_freq = ROPE_BASE ** (-jnp.arange(0, d, 2, dtype=jnp.float32) / d)
    ang = jnp.arange(s, dtype=jnp.float32)[:, None] * inv_freq[None, :]
    cos = jnp.cos(ang)[None, :, None, :]
    sin = jnp.sin(ang)[None, :, None, :]
    xf = x.astype(jnp.float32)
    x1, x2 = xf[..., : d // 2], xf[..., d // 2:]
    return jnp.concatenate([x1 * cos - x2 * sin, x2 * cos + x1 * sin], axis=-1).astype(x.dtype)


def retention(q, k, v):
    b, s, h, dk = q.shape
    dv = v.shape[-1]
    c = RET_CHUNK
    n = s // c
    log_gamma = jnp.log1p(-(2.0 ** (-5.0 - jnp.arange(h, dtype=jnp.float32))))
    idx = jnp.arange(c, dtype=jnp.float32)
    diff = idx[:, None] - idx[None, :]
    intra = jnp.where(diff >= 0, jnp.exp(log_gamma[:, None, None] * jnp.maximum(diff, 0.0)), 0.0)
    q_decay = jnp.exp(log_gamma[:, None] * (idx + 1.0))[None, :, :, None]
    k_decay = jnp.exp(log_gamma[:, None] * (c - 1.0 - idx))[None, :, :, None]
    chunk_decay = jnp.exp(log_gamma * c)[None, :, None, None]

    def to_chunks(t):
        return t.reshape(b, n, c, h, t.shape[-1]).transpose(1, 0, 3, 2, 4)

    qc, kc, vc = to_chunks(q), to_chunks(k * (dk ** -0.5)), to_chunks(v)

    def step(state, inp):
        qi, ki, vi = inp
        scores = jnp.einsum('bhid,bhjd->bhij', qi, ki) * intra
        o = (jnp.einsum('bhij,bhje->bhie', scores, vi)
             + jnp.einsum('bhid,bhde->bhie', qi * q_decay, state))
        state = state * chunk_decay + jnp.einsum('bhjd,bhje->bhde', ki * k_decay, vi)
        return state, o

    state0 = jnp.zeros((b, h, dk, dv), jnp.float32)
    _, o = lax.scan(step, state0, (qc, kc, vc))
    return o.transpose(1, 0, 3, 2, 4).reshape(b, s, h, dv).astype(v.dtype)


def forgetting_attention(q, k, v, log_f):
    b, s, h, d = q.shape
    nb = s // Q_BLOCK
    scale = d ** -0.5
    cum = jnp.cumsum(log_f, axis=1).transpose(0, 2, 1)
    kh = k.transpose(0, 2, 1, 3)
    vh = v.transpose(0, 2, 1, 3)
    qb = q.reshape(b, nb, Q_BLOCK, h, d).transpose(1, 0, 3, 2, 4)
    cb = cum.reshape(b, h, nb, Q_BLOCK).transpose(2, 0, 1, 3)
    key_pos = jnp.arange(s)

    def block(args):
        qi, ci, i = args
        q_pos = i * Q_BLOCK + jnp.arange(Q_BLOCK)
        logits = (jnp.einsum('bhqd,bhkd->bhqk', qi, kh).astype(jnp.float32) * scale
                  + ci[..., None] - cum[:, :, None, :])
        logits = jnp.where(key_pos[None, :] <= q_pos[:, None], logits, -jnp.inf)
        p = jax.nn.softmax(logits, axis=-1).astype(v.dtype)
        return jnp.einsum('bhqk,bhkd->bhqd', p, vh)

    o = lax.map(block, (qb, cb, jnp.arange(nb)))
    return o.transpose(1, 0, 3, 2, 4).reshape(b, s, h, d)


def memory_attention(q, k, v):
    logits = jnp.einsum('bshd,bmhd->bhsm', q, k).astype(jnp.float32) * (q.shape[-1] ** -0.5)
    p = jax.nn.softmax(logits, axis=-1).astype(v.dtype)
    return jnp.einsum('bhsm,bmhd->bshd', p, v)


def grouped_moe(h, router_w, router_bias, w_gate, w_up, w_down):
    t = h.reshape(-1, h.shape[-1])
    logits = t.astype(jnp.float32) @ router_w.astype(jnp.float32)
    scores = jax.nn.sigmoid(logits)
    biased = scores + router_bias.astype(jnp.float32)
    group_score = lax.top_k(biased.reshape(-1, N_GROUPS, EXPERTS_PER_GROUP), TOP_K)[0].sum(-1)
    top_group = jnp.argmax(group_score, axis=-1)
    expert_group = jnp.arange(N_EXPERTS) // EXPERTS_PER_GROUP
    masked = jnp.where(top_group[:, None] == expert_group[None, :], biased, -jnp.inf)
    _, idx = lax.top_k(masked, TOP_K)
    sel = jnp.take_along_axis(scores, idx, axis=-1)
    w = sel / jnp.sum(sel, axis=-1, keepdims=True)
    gates = jnp.einsum('tk,tke->te', w, jax.nn.one_hot(idx, N_EXPERTS, dtype=jnp.float32)).astype(t.dtype)
    out = jnp.zeros_like(t)
    for e in range(N_EXPERTS):
        a = jax.nn.silu(t @ w_gate[e]) * (t @ w_up[e])
        out = out + gates[:, e:e + 1] * (a @ w_down[e])
    return out.reshape(h.shape)


def setup_inputs(seed: int = 0) -> dict:
    key = jax.random.key(seed)
    ks = jax.random.split(key, 21)

    def nrm(k, shape, scale):
        return scale * jax.random.normal(k, shape, jnp.float32)

    resid = (2.0 * DEPTH) ** -0.5
    return {
        "x": nrm(ks[0], (BATCH, SEQ, D_MODEL), 1.0),
        "mem": nrm(ks[1], (BATCH, MEM_LEN, D_MODEL), 1.0),
        "norm_mix": 1.0 + nrm(ks[2], (DEPTH, D_MODEL), 0.02),
        "norm_mem": 1.0 + nrm(ks[3], (DEPTH, D_MODEL), 0.02),
        "w_in": nrm(ks[4], (DEPTH, D_MODEL, IN_COLS), D_MODEL ** -0.5),
        "b_forget": 2.0 + nrm(ks[5], (DEPTH, FOX_HEADS), 0.1),
        "fox_q_norm": 1.0 + nrm(ks[6], (DEPTH, FOX_HEAD_DIM), 0.02),
        "fox_k_norm": 1.0 + nrm(ks[7], (DEPTH, FOX_HEAD_DIM), 0.02),
        "mem_q_norm": 1.0 + nrm(ks[8], (DEPTH, MEM_HEAD_DIM), 0.02),
        "mem_k_norm": 1.0 + nrm(ks[9], (DEPTH, MEM_HEAD_DIM), 0.02),
        "w_mem_kv": nrm(ks[10], (DEPTH, D_MODEL, 2 * MEM_WIDTH), D_MODEL ** -0.5),
        "w_o_ret": nrm(ks[11], (DEPTH, RET_WIDTH, D_MODEL), RET_WIDTH ** -0.5),
        "w_o_fox": nrm(ks[12], (DEPTH, FOX_WIDTH, D_MODEL), FOX_WIDTH ** -0.5),
        "w_o_mem": nrm(ks[13], (DEPTH, MEM_WIDTH, D_MODEL), MEM_WIDTH ** -0.5),
        "w_out": nrm(ks[14], (DEPTH, D_MODEL, D_MODEL), D_MODEL ** -0.5 * resid),
        "norm_ffn": 1.0 + nrm(ks[15], (DEPTH, D_MODEL), 0.02),
        "router_w": nrm(ks[16], (D_MODEL, N_EXPERTS), D_MODEL ** -0.5),
        "router_bias": nrm(ks[17], (N_EXPERTS,), 0.01),
        "w_gate": nrm(ks[18], (DEPTH, N_EXPERTS, D_MODEL, D_FF_EXPERT), D_MODEL ** -0.5),
        "w_up": nrm(ks[19], (DEPTH, N_EXPERTS, D_MODEL, D_FF_EXPERT), D_MODEL ** -0.5),
        "w_down": nrm(ks[20], (DEPTH, N_EXPERTS, D_FF_EXPERT, D_MODEL), D_FF_EXPERT ** -0.5 * resid),
    }


def reference(x, mem, norm_mix, norm_mem, w_in, b_forget, fox_q_norm, fox_k_norm,
              mem_q_norm, mem_k_norm, w_mem_kv, w_o_ret, w_o_fox, w_o_mem, w_out,
              norm_ffn, router_w, router_bias, w_gate, w_up, w_down):
    b, s, _ = x.shape
    m = mem.shape[1]
    split_points = [int(p) for p in np.cumsum(SPLIT_SIZES)[:-1]]

    def heads(t, n_heads):
        return t.reshape(t.shape[0], t.shape[1], n_heads, -1)

    for l in range(DEPTH):
        h = rms_norm(x, norm_mix[l])
        proj = h @ w_in[l]
        (rq, rk, rv, rg, fq, fk, fv, ff, mq,
         g_ret, g_fox, g_mem) = jnp.split(proj, split_points, axis=-1)

        ro = retention(rotary(heads(rq, RET_HEADS)), rotary(heads(rk, RET_HEADS)), heads(rv, RET_HEADS))
        ro = head_rms(ro).reshape(b, s, RET_WIDTH) * jax.nn.silu(rg)
        y_ret = ro @ w_o_ret[l]

        log_f = jax.nn.log_sigmoid((ff + b_forget[l]).astype(jnp.float32))
        fo = forgetting_attention(rms_norm(heads(fq, FOX_HEADS), fox_q_norm[l]),
                                  rms_norm(heads(fk, FOX_HEADS), fox_k_norm[l]),
                                  heads(fv, FOX_HEADS), log_f)
        y_fox = fo.reshape(b, s, FOX_WIDTH) @ w_o_fox[l]

        mk, mv = jnp.split(rms_norm(mem, norm_mem[l]) @ w_mem_kv[l], 2, axis=-1)
        mo = memory_attention(rms_norm(heads(mq, MEM_HEADS), mem_q_norm[l]),
                              rms_norm(mk.reshape(b, m, MEM_HEADS, MEM_HEAD_DIM), mem_k_norm[l]),
                              mv.reshape(b, m, MEM_HEADS, MEM_HEAD_DIM))
        y_mem = mo.reshape(b, s, MEM_WIDTH) @ w_o_mem[l]

        merged = (jax.nn.sigmoid(g_ret) * y_ret + jax.nn.sigmoid(g_fox) * y_fox
                  + jax.nn.sigmoid(g_mem) * y_mem)
        x = x + merged @ w_out[l]

        x = x + grouped_moe(rms_norm(x, norm_ffn[l]), router_w, router_bias,
                            w_gate[l], w_up[l], w_down[l])
    return x
```

```python
import functools

import jax
import jax.numpy as jnp
from jax import lax
from jax.experimental import pallas as pl
from jax.experimental.pallas import tpu as pltpu

F32 = jnp.float32
BF16 = jnp.bfloat16

D_MODEL = 1024
EPS = 1e-6
RET_HEADS = 4
RET_HEAD_DIM = 256
RET_CHUNK = 128
ROPE_BASE = 10000.0
FOX_HEADS = 16
FOX_HEAD_DIM = 64
MEM_HEADS = 4
MEM_HEAD_DIM = 256
N_EXPERTS = 16
N_GROUPS = 4
EXPERTS_PER_GROUP = 4
D_FF = 512

LANES = 128
SUBLANES = 8
VMEM_LIMIT = 56 * 1024 * 1024
NEG_BIG = -1e30

NT_DIMS = (((1,), (1,)), ((), ()))


def _params(*sem):
    return pltpu.CompilerParams(dimension_semantics=sem, vmem_limit_bytes=VMEM_LIMIT)


def _rms_rows(x, gain_row):
    ms = jnp.mean(x * x, axis=-1, keepdims=True)
    return x * lax.rsqrt(ms + EPS) * gain_row


IN_TM = 1024
IN_BLOCKS = 11


def _inproj_kernel(x_ref, g_ref, w_ref, wff_ref, out_ref, ff_ref, hn_ref):
    @pl.when(pl.program_id(1) == 0)
    def _():
        hb = _rms_rows(x_ref[...], g_ref[...]).astype(BF16)
        hn_ref[...] = hb
        ff_ref[...] = jnp.dot(hb, wff_ref[...], preferred_element_type=F32)

    out_ref[...] = jnp.dot(hn_ref[...], w_ref[...], preferred_element_type=F32).astype(BF16)


def _in_projection(x2, gain, w_main, w_ff):
    t = x2.shape[0]
    return pl.pallas_call(
        _inproj_kernel,
        grid=(t // IN_TM, IN_BLOCKS),
        in_specs=[
            pl.BlockSpec((IN_TM, D_MODEL), lambda i, j: (i, 0)),
            pl.BlockSpec((1, D_MODEL), lambda i, j: (0, 0)),
            pl.BlockSpec((D_MODEL, D_MODEL), lambda i, j: (0, j)),
            pl.BlockSpec((D_MODEL, LANES), lambda i, j: (0, 0)),
        ],
        out_specs=[
            pl.BlockSpec((IN_TM, D_MODEL), lambda i, j: (i, j)),
            pl.BlockSpec((IN_TM, LANES), lambda i, j: (i, 0)),
        ],
        out_shape=[
            jax.ShapeDtypeStruct((t, IN_BLOCKS * D_MODEL), BF16),
            jax.ShapeDtypeStruct((t, LANES), F32),
        ],
        scratch_shapes=[pltpu.VMEM((IN_TM, D_MODEL), BF16)],
        compiler_params=_params("parallel", "arbitrary"),
        name="in_projection",
    )(x2, gain, w_main, w_ff)


CUM_BLOCK = 512


def _cumsum_kernel(ff_ref, b_ref, out_ref):
    s = ff_ref.shape[0]
    z = ff_ref[...] + b_ref[...]
    log_f = jnp.minimum(z, 0.0) - jnp.log1p(jnp.exp(-jnp.abs(z)))
    log_f_t = log_f.T[:FOX_HEADS]
    r = lax.broadcasted_iota(jnp.int32, (CUM_BLOCK, CUM_BLOCK), 0)
    c = lax.broadcasted_iota(jnp.int32, (CUM_BLOCK, CUM_BLOCK), 1)
    upper = (r <= c).astype(F32)
    carry = jnp.zeros((FOX_HEADS, 1), F32)
    for blk in range(s // CUM_BLOCK):
        seg = log_f_t[:, blk * CUM_BLOCK:(blk + 1) * CUM_BLOCK]
        cs = jnp.dot(seg, upper, preferred_element_type=F32,
                     precision=lax.Precision.HIGHEST) + carry
        out_ref[:, blk * CUM_BLOCK:(blk + 1) * CUM_BLOCK] = cs
        carry = cs[:, CUM_BLOCK - 1:CUM_BLOCK]


def _forget_cumsum(ff, b_pad, batch, seq):
    return pl.pallas_call(
        _cumsum_kernel,
        grid=(batch,),
        in_specs=[
            pl.BlockSpec((seq, LANES), lambda b: (b, 0)),
            pl.BlockSpec((1, LANES), lambda b: (0, 0)),
        ],
        out_specs=pl.BlockSpec((None, FOX_HEADS, seq), lambda b: (b, 0, 0)),
        out_shape=jax.ShapeDtypeStruct((batch, FOX_HEADS, seq), F32),
        compiler_params=_params("parallel"),
        name="forget_cumsum",
    )(ff, b_pad)


def _retention_kernel(q_ref, k_ref, v_ref, g_ref, cos_ref, sin_ref, intra_ref, qd_ref, kd_ref,
                      cd_ref, out_ref, state_ref):
    seq = q_ref.shape[0]
    half = RET_HEAD_DIM // 2
    state_ref[...] = jnp.zeros_like(state_ref)

    def chunk(c, carry):
        r0 = pl.multiple_of(c * RET_CHUNK, RET_CHUNK)
        rows = pl.ds(r0, RET_CHUNK)
        cs = cos_ref[rows, :]
        sn = sin_ref[rows, :]

        def rot(x):
            x1 = x[:, :half]
            x2 = x[:, half:]
            return jnp.concatenate([x1 * cs - x2 * sn, x2 * cs + x1 * sn], axis=-1)

        qr = rot(q_ref[rows, :].astype(F32))
        kr = rot(k_ref[rows, :].astype(F32))
        v = v_ref[rows, :]
        scores = lax.dot_general(qr.astype(BF16), kr.astype(BF16), NT_DIMS,
                                 preferred_element_type=F32) * intra_ref[...]
        st = state_ref[...]
        o = (jnp.dot(scores.astype(BF16), v, preferred_element_type=F32)
             + jnp.dot((qr * qd_ref[...]).astype(BF16), st.astype(BF16),
                       preferred_element_type=F32))
        kd_t = (kr * kd_ref[...]).T.astype(BF16)
        state_ref[...] = st * cd_ref[...] + jnp.dot(kd_t, v, preferred_element_type=F32)
        on = o * lax.rsqrt(jnp.mean(o * o, axis=-1, keepdims=True) + EPS)
        g = g_ref[rows, :].astype(F32)
        out_ref[rows, :] = (on * (g * jax.nn.sigmoid(g))).astype(BF16)
        return carry

    lax.fori_loop(0, seq // RET_CHUNK, chunk, 0)


def _retention(proj, cos, sin, intra, qd, kd, cd, batch, seq):
    t = batch * seq
    hd = RET_HEAD_DIM
    col = lambda off: (lambda b, h: (b, off + h))
    tab = lambda b, h: (h, 0, 0)
    return pl.pallas_call(
        _retention_kernel,
        grid=(batch, RET_HEADS),
        in_specs=[
            pl.BlockSpec((seq, hd), col(0)),
            pl.BlockSpec((seq, hd), col(RET_HEADS)),
            pl.BlockSpec((seq, hd), col(2 * RET_HEADS)),
            pl.BlockSpec((seq, hd), col(3 * RET_HEADS)),
            pl.BlockSpec((seq, hd // 2), lambda b, h: (0, 0)),
            pl.BlockSpec((seq, hd // 2), lambda b, h: (0, 0)),
            pl.BlockSpec((None, RET_CHUNK, RET_CHUNK), tab),
            pl.BlockSpec((None, RET_CHUNK, hd), tab),
            pl.BlockSpec((None, RET_CHUNK, hd), tab),
            pl.BlockSpec((None, 1, hd), tab),
        ],
        out_specs=pl.BlockSpec((seq, hd), lambda b, h: (b, h)),
        out_shape=jax.ShapeDtypeStruct((t, RET_HEADS * hd), BF16),
        scratch_shapes=[pltpu.VMEM((hd, hd), F32)],
        compiler_params=_params("parallel", "parallel"),
        name="retention",
    )(proj, proj, proj, proj, cos, sin, intra, qd, kd, cd)


def _retention_tables(seq):
    h = jnp.arange(RET_HEADS, dtype=F32)
    log_gamma = jnp.log1p(-(2.0 ** (-5.0 - h)))
    idx = jnp.arange(RET_CHUNK, dtype=F32)
    diff = idx[:, None] - idx[None, :]
    scale = RET_HEAD_DIM ** -0.5
    intra = jnp.where(diff >= 0, jnp.exp(log_gamma[:, None, None] * jnp.maximum(diff, 0.0)), 0.0) * scale
    q_decay = jnp.exp(log_gamma[:, None] * (idx + 1.0))
    k_decay = jnp.exp(log_gamma[:, None] * (RET_CHUNK - 1.0 - idx)) * scale
    chunk_decay = jnp.exp(log_gamma * RET_CHUNK)
    qd = jnp.broadcast_to(q_decay[:, :, None], (RET_HEADS, RET_CHUNK, RET_HEAD_DIM))
    kd = jnp.broadcast_to(k_decay[:, :, None], (RET_HEADS, RET_CHUNK, RET_HEAD_DIM))
    cd = jnp.broadcast_to(chunk_decay[:, None, None], (RET_HEADS, 1, RET_HEAD_DIM))
    inv_freq = ROPE_BASE ** (-jnp.arange(0, RET_HEAD_DIM, 2, dtype=F32) / RET_HEAD_DIM)
    ang = jnp.arange(seq, dtype=F32)[:, None] * inv_freq[None, :]
    return jnp.cos(ang), jnp.sin(ang), intra, qd, kd, cd


FOX_TQ = 512
FOX_TK = 512
FOX_PREP_ROWS = 512


def _pair_rms(x, lo_mask, gain_row):
    x2 = x * x
    s_all = jnp.sum(x2, axis=-1, keepdims=True)
    s_lo = jnp.sum(jnp.where(lo_mask, x2, 0.0), axis=-1, keepdims=True)
    ms = jnp.where(lo_mask, s_lo, s_all - s_lo) * (1.0 / FOX_HEAD_DIM)
    return x * lax.rsqrt(ms + EPS) * gain_row


def _fox_kernel(q_ref, k_ref, v_ref, c_ref, qg_ref, kg_ref, out_ref,
                kn_ref, va_ref, qz_ref, acc_ref, m_ref):
    seq = k_ref.shape[0]
    hp = pl.program_id(1)
    qi = pl.program_id(2)
    lo = lax.broadcasted_iota(jnp.int32, (1, LANES), 1) < FOX_HEAD_DIM

    @pl.when(qi == 0)
    def _prepare_keys_values():
        def prep(i, carry):
            rows = pl.ds(pl.multiple_of(i * FOX_PREP_ROWS, FOX_PREP_ROWS), FOX_PREP_ROWS)
            kn_ref[rows, :] = _pair_rms(k_ref[rows, :].astype(F32), lo, kg_ref[...]).astype(BF16)
            v = v_ref[rows, :]
            one = jnp.ones_like(v)
            va_ref[0, rows, :] = jnp.where(lo, v, one)
            va_ref[1, rows, :] = jnp.where(lo, one, v)
            return carry
        lax.fori_loop(0, seq // FOX_PREP_ROWS, prep, 0)

    qn = _pair_rms(q_ref[...].astype(F32), lo, qg_ref[...]) * (FOX_HEAD_DIM ** -0.5)
    qz_ref[0] = jnp.where(lo, qn, 0.0).astype(BF16)
    qz_ref[1] = jnp.where(lo, 0.0, qn).astype(BF16)
    m_ref[...] = jnp.full_like(m_ref, NEG_BIG)
    acc_ref[...] = jnp.zeros_like(acc_ref)
    t0 = pl.multiple_of(qi * FOX_TQ, FOX_TQ)

    def kv_step(j, masked):
        k0 = pl.multiple_of(j * FOX_TK, FOX_TK)
        kb = kn_ref[pl.ds(k0, FOX_TK), :]
        for h in range(2):
            row = pl.ds(2 * hp + h, 1)
            c_t0 = c_ref[row, pl.ds(t0, FOX_TQ)][:, 0:1]
            bias = c_t0 - c_ref[row, pl.ds(k0, FOX_TK)]
            s = lax.dot_general(qz_ref[h], kb, NT_DIMS, preferred_element_type=F32) + bias
            if masked:
                ri = lax.broadcasted_iota(jnp.int32, (FOX_TQ, FOX_TK), 0)
                ci = lax.broadcasted_iota(jnp.int32, (FOX_TQ, FOX_TK), 1)
                s = jnp.where(ci <= ri, s, NEG_BIG)
            m_old = m_ref[h]
            m_new = jnp.maximum(m_old, jnp.max(s, axis=-1, keepdims=True))
            p = jnp.exp(s - m_new)
            alpha = jnp.exp(m_old - m_new)
            acc_ref[h] = alpha * acc_ref[h] + jnp.dot(
                p.astype(BF16), va_ref[h, pl.ds(k0, FOX_TK), :], preferred_element_type=F32)
            m_ref[h] = m_new

    def body(j, carry):
        kv_step(j, False)
        return carry

    lax.fori_loop(0, qi, body, 0)
    kv_step(qi, True)

    a0 = acc_ref[0]
    a1 = acc_ref[1]
    l0 = a0[:, FOX_HEAD_DIM:FOX_HEAD_DIM + 1]
    l1 = a1[:, 0:1]
    out_ref[...] = jnp.where(lo, a0 / l0, a1 / l1).astype(BF16)


def _fox_attention(proj, cum, qg, kg, batch, seq):
    t = batch * seq
    nq = seq // FOX_TQ
    pairs = FOX_HEADS // 2
    base = 4 * (D_MODEL // LANES)
    return pl.pallas_call(
        _fox_kernel,
        grid=(batch, pairs, nq),
        in_specs=[
            pl.BlockSpec((FOX_TQ, LANES), lambda b, p, i: (b * nq + i, base + p)),
            pl.BlockSpec((seq, LANES), lambda b, p, i: (b, base + pairs + p)),
            pl.BlockSpec((seq, LANES), lambda b, p, i: (b, base + 2 * pairs + p)),
            pl.BlockSpec((None, FOX_HEADS, seq), lambda b, p, i: (b, 0, 0)),
            pl.BlockSpec((1, LANES), lambda b, p, i: (0, 0)),
            pl.BlockSpec((1, LANES), lambda b, p, i: (0, 0)),
        ],
        out_specs=pl.BlockSpec((FOX_TQ, LANES), lambda b, p, i: (b * nq + i, p)),
        out_shape=jax.ShapeDtypeStruct((t, FOX_HEADS * FOX_HEAD_DIM), BF16),
        scratch_shapes=[
            pltpu.VMEM((seq, LANES), BF16),
            pltpu.VMEM((2, seq, LANES), BF16),
            pltpu.VMEM((2, FOX_TQ, LANES), BF16),
            pltpu.VMEM((2, FOX_TQ, LANES), F32),
            pltpu.VMEM((2, FOX_TQ, 1), F32),
        ],
        compiler_params=_params("parallel", "parallel", "arbitrary"),
        name="fox_attention",
    )(proj, proj, proj, cum, qg, kg)


MEMKV_TM = 512
MEMATT_TQ = 1024


def _mem_kv_kernel(mem_ref, g_ref, w_ref, kg_ref, k_out, v_out):
    hb = _rms_rows(mem_ref[...], g_ref[...]).astype(BF16)
    kv = jnp.dot(hb, w_ref[...], preferred_element_type=F32)
    width = MEM_HEADS * MEM_HEAD_DIM
    for h in range(MEM_HEADS):
        cols = slice(h * MEM_HEAD_DIM, (h + 1) * MEM_HEAD_DIM)
        k_out[:, cols] = _rms_rows(kv[:, cols], kg_ref[...]).astype(BF16)
    v_out[...] = kv[:, width:].astype(BF16)


def _mem_kv(mem2, gain, w_kv, k_gain):
    rows = mem2.shape[0]
    width = MEM_HEADS * MEM_HEAD_DIM
    return pl.pallas_call(
        _mem_kv_kernel,
        grid=(rows // MEMKV_TM,),
        in_specs=[
            pl.BlockSpec((MEMKV_TM, D_MODEL), lambda i: (i, 0)),
            pl.BlockSpec((1, D_MODEL), lambda i: (0, 0)),
            pl.BlockSpec((D_MODEL, 2 * width), lambda i: (0, 0)),
            pl.BlockSpec((1, MEM_HEAD_DIM), lambda i: (0, 0)),
        ],
        out_specs=[pl.BlockSpec((MEMKV_TM, width), lambda i: (i, 0))] * 2,
        out_shape=[jax.ShapeDtypeStruct((rows, width), BF16)] * 2,
        compiler_params=_params("parallel"),
        name="mem_kv",
    )(mem2, gain, w_kv, k_gain)


def _mem_attn_kernel(q_ref, k_ref, v_ref, qg_ref, out_ref):
    for h in range(MEM_HEADS):
        cols = slice(h * MEM_HEAD_DIM, (h + 1) * MEM_HEAD_DIM)
        qn = _rms_rows(q_ref[:, cols].astype(F32), qg_ref[...]) * (MEM_HEAD_DIM ** -0.5)
        s = lax.dot_general(qn.astype(BF16), k_ref[:, cols], NT_DIMS, preferred_element_type=F32)
        p = jnp.exp(s - jnp.max(s, axis=-1, keepdims=True))
        denom = jnp.sum(p, axis=-1, keepdims=True)
        o = jnp.dot(p.astype(BF16), v_ref[:, cols], preferred_element_type=F32)
        out_ref[:, cols] = (o / denom).astype(BF16)


def _mem_attention(proj, mk, mv, q_gain, batch, seq, mem_len):
    t = batch * seq
    nq = seq // MEMATT_TQ
    width = MEM_HEADS * MEM_HEAD_DIM
    return pl.pallas_call(
        _mem_attn_kernel,
        grid=(batch, nq),
        in_specs=[
            pl.BlockSpec((MEMATT_TQ, width), lambda b, i: (b * nq + i, 7)),
            pl.BlockSpec((mem_len, width), lambda b, i: (b, 0)),
            pl.BlockSpec((mem_len, width), lambda b, i: (b, 0)),
            pl.BlockSpec((1, MEM_HEAD_DIM), lambda b, i: (0, 0)),
        ],
        out_specs=pl.BlockSpec((MEMATT_TQ, width), lambda b, i: (b * nq + i, 0)),
        out_shape=jax.ShapeDtypeStruct((t, width), BF16),
        compiler_params=_params("parallel", "parallel"),
        name="mem_attention",
    )(proj, mk, mv, q_gain)


MIX_TM = 512
ROUTER_ROWS = 32


def _top2_sum(b0, b1, b2, b3):
    p, q = jnp.maximum(b0, b1), jnp.minimum(b0, b1)
    r, s = jnp.maximum(b2, b3), jnp.minimum(b2, b3)
    return jnp.maximum(p, r) + jnp.maximum(jnp.minimum(p, r), jnp.maximum(q, s))


def _mix_kernel(ro_ref, fo_ref, mo_ref, gr_ref, gf_ref, gm_ref, x_ref,
                wr_ref, wf_ref, wm_ref, wo_ref, nf_ref, rw_ref, rb_ref,
                xo_ref, h_ref, gt_ref):
    def branch(a_ref, w_ref, g_ref):
        y = jnp.dot(a_ref[...], w_ref[...], preferred_element_type=F32)
        return jax.nn.sigmoid(g_ref[...].astype(F32)) * y

    merged = (branch(ro_ref, wr_ref, gr_ref) + branch(fo_ref, wf_ref, gf_ref)
              + branch(mo_ref, wm_ref, gm_ref))
    xn = x_ref[...] + jnp.dot(merged.astype(BF16), wo_ref[...], preferred_element_type=F32)
    xo_ref[...] = xn
    h = _rms_rows(xn, nf_ref[...])
    h_hi = h.astype(BF16)
    h_ref[...] = h_hi

    h_lo = (h - h_hi.astype(F32)).astype(BF16)
    rw = rw_ref[...]
    rw_hi = rw.astype(BF16)
    rw_lo = (rw - rw_hi.astype(F32)).astype(BF16)
    dot_nt = lambda a, b: lax.dot_general(a, b, NT_DIMS, preferred_element_type=F32)
    logits = dot_nt(rw_hi, h_hi) + dot_nt(rw_hi, h_lo) + dot_nt(rw_lo, h_hi)
    scores = jax.nn.sigmoid(logits)
    biased = scores + rb_ref[...]
    n = EXPERTS_PER_GROUP
    sc = [scores[SUBLANES * j:SUBLANES * (j + 1)] for j in range(n)]
    bi = [biased[SUBLANES * j:SUBLANES * (j + 1)] for j in range(n)]
    group_score = _top2_sum(*bi)
    rows = lax.broadcasted_iota(jnp.int32, group_score.shape, 0)
    best = jnp.max(group_score, axis=0, keepdims=True)
    top_group = jnp.min(jnp.where(group_score == best, rows, SUBLANES), axis=0, keepdims=True)
    in_group = rows == top_group
    picked = []
    for j in range(n):
        rank = jnp.zeros(group_score.shape, jnp.int32)
        for i in range(n):
            if i == j:
                continue
            ahead = (bi[i] > bi[j]) | ((bi[i] == bi[j]) & (i < j))
            rank = rank + ahead.astype(jnp.int32)
        picked.append(jnp.where(in_group & (rank < 2), sc[j], 0.0))
    denom = jnp.sum(picked[0] + picked[1] + picked[2] + picked[3], axis=0, keepdims=True)
    for j in range(n):
        gt_ref[SUBLANES * j:SUBLANES * (j + 1), :] = picked[j] / denom


def _mix_and_route(ro, fo, mo, proj, x2, w_r, w_f, w_m, w_o, norm_ffn, rw_pad, rb_pad):
    t = x2.shape[0]
    tok = lambda i: (i, 0)
    const = lambda i: (0, 0)
    act = pl.BlockSpec((MIX_TM, D_MODEL), tok)
    wspec = pl.BlockSpec((D_MODEL, D_MODEL), const)
    return pl.pallas_call(
        _mix_kernel,
        grid=(t // MIX_TM,),
        in_specs=[
            act, act, act,
            pl.BlockSpec((MIX_TM, D_MODEL), lambda i: (i, 8)),
            pl.BlockSpec((MIX_TM, D_MODEL), lambda i: (i, 9)),
            pl.BlockSpec((MIX_TM, D_MODEL), lambda i: (i, 10)),
            act,
            wspec, wspec, wspec, wspec,
            pl.BlockSpec((1, D_MODEL), const),
            pl.BlockSpec((ROUTER_ROWS, D_MODEL), const),
            pl.BlockSpec((ROUTER_ROWS, 1), const),
        ],
        out_specs=[
            act,
            act,
            pl.BlockSpec((ROUTER_ROWS, MIX_TM), lambda i: (0, i)),
        ],
        out_shape=[
            jax.ShapeDtypeStruct((t, D_MODEL), F32),
            jax.ShapeDtypeStruct((t, D_MODEL), BF16),
            jax.ShapeDtypeStruct((ROUTER_ROWS, t), F32),
        ],
        compiler_params=_params("parallel"),
        name="mix_and_route",
    )(ro, fo, mo, proj, proj, proj, x2, w_r, w_f, w_m, w_o, norm_ffn, rw_pad, rb_pad)


MOE_TM = 1024


def _moe_kernel(h_ref, gt_ref, x_ref, wg_ref, wu_ref, wd_ref, out_ref, acc_ref, gates_ref):
    e = pl.program_id(1)

    @pl.when(e == 0)
    def _():
        acc_ref[...] = x_ref[...]
        pad = jnp.zeros((LANES - ROUTER_ROWS, MOE_TM), F32)
        gates_ref[...] = jnp.concatenate([gt_ref[...], pad], axis=0).T

    h = h_ref[...]
    a = jnp.dot(h, wg_ref[...], preferred_element_type=F32)
    u = jnp.dot(h, wu_ref[...], preferred_element_type=F32)
    act = (a * jax.nn.sigmoid(a) * u).astype(BF16)
    y = jnp.dot(act, wd_ref[...], preferred_element_type=F32)
    col = SUBLANES * (e % EXPERTS_PER_GROUP) + e // EXPERTS_PER_GROUP
    lanes = lax.broadcasted_iota(jnp.int32, (1, LANES), 1)
    gate = jnp.sum(jnp.where(lanes == col, gates_ref[...], 0.0), axis=-1, keepdims=True)
    acc_ref[...] += gate * y

    @pl.when(e == N_EXPERTS - 1)
    def _():
        out_ref[...] = acc_ref[...]


def _experts(h2, gates_t, x2, w_gate, w_up, w_down):
    t = x2.shape[0]
    return pl.pallas_call(
        _moe_kernel,
        grid=(t // MOE_TM, N_EXPERTS),
        in_specs=[
            pl.BlockSpec((MOE_TM, D_MODEL), lambda i, e: (i, 0)),
            pl.BlockSpec((ROUTER_ROWS, MOE_TM), lambda i, e: (0, i)),
            pl.BlockSpec((MOE_TM, D_MODEL), lambda i, e: (i, 0)),
            pl.BlockSpec((None, D_MODEL, D_FF), lambda i, e: (e, 0, 0)),
            pl.BlockSpec((None, D_MODEL, D_FF), lambda i, e: (e, 0, 0)),
            pl.BlockSpec((None, D_FF, D_MODEL), lambda i, e: (e, 0, 0)),
        ],
        out_specs=pl.BlockSpec((MOE_TM, D_MODEL), lambda i, e: (i, 0)),
        out_shape=jax.ShapeDtypeStruct((t, D_MODEL), F32),
        scratch_shapes=[pltpu.VMEM((MOE_TM, D_MODEL), F32), pltpu.VMEM((MOE_TM, LANES), F32)],
        compiler_params=_params("parallel", "arbitrary"),
        name="experts",
    )(h2, gates_t, x2, w_gate, w_up, w_down)


def _router_layout(router_w, router_bias):
    w = router_w.T.reshape(N_GROUPS, EXPERTS_PER_GROUP, D_MODEL)
    w = jnp.transpose(w, (1, 0, 2))
    w = jnp.pad(w, ((0, 0), (0, SUBLANES - N_GROUPS), (0, 0))).reshape(ROUTER_ROWS, D_MODEL)
    b = jnp.transpose(router_bias.reshape(N_GROUPS, EXPERTS_PER_GROUP), (1, 0))
    b = jnp.pad(b, ((0, 0), (0, SUBLANES - N_GROUPS)), constant_values=NEG_BIG)
    return w.astype(F32), b.reshape(ROUTER_ROWS, 1).astype(F32)


def kernel(x, mem, norm_mix, norm_mem, w_in, b_forget, fox_q_norm, fox_k_norm, mem_q_norm,
           mem_k_norm, w_mem_kv, w_o_ret, w_o_fox, w_o_mem, w_out, norm_ffn, router_w,
           router_bias, w_gate, w_up, w_down):
    batch, seq, d = x.shape
    mem_len = mem.shape[1]
    depth = w_in.shape[0]
    t = batch * seq
    ff_lo = 7 * D_MODEL
    ff_hi = ff_lo + FOX_HEADS

    cos, sin, intra, qd, kd, cd = _retention_tables(seq)
    rw_pad, rb_pad = _router_layout(router_w, router_bias)
    row = lambda v: v.reshape(1, -1).astype(F32)

    x2 = x.reshape(t, d)
    mem2 = mem.reshape(batch * mem_len, d)
    for l in range(depth):
        w_main = jnp.concatenate([w_in[l, :, :ff_lo], w_in[l, :, ff_hi:]], axis=1).astype(BF16)
        w_ff = jnp.pad(w_in[l, :, ff_lo:ff_hi], ((0, 0), (0, LANES - FOX_HEADS))).astype(BF16)
        b_pad = jnp.pad(b_forget[l], (0, LANES - FOX_HEADS)).reshape(1, LANES)

        proj, ff = _in_projection(x2, row(norm_mix[l]), w_main, w_ff)
        cum = _forget_cumsum(ff, b_pad, batch, seq)
        ro = _retention(proj, cos, sin, intra, qd, kd, cd, batch, seq)
        fo = _fox_attention(proj, cum, row(jnp.tile(fox_q_norm[l], 2)),
                            row(jnp.tile(fox_k_norm[l], 2)), batch, seq)
        mk, mv = _mem_kv(mem2, row(norm_mem[l]), w_mem_kv[l].astype(BF16), row(mem_k_norm[l]))
        mo = _mem_attention(proj, mk, mv, row(mem_q_norm[l]), batch, seq, mem_len)
        x2, h2, gates_t = _mix_and_route(
            ro, fo, mo, proj, x2, w_o_ret[l].astype(BF16), w_o_fox[l].astype(BF16),
            w_o_mem[l].astype(BF16), w_out[l].astype(BF16), row(norm_ffn[l]), rw_pad, rb_pad)
        x2 = _experts(h2, gates_t, x2, w_gate[l].astype(BF16), w_up[l].astype(BF16),
                      w_down[l].astype(BF16))
    return x2.reshape(batch, seq, d)
```

```python
import functools

import jax
import jax.numpy as jnp
from jax import lax
from jax.experimental import pallas as pl
from jax.experimental.pallas import tpu as pltpu

F32 = jnp.float32
BF16 = jnp.bfloat16

D_MODEL = 1024
EPS = 1e-6
RET_HEADS = 4
RET_HEAD_DIM = 256
RET_CHUNK = 128
ROPE_BASE = 10000.0
FOX_HEADS = 16
FOX_HEAD_DIM = 64
MEM_HEADS = 4
MEM_HEAD_DIM = 256
N_EXPERTS = 16
N_GROUPS = 4
EXPERTS_PER_GROUP = 4
D_FF = 512

LANES = 128
SUBLANES = 8
VMEM_LIMIT = 56 * 1024 * 1024
NEG_BIG = -1e30

NT_DIMS = (((1,), (1,)), ((), ()))


def _params(*sem):
    return pltpu.CompilerParams(dimension_semantics=sem, vmem_limit_bytes=VMEM_LIMIT)


def _rms_rows(x, gain_row):
    ms = jnp.mean(x * x, axis=-1, keepdims=True)
    return x * lax.rsqrt(ms + EPS) * gain_row


IN_TM = 2048
IN_BLOCKS = 11


def _inproj_kernel(x_ref, g_ref, w_ref, wff_ref, out_ref, ff_ref, hn_ref):
    @pl.when(pl.program_id(1) == 0)
    def _():
        hb = _rms_rows(x_ref[...], g_ref[...]).astype(BF16)
        hn_ref[...] = hb
        ff_ref[...] = jnp.dot(hb, wff_ref[...], preferred_element_type=F32)

    out_ref[...] = jnp.dot(hn_ref[...], w_ref[...], preferred_element_type=F32).astype(BF16)


def _in_projection(x2, gain, w_main, w_ff):
    t = x2.shape[0]
    return pl.pallas_call(
        _inproj_kernel,
        grid=(t // IN_TM, IN_BLOCKS),
        in_specs=[
            pl.BlockSpec((IN_TM, D_MODEL), lambda i, j: (i, 0)),
            pl.BlockSpec((1, D_MODEL), lambda i, j: (0, 0)),
            pl.BlockSpec((D_MODEL, D_MODEL), lambda i, j: (0, j)),
            pl.BlockSpec((D_MODEL, LANES), lambda i, j: (0, 0)),
        ],
        out_specs=[
            pl.BlockSpec((IN_TM, D_MODEL), lambda i, j: (i, j)),
            pl.BlockSpec((IN_TM, LANES), lambda i, j: (i, 0)),
        ],
        out_shape=[
            jax.ShapeDtypeStruct((t, IN_BLOCKS * D_MODEL), BF16),
            jax.ShapeDtypeStruct((t, LANES), F32),
        ],
        scratch_shapes=[pltpu.VMEM((IN_TM, D_MODEL), BF16)],
        compiler_params=_params("parallel", "arbitrary"),
        name="in_projection",
    )(x2, gain, w_main, w_ff)


CUM_BLOCK = 512


def _cumsum_kernel(ff_ref, b_ref, out_ref):
    s = ff_ref.shape[0]
    r = lax.broadcasted_iota(jnp.int32, (CUM_BLOCK, CUM_BLOCK), 0)
    c = lax.broadcasted_iota(jnp.int32, (CUM_BLOCK, CUM_BLOCK), 1)
    lower = (c <= r).astype(F32)
    carry = jnp.zeros((1, LANES), F32)
    for blk in range(s // CUM_BLOCK):
        rows = slice(blk * CUM_BLOCK, (blk + 1) * CUM_BLOCK)
        z = ff_ref[rows, :] + b_ref[...]
        log_f = jnp.minimum(z, 0.0) - jnp.log1p(jnp.exp(-jnp.abs(z)))
        cs = jnp.dot(lower, log_f, preferred_element_type=F32,
                     precision=lax.Precision.HIGHEST) + carry
        out_ref[rows, :] = cs
        carry = cs[CUM_BLOCK - 1:CUM_BLOCK, :]


def _forget_cumsum(ff, b_pad, batch, seq):
    return pl.pallas_call(
        _cumsum_kernel,
        grid=(batch,),
        in_specs=[
            pl.BlockSpec((seq, LANES), lambda b: (b, 0)),
            pl.BlockSpec((1, LANES), lambda b: (0, 0)),
        ],
        out_specs=pl.BlockSpec((seq, LANES), lambda b: (b, 0)),
        out_shape=jax.ShapeDtypeStruct((batch * seq, LANES), F32),
        compiler_params=_params("parallel"),
        name="forget_cumsum",
    )(ff, b_pad)


def _retention_kernel(q_ref, k_ref, v_ref, g_ref, cos_ref, sin_ref, intra_ref, qd_ref, kd_ref,
                      cd_ref, out_ref, state_ref):
    seq = q_ref.shape[0]
    half = RET_HEAD_DIM // 2
    state_ref[...] = jnp.zeros_like(state_ref)

    def chunk(c, carry):
        r0 = pl.multiple_of(c * RET_CHUNK, RET_CHUNK)
        rows = pl.ds(r0, RET_CHUNK)
        cs = cos_ref[rows, :]
        sn = sin_ref[rows, :]

        def rot(x):
            x1 = x[:, :half]
            x2 = x[:, half:]
            return jnp.concatenate([x1 * cs - x2 * sn, x2 * cs + x1 * sn], axis=-1)

        qr = rot(q_ref[rows, :].astype(F32))
        kr = rot(k_ref[rows, :].astype(F32))
        v = v_ref[rows, :]
        scores = lax.dot_general(qr.astype(BF16), kr.astype(BF16), NT_DIMS,
                                 preferred_element_type=F32) * intra_ref[...]
        st = state_ref[...]
        o = (jnp.dot(scores.astype(BF16), v, preferred_element_type=F32)
             + jnp.dot((qr * qd_ref[...]).astype(BF16), st.astype(BF16),
                       preferred_element_type=F32))
        kd_t = (kr * kd_ref[...]).T.astype(BF16)
        state_ref[...] = st * cd_ref[...] + jnp.dot(kd_t, v, preferred_element_type=F32)
        on = o * lax.rsqrt(jnp.mean(o * o, axis=-1, keepdims=True) + EPS)
        g = g_ref[rows, :].astype(F32)
        out_ref[rows, :] = (on * (g * jax.nn.sigmoid(g))).astype(BF16)
        return carry

    lax.fori_loop(0, seq // RET_CHUNK, chunk, 0)


def _retention(proj, cos, sin, intra, qd, kd, cd, batch, seq):
    t = batch * seq
    hd = RET_HEAD_DIM
    col = lambda off: (lambda b, h: (b, off + h))
    tab = lambda b, h: (h, 0, 0)
    return pl.pallas_call(
        _retention_kernel,
        grid=(batch, RET_HEADS),
        in_specs=[
            pl.BlockSpec((seq, hd), col(0)),
            pl.BlockSpec((seq, hd), col(RET_HEADS)),
            pl.BlockSpec((seq, hd), col(2 * RET_HEADS)),
            pl.BlockSpec((seq, hd), col(3 * RET_HEADS)),
            pl.BlockSpec((seq, hd // 2), lambda b, h: (0, 0)),
            pl.BlockSpec((seq, hd // 2), lambda b, h: (0, 0)),
            pl.BlockSpec((None, RET_CHUNK, RET_CHUNK), tab),
            pl.BlockSpec((None, RET_CHUNK, hd), tab),
            pl.BlockSpec((None, RET_CHUNK, hd), tab),
            pl.BlockSpec((None, 1, hd), tab),
        ],
        out_specs=pl.BlockSpec((seq, hd), lambda b, h: (b, h)),
        out_shape=jax.ShapeDtypeStruct((t, RET_HEADS * hd), BF16),
        scratch_shapes=[pltpu.VMEM((hd, hd), F32)],
        compiler_params=_params("parallel", "parallel"),
        name="retention",
    )(proj, proj, proj, proj, cos, sin, intra, qd, kd, cd)


def _retention_tables(seq):
    h = jnp.arange(RET_HEADS, dtype=F32)
    log_gamma = jnp.log1p(-(2.0 ** (-5.0 - h)))
    idx = jnp.arange(RET_CHUNK, dtype=F32)
    diff = idx[:, None] - idx[None, :]
    scale = RET_HEAD_DIM ** -0.5
    intra = jnp.where(diff >= 0, jnp.exp(log_gamma[:, None, None] * jnp.maximum(diff, 0.0)), 0.0) * scale
    q_decay = jnp.exp(log_gamma[:, None] * (idx + 1.0))
    k_decay = jnp.exp(log_gamma[:, None] * (RET_CHUNK - 1.0 - idx)) * scale
    chunk_decay = jnp.exp(log_gamma * RET_CHUNK)
    qd = jnp.broadcast_to(q_decay[:, :, None], (RET_HEADS, RET_CHUNK, RET_HEAD_DIM))
    kd = jnp.broadcast_to(k_decay[:, :, None], (RET_HEADS, RET_CHUNK, RET_HEAD_DIM))
    cd = jnp.broadcast_to(chunk_decay[:, None, None], (RET_HEADS, 1, RET_HEAD_DIM))
    inv_freq = ROPE_BASE ** (-jnp.arange(0, RET_HEAD_DIM, 2, dtype=F32) / RET_HEAD_DIM)
    ang = jnp.arange(seq, dtype=F32)[:, None] * inv_freq[None, :]
    return jnp.cos(ang), jnp.sin(ang), intra, qd, kd, cd


FOX_TQ = 512
FOX_TK = 512
assert FOX_TQ == FOX_TK


def _pair_rms(x, lo_mask, gain_row):
    x2 = x * x
    s_all = jnp.sum(x2, axis=-1, keepdims=True)
    s_lo = jnp.sum(jnp.where(lo_mask, x2, 0.0), axis=-1, keepdims=True)
    ms = jnp.where(lo_mask, s_lo, s_all - s_lo) * (1.0 / FOX_HEAD_DIM)
    return x * lax.rsqrt(ms + EPS) * gain_row


AUG_CQ = 0
AUG_CK = 3
AUG_M = 6
AUG_SPLIT = 3
FOX_EXACT_ABOVE = 30.0


def _split3(x):
    hi = x.astype(BF16)
    r1 = x - hi.astype(F32)
    mid = r1.astype(BF16)
    lo = (r1 - mid.astype(F32)).astype(BF16)
    return hi, mid, lo


def _lanes_in(lane, start, count=AUG_SPLIT):
    return (lane >= start) & (lane < start + count)


def _fox_kernel(bound_ref, q_ref, k_ref, v_ref, c_ref, qg_ref, kg_ref, out_ref,
                kaug_ref, caq_ref, vat_ref, qb_ref, qat_ref, acc_ref, m_ref):
    seq = k_ref.shape[0]
    hp = pl.program_id(1)
    qi = pl.program_id(2)
    lane = lax.broadcasted_iota(jnp.int32, (1, LANES), 1)
    lo = lane < FOX_HEAD_DIM
    head_lanes = (lo, jnp.logical_not(lo))
    spare = (FOX_HEAD_DIM, 0)
    bound = bound_ref[0]

    @pl.when(qi == 0)
    def _prepare_keys_values():
        r = lax.broadcasted_iota(jnp.int32, (LANES, LANES), 0)
        c = lax.broadcasted_iota(jnp.int32, (LANES, LANES), 1)
        place = []
        for k in range(AUG_SPLIT):
            pk = jnp.zeros((LANES, LANES), F32)
            for h in range(2):
                src = r == 2 * hp + h
                pk = pk + jnp.where(src & (c == spare[h] + AUG_CQ + k), 1.0, 0.0)
                pk = pk - jnp.where(src & (c == spare[h] + AUG_CK + k), 1.0, 0.0)
            place.append(pk.astype(BF16))
        bound_parts = [p.astype(F32) for p in _split3(jnp.full((1, LANES), -bound, F32))]

        def prep(i, carry):
            rows = pl.ds(pl.multiple_of(i * FOX_TK, FOX_TK), FOX_TK)
            parts = _split3(c_ref[rows, :])
            e = sum(jnp.dot(parts[k], place[k], preferred_element_type=F32)
                    for k in range(AUG_SPLIT))
            kn = _pair_rms(k_ref[rows, :].astype(F32), lo, kg_ref[...])
            v = v_ref[rows, :]
            one = jnp.ones_like(v)
            for h in range(2):
                ones_k = jnp.where(_lanes_in(lane, spare[h] + AUG_CQ)
                                   | _lanes_in(lane, spare[h] + AUG_M), 1.0, 0.0)
                kaug_ref[h, rows, :] = jnp.where(
                    head_lanes[h], kn,
                    jnp.where(_lanes_in(lane, spare[h] + AUG_CK), e, ones_k)).astype(BF16)
                const_q = jnp.where(_lanes_in(lane, spare[h] + AUG_CK), 1.0, 0.0)
                for k in range(AUG_SPLIT):
                    const_q = jnp.where(lane == spare[h] + AUG_M + k, bound_parts[k], const_q)
                caq_ref[h, rows, :] = jnp.where(
                    _lanes_in(lane, spare[h] + AUG_CQ), e, const_q).astype(BF16)
                va = jnp.where(head_lanes[h], v, one)
                vat_ref[h, i] = va.astype(F32).T.astype(BF16)
            return carry
        lax.fori_loop(0, seq // FOX_TK, prep, 0)

    t0 = pl.multiple_of(qi * FOX_TQ, FOX_TQ)
    q_rows = pl.ds(t0, FOX_TQ)
    qn = _pair_rms(q_ref[...].astype(F32), lo, qg_ref[...]) * (FOX_HEAD_DIM ** -0.5)
    kq = lax.broadcasted_iota(jnp.int32, (FOX_TK, FOX_TQ), 0)
    qq = lax.broadcasted_iota(jnp.int32, (FOX_TK, FOX_TQ), 1)

    q_aug = [jnp.where(head_lanes[h], qn, caq_ref[h, q_rows, :].astype(F32)) for h in range(2)]
    for h in range(2):
        qat_ref[h] = q_aug[h].T.astype(BF16)

    @pl.when(bound > FOX_EXACT_ABOVE)
    def _exact_row_max():
        for h in range(2):
            qb_ref[h] = jnp.where(_lanes_in(lane, spare[h] + AUG_M), 0.0, q_aug[h]).astype(BF16)
            m_ref[h] = jnp.full((FOX_TQ, 1), NEG_BIG, F32)

        def scan(j, masked):
            keys = pl.ds(pl.multiple_of(j * FOX_TK, FOX_TK), FOX_TK)
            for h in range(2):
                s = lax.dot_general(qb_ref[h], kaug_ref[h, keys, :], NT_DIMS,
                                    preferred_element_type=F32)
                if masked:
                    s = jnp.where(qq <= kq, s, NEG_BIG)
                m_ref[h] = jnp.maximum(m_ref[h], jnp.max(s, axis=-1, keepdims=True))

        def scan_body(j, carry):
            scan(j, False)
            return carry
        lax.fori_loop(0, qi, scan_body, 0)
        scan(qi, True)
        for h in range(2):
            m_parts = _split3(-m_ref[h])
            qa = q_aug[h]
            for k in range(AUG_SPLIT):
                qa = jnp.where(lane == spare[h] + AUG_M + k, m_parts[k].astype(F32), qa)
            qat_ref[h] = qa.T.astype(BF16)

    acc_ref[...] = jnp.zeros_like(acc_ref)

    def probs(h, keys, q_cols=slice(None)):
        st = jnp.dot(kaug_ref[h, keys, :], qat_ref[h, :, q_cols], preferred_element_type=F32)
        return jnp.exp(st)

    def kv_blocks(blocks):
        for h in range(2):
            total = None
            for j in blocks:
                keys = pl.ds(pl.multiple_of(j * FOX_TK, FOX_TK), FOX_TK)
                o = jnp.dot(vat_ref[h, j], probs(h, keys).astype(BF16), preferred_element_type=F32)
                total = o if total is None else total + o
            acc_ref[h] += total

    def pair(i, carry):
        kv_blocks((2 * i, 2 * i + 1))
        return carry

    lax.fori_loop(0, lax.shift_right_logical(qi, 1), pair, 0)

    @pl.when((qi & 1) == 1)
    def _odd_block():
        kv_blocks((qi - 1,))

    half = FOX_TK // 2
    first, second = slice(0, half), slice(half, FOX_TK)
    causal_a = (lax.broadcasted_iota(jnp.int32, (half, FOX_TQ), 0)
                <= lax.broadcasted_iota(jnp.int32, (half, FOX_TQ), 1))
    causal_b = (lax.broadcasted_iota(jnp.int32, (half, half), 0)
                <= lax.broadcasted_iota(jnp.int32, (half, half), 1))
    for h in range(2):
        p_a = jnp.where(causal_a, probs(h, pl.ds(t0, half)), 0.0)
        p_b = jnp.where(causal_b, probs(h, pl.ds(t0 + half, half), second), 0.0)
        acc_ref[h] += jnp.dot(vat_ref[h, qi, :, first], p_a.astype(BF16),
                              preferred_element_type=F32)
        acc_ref[h, :, second] += jnp.dot(vat_ref[h, qi, :, second], p_b.astype(BF16),
                                         preferred_element_type=F32)

    o0 = acc_ref[0]
    o1 = acc_ref[1]
    l0 = o0[FOX_HEAD_DIM:FOX_HEAD_DIM + 1, :]
    l1 = o1[0:1, :]
    row = lax.broadcasted_iota(jnp.int32, (LANES, 1), 0)
    out_ref[...] = jnp.where(row < FOX_HEAD_DIM, o0 / l0, o1 / l1).T.astype(BF16)


def _fox_attention(proj, cum, bound, qg, kg, batch, seq):
    t = batch * seq
    nq = seq // FOX_TQ
    pairs = FOX_HEADS // 2
    base = 4 * (D_MODEL // LANES)
    return pl.pallas_call(
        _fox_kernel,
        grid=(batch, pairs, nq),
        in_specs=[
            pl.BlockSpec(memory_space=pltpu.SMEM),
            pl.BlockSpec((FOX_TQ, LANES), lambda b, p, i: (b * nq + i, base + p)),
            pl.BlockSpec((seq, LANES), lambda b, p, i: (b, base + pairs + p)),
            pl.BlockSpec((seq, LANES), lambda b, p, i: (b, base + 2 * pairs + p)),
            pl.BlockSpec((seq, LANES), lambda b, p, i: (b, 0)),
            pl.BlockSpec((1, LANES), lambda b, p, i: (0, 0)),
            pl.BlockSpec((1, LANES), lambda b, p, i: (0, 0)),
        ],
        out_specs=pl.BlockSpec((FOX_TQ, LANES), lambda b, p, i: (b * nq + i, p)),
        out_shape=jax.ShapeDtypeStruct((t, FOX_HEADS * FOX_HEAD_DIM), BF16),
        scratch_shapes=[
            pltpu.VMEM((2, seq, LANES), BF16),
            pltpu.VMEM((2, seq, LANES), BF16),
            pltpu.VMEM((2, seq // FOX_TK, LANES, FOX_TK), BF16),
            pltpu.VMEM((2, FOX_TQ, LANES), BF16),
            pltpu.VMEM((2, LANES, FOX_TQ), BF16),
            pltpu.VMEM((2, LANES, FOX_TQ), F32),
            pltpu.VMEM((2, FOX_TQ, 1), F32),
        ],
        compiler_params=_params("parallel", "parallel", "arbitrary"),
        name="fox_attention",
    )(bound, proj, proj, proj, cum, qg, kg)


MEMKV_TM = 512
MEMATT_TQ = 1024


def _mem_kv_kernel(mem_ref, g_ref, w_ref, kg_ref, k_out, v_out):
    hb = _rms_rows(mem_ref[...], g_ref[...]).astype(BF16)
    kv = jnp.dot(hb, w_ref[...], preferred_element_type=F32)
    width = MEM_HEADS * MEM_HEAD_DIM
    for h in range(MEM_HEADS):
        cols = slice(h * MEM_HEAD_DIM, (h + 1) * MEM_HEAD_DIM)
        k_out[:, cols] = _rms_rows(kv[:, cols], kg_ref[...]).astype(BF16)
    v_out[...] = kv[:, width:].astype(BF16)


def _mem_kv(mem2, gain, w_kv, k_gain):
    rows = mem2.shape[0]
    width = MEM_HEADS * MEM_HEAD_DIM
    return pl.pallas_call(
        _mem_kv_kernel,
        grid=(rows // MEMKV_TM,),
        in_specs=[
            pl.BlockSpec((MEMKV_TM, D_MODEL), lambda i: (i, 0)),
            pl.BlockSpec((1, D_MODEL), lambda i: (0, 0)),
            pl.BlockSpec((D_MODEL, 2 * width), lambda i: (0, 0)),
            pl.BlockSpec((1, MEM_HEAD_DIM), lambda i: (0, 0)),
        ],
        out_specs=[pl.BlockSpec((MEMKV_TM, width), lambda i: (i, 0))] * 2,
        out_shape=[jax.ShapeDtypeStruct((rows, width), BF16)] * 2,
        compiler_params=_params("parallel"),
        name="mem_kv",
    )(mem2, gain, w_kv, k_gain)


def _mem_attn_kernel(q_ref, k_ref, v_ref, qg_ref, out_ref):
    for h in range(MEM_HEADS):
        cols = slice(h * MEM_HEAD_DIM, (h + 1) * MEM_HEAD_DIM)
        qn = _rms_rows(q_ref[:, cols].astype(F32), qg_ref[...]) * (MEM_HEAD_DIM ** -0.5)
        s = lax.dot_general(qn.astype(BF16), k_ref[:, cols], NT_DIMS, preferred_element_type=F32)
        p = jnp.exp(s - jnp.max(s, axis=-1, keepdims=True))
        denom = jnp.sum(p, axis=-1, keepdims=True)
        o = jnp.dot(p.astype(BF16), v_ref[:, cols], preferred_element_type=F32)
        out_ref[:, cols] = (o / denom).astype(BF16)


def _mem_attention(proj, mk, mv, q_gain, batch, seq, mem_len):
    t = batch * seq
    nq = seq // MEMATT_TQ
    width = MEM_HEADS * MEM_HEAD_DIM
    return pl.pallas_call(
        _mem_attn_kernel,
        grid=(batch, nq),
        in_specs=[
            pl.BlockSpec((MEMATT_TQ, width), lambda b, i: (b * nq + i, 7)),
            pl.BlockSpec((mem_len, width), lambda b, i: (b, 0)),
            pl.BlockSpec((mem_len, width), lambda b, i: (b, 0)),
            pl.BlockSpec((1, MEM_HEAD_DIM), lambda b, i: (0, 0)),
        ],
        out_specs=pl.BlockSpec((MEMATT_TQ, width), lambda b, i: (b * nq + i, 0)),
        out_shape=jax.ShapeDtypeStruct((t, width), BF16),
        compiler_params=_params("parallel", "parallel"),
        name="mem_attention",
    )(proj, mk, mv, q_gain)


MIX_TM = 512
ROUTER_ROWS = 32


def _top2_sum(b0, b1, b2, b3):
    p, q = jnp.maximum(b0, b1), jnp.minimum(b0, b1)
    r, s = jnp.maximum(b2, b3), jnp.minimum(b2, b3)
    return jnp.maximum(p, r) + jnp.maximum(jnp.minimum(p, r), jnp.maximum(q, s))


def _mix_kernel(ro_ref, fo_ref, mo_ref, gr_ref, gf_ref, gm_ref, x_ref,
                wr_ref, wf_ref, wm_ref, wo_ref, nf_ref, rw_ref, rb_ref,
                xo_ref, h_ref, gt_ref):
    def branch(a_ref, w_ref, g_ref):
        y = jnp.dot(a_ref[...], w_ref[...], preferred_element_type=F32)
        return jax.nn.sigmoid(g_ref[...].astype(F32)) * y

    merged = (branch(ro_ref, wr_ref, gr_ref) + branch(fo_ref, wf_ref, gf_ref)
              + branch(mo_ref, wm_ref, gm_ref))
    xn = x_ref[...] + jnp.dot(merged.astype(BF16), wo_ref[...], preferred_element_type=F32)
    xo_ref[...] = xn
    h = _rms_rows(xn, nf_ref[...])
    h_hi = h.astype(BF16)
    h_ref[...] = h_hi

    h_lo = (h - h_hi.astype(F32)).astype(BF16)
    rw = rw_ref[...]
    rw_hi = rw.astype(BF16)
    rw_lo = (rw - rw_hi.astype(F32)).astype(BF16)
    dot_nt = lambda a, b: lax.dot_general(a, b, NT_DIMS, preferred_element_type=F32)
    logits = dot_nt(rw_hi, h_hi) + dot_nt(rw_hi, h_lo) + dot_nt(rw_lo, h_hi)
    scores = jax.nn.sigmoid(logits)
    biased = scores + rb_ref[...]
    n = EXPERTS_PER_GROUP
    sc = [scores[SUBLANES * j:SUBLANES * (j + 1)] for j in range(n)]
    bi = [biased[SUBLANES * j:SUBLANES * (j + 1)] for j in range(n)]
    group_score = _top2_sum(*bi)
    rows = lax.broadcasted_iota(jnp.int32, group_score.shape, 0)
    best = jnp.max(group_score, axis=0, keepdims=True)
    top_group = jnp.min(jnp.where(group_score == best, rows, SUBLANES), axis=0, keepdims=True)
    in_group = rows == top_group
    picked = []
    for j in range(n):
        rank = jnp.zeros(group_score.shape, jnp.int32)
        for i in range(n):
            if i == j:
                continue
            ahead = (bi[i] > bi[j]) | ((bi[i] == bi[j]) & (i < j))
            rank = rank + ahead.astype(jnp.int32)
        picked.append(jnp.where(in_group & (rank < 2), sc[j], 0.0))
    denom = jnp.sum(picked[0] + picked[1] + picked[2] + picked[3], axis=0, keepdims=True)
    for j in range(n):
        gt_ref[SUBLANES * j:SUBLANES * (j + 1), :] = picked[j] / denom


def _mix_and_route(ro, fo, mo, proj, x2, w_r, w_f, w_m, w_o, norm_ffn, rw_pad, rb_pad):
    t = x2.shape[0]
    tok = lambda i: (i, 0)
    const = lambda i: (0, 0)
    act = pl.BlockSpec((MIX_TM, D_MODEL), tok)
    wspec = pl.BlockSpec((D_MODEL, D_MODEL), const)
    return pl.pallas_call(
        _mix_kernel,
        grid=(t // MIX_TM,),
        in_specs=[
            act, act, act,
            pl.BlockSpec((MIX_TM, D_MODEL), lambda i: (i, 8)),
            pl.BlockSpec((MIX_TM, D_MODEL), lambda i: (i, 9)),
            pl.BlockSpec((MIX_TM, D_MODEL), lambda i: (i, 10)),
            act,
            wspec, wspec, wspec, wspec,
            pl.BlockSpec((1, D_MODEL), const),
            pl.BlockSpec((ROUTER_ROWS, D_MODEL), const),
            pl.BlockSpec((ROUTER_ROWS, 1), const),
        ],
        out_specs=[
            act,
            act,
            pl.BlockSpec((ROUTER_ROWS, MIX_TM), lambda i: (0, i)),
        ],
        out_shape=[
            jax.ShapeDtypeStruct((t, D_MODEL), F32),
            jax.ShapeDtypeStruct((t, D_MODEL), BF16),
            jax.ShapeDtypeStruct((ROUTER_ROWS, t), F32),
        ],
        compiler_params=_params("parallel"),
        name="mix_and_route",
    )(ro, fo, mo, proj, proj, proj, x2, w_r, w_f, w_m, w_o, norm_ffn, rw_pad, rb_pad)


MOE_TM = 1024


def _moe_kernel(h_ref, gt_ref, x_ref, wg_ref, wu_ref, wd_ref, out_ref, acc_ref, gates_ref):
    e = pl.program_id(1)

    @pl.when(e == 0)
    def _():
        acc_ref[...] = x_ref[...]
        pad = jnp.zeros((LANES - ROUTER_ROWS, MOE_TM), F32)
        gates_ref[...] = jnp.concatenate([gt_ref[...], pad], axis=0).T

    h = h_ref[...]
    a = jnp.dot(h, wg_ref[...], preferred_element_type=F32)
    u = jnp.dot(h, wu_ref[...], preferred_element_type=F32)
    act = (a * jax.nn.sigmoid(a) * u).astype(BF16)
    y = jnp.dot(act, wd_ref[...], preferred_element_type=F32)
    col = SUBLANES * (e % EXPERTS_PER_GROUP) + e // EXPERTS_PER_GROUP
    lanes = lax.broadcasted_iota(jnp.int32, (1, LANES), 1)
    gate = jnp.sum(jnp.where(lanes == col, gates_ref[...], 0.0), axis=-1, keepdims=True)
    acc_ref[...] += gate * y

    @pl.when(e == N_EXPERTS - 1)
    def _():
        out_ref[...] = acc_ref[...]


def _experts(h2, gates_t, x2, w_gate, w_up, w_down):
    t = x2.shape[0]
    return pl.pallas_call(
        _moe_kernel,
        grid=(t // MOE_TM, N_EXPERTS),
        in_specs=[
            pl.BlockSpec((MOE_TM, D_MODEL), lambda i, e: (i, 0)),
            pl.BlockSpec((ROUTER_ROWS, MOE_TM), lambda i, e: (0, i)),
            pl.BlockSpec((MOE_TM, D_MODEL), lambda i, e: (i, 0)),
            pl.BlockSpec((None, D_MODEL, D_FF), lambda i, e: (e, 0, 0)),
            pl.BlockSpec((None, D_MODEL, D_FF), lambda i, e: (e, 0, 0)),
            pl.BlockSpec((None, D_FF, D_MODEL), lambda i, e: (e, 0, 0)),
        ],
        out_specs=pl.BlockSpec((MOE_TM, D_MODEL), lambda i, e: (i, 0)),
        out_shape=jax.ShapeDtypeStruct((t, D_MODEL), F32),
        scratch_shapes=[pltpu.VMEM((MOE_TM, D_MODEL), F32), pltpu.VMEM((MOE_TM, LANES), F32)],
        compiler_params=_params("parallel", "arbitrary"),
        name="experts",
    )(h2, gates_t, x2, w_gate, w_up, w_down)


def _router_layout(router_w, router_bias):
    w = router_w.T.reshape(N_GROUPS, EXPERTS_PER_GROUP, D_MODEL)
    w = jnp.transpose(w, (1, 0, 2))
    w = jnp.pad(w, ((0, 0), (0, SUBLANES - N_GROUPS), (0, 0))).reshape(ROUTER_ROWS, D_MODEL)
    b = jnp.transpose(router_bias.reshape(N_GROUPS, EXPERTS_PER_GROUP), (1, 0))
    b = jnp.pad(b, ((0, 0), (0, SUBLANES - N_GROUPS)), constant_values=NEG_BIG)
    return w.astype(F32), b.reshape(ROUTER_ROWS, 1).astype(F32)


def kernel(x, mem, norm_mix, norm_mem, w_in, b_forget, fox_q_norm, fox_k_norm, mem_q_norm,
           mem_k_norm, w_mem_kv, w_o_ret, w_o_fox, w_o_mem, w_out, norm_ffn, router_w,
           router_bias, w_gate, w_up, w_down):
    batch, seq, d = x.shape
    mem_len = mem.shape[1]
    depth = w_in.shape[0]
    t = batch * seq
    ff_lo = 7 * D_MODEL
    ff_hi = ff_lo + FOX_HEADS

    cos, sin, intra, qd, kd, cd = _retention_tables(seq)
    rw_pad, rb_pad = _router_layout(router_w, router_bias)
    row = lambda v: v.reshape(1, -1).astype(F32)

    x2 = x.reshape(t, d)
    mem2 = mem.reshape(batch * mem_len, d)
    for l in range(depth):
        w_main = jnp.concatenate([w_in[l, :, :ff_lo], w_in[l, :, ff_hi:]], axis=1).astype(BF16)
        w_ff = jnp.pad(w_in[l, :, ff_lo:ff_hi], ((0, 0), (0, LANES - FOX_HEADS))).astype(BF16)
        b_pad = jnp.pad(b_forget[l], (0, LANES - FOX_HEADS)).reshape(1, LANES)

        proj, ff = _in_projection(x2, row(norm_mix[l]), w_main, w_ff)
        cum = _forget_cumsum(ff, b_pad, batch, seq)
        ro = _retention(proj, cos, sin, intra, qd, kd, cd, batch, seq)
        bound = (8.16 * jnp.max(jnp.abs(fox_q_norm[l])) * jnp.max(jnp.abs(fox_k_norm[l])))
        fo = _fox_attention(proj, cum, bound.reshape(1).astype(F32),
                            row(jnp.tile(fox_q_norm[l], 2)), row(jnp.tile(fox_k_norm[l], 2)),
                            batch, seq)
        mk, mv = _mem_kv(mem2, row(norm_mem[l]), w_mem_kv[l].astype(BF16), row(mem_k_norm[l]))
        mo = _mem_attention(proj, mk, mv, row(mem_q_norm[l]), batch, seq, mem_len)
        x2, h2, gates_t = _mix_and_route(
            ro, fo, mo, proj, x2, w_o_ret[l].astype(BF16), w_o_fox[l].astype(BF16),
            w_o_mem[l].astype(BF16), w_out[l].astype(BF16), row(norm_ffn[l]), rw_pad, rb_pad)
        x2 = _experts(h2, gates_t, x2, w_gate[l].astype(BF16), w_up[l].astype(BF16),
                      w_down[l].astype(BF16))
    return x2.reshape(batch, seq, d)
```

```python
import functools

import jax
import jax.numpy as jnp
from jax import lax
from jax.experimental import pallas as pl
from jax.experimental.pallas import tpu as pltpu

F32 = jnp.float32
BF16 = jnp.bfloat16

D_MODEL = 1024
EPS = 1e-6
RET_HEADS = 4
RET_HEAD_DIM = 256
RET_CHUNK = 128
ROPE_BASE = 10000.0
FOX_HEADS = 16
FOX_HEAD_DIM = 64
MEM_HEADS = 4
MEM_HEAD_DIM = 256
N_EXPERTS = 16
N_GROUPS = 4
EXPERTS_PER_GROUP = 4
D_FF = 512

LANES = 128
SUBLANES = 8
VMEM_LIMIT = 56 * 1024 * 1024
NEG_BIG = -1e30

NT_DIMS = (((1,), (1,)), ((), ()))


def _params(*sem):
    return pltpu.CompilerParams(dimension_semantics=sem, vmem_limit_bytes=VMEM_LIMIT)


def _rms_rows(x, gain_row):
    ms = jnp.mean(x * x, axis=-1, keepdims=True)
    return x * lax.rsqrt(ms + EPS) * gain_row


IN_TM = 2048
IN_BLOCKS = 11


def _inproj_kernel(x_ref, g_ref, w_ref, wff_ref, out_ref, ff_ref, hn_ref):
    @pl.when(pl.program_id(1) == 0)
    def _():
        hb = _rms_rows(x_ref[...], g_ref[...]).astype(BF16)
        hn_ref[...] = hb
        ff_ref[...] = jnp.dot(hb, wff_ref[...], preferred_element_type=F32)

    out_ref[...] = jnp.dot(hn_ref[...], w_ref[...], preferred_element_type=F32).astype(BF16)


def _in_projection(x2, gain, w_main, w_ff):
    t = x2.shape[0]
    return pl.pallas_call(
        _inproj_kernel,
        grid=(t // IN_TM, IN_BLOCKS),
        in_specs=[
            pl.BlockSpec((IN_TM, D_MODEL), lambda i, j: (i, 0)),
            pl.BlockSpec((1, D_MODEL), lambda i, j: (0, 0)),
            pl.BlockSpec((D_MODEL, D_MODEL), lambda i, j: (0, j)),
            pl.BlockSpec((D_MODEL, LANES), lambda i, j: (0, 0)),
        ],
        out_specs=[
            pl.BlockSpec((IN_TM, D_MODEL), lambda i, j: (i, j)),
            pl.BlockSpec((IN_TM, LANES), lambda i, j: (i, 0)),
        ],
        out_shape=[
            jax.ShapeDtypeStruct((t, IN_BLOCKS * D_MODEL), BF16),
            jax.ShapeDtypeStruct((t, LANES), F32),
        ],
        scratch_shapes=[pltpu.VMEM((IN_TM, D_MODEL), BF16)],
        compiler_params=_params("parallel", "arbitrary"),
        name="in_projection",
    )(x2, gain, w_main, w_ff)


CUM_BLOCK = 512


def _cumsum_kernel(ff_ref, b_ref, out_ref):
    s = ff_ref.shape[0]
    r = lax.broadcasted_iota(jnp.int32, (CUM_BLOCK, CUM_BLOCK), 0)
    c = lax.broadcasted_iota(jnp.int32, (CUM_BLOCK, CUM_BLOCK), 1)
    lower = (c <= r).astype(F32)
    carry = jnp.zeros((1, LANES), F32)
    for blk in range(s // CUM_BLOCK):
        rows = slice(blk * CUM_BLOCK, (blk + 1) * CUM_BLOCK)
        z = ff_ref[rows, :] + b_ref[...]
        log_f = jnp.minimum(z, 0.0) - jnp.log1p(jnp.exp(-jnp.abs(z)))
        cs = jnp.dot(lower, log_f, preferred_element_type=F32,
                     precision=lax.Precision.HIGHEST) + carry
        out_ref[rows, :] = cs
        carry = cs[CUM_BLOCK - 1:CUM_BLOCK, :]


def _forget_cumsum(ff, b_pad, batch, seq):
    return pl.pallas_call(
        _cumsum_kernel,
        grid=(batch,),
        in_specs=[
            pl.BlockSpec((seq, LANES), lambda b: (b, 0)),
            pl.BlockSpec((1, LANES), lambda b: (0, 0)),
        ],
        out_specs=pl.BlockSpec((seq, LANES), lambda b: (b, 0)),
        out_shape=jax.ShapeDtypeStruct((batch * seq, LANES), F32),
        compiler_params=_params("parallel"),
        name="forget_cumsum",
    )(ff, b_pad)


def _retention_kernel(q_ref, k_ref, v_ref, g_ref, cos_ref, sin_ref, intra_ref, qd_ref, kd_ref,
                      cd_ref, out_ref, state_ref):
    seq = q_ref.shape[0]
    half = RET_HEAD_DIM // 2
    state_ref[...] = jnp.zeros_like(state_ref)

    def chunk(c, carry):
        r0 = pl.multiple_of(c * RET_CHUNK, RET_CHUNK)
        rows = pl.ds(r0, RET_CHUNK)
        cs = cos_ref[rows, :]
        sn = sin_ref[rows, :]

        def rot(x):
            x1 = x[:, :half]
            x2 = x[:, half:]
            return jnp.concatenate([x1 * cs - x2 * sn, x2 * cs + x1 * sn], axis=-1)

        qr = rot(q_ref[rows, :].astype(F32))
        kr = rot(k_ref[rows, :].astype(F32))
        v = v_ref[rows, :]
        scores = lax.dot_general(qr.astype(BF16), kr.astype(BF16), NT_DIMS,
                                 preferred_element_type=F32) * intra_ref[...]
        st = state_ref[...]
        o = (jnp.dot(scores.astype(BF16), v, preferred_element_type=F32)
             + jnp.dot((qr * qd_ref[...]).astype(BF16), st.astype(BF16),
                       preferred_element_type=F32))
        kd_t = (kr * kd_ref[...]).T.astype(BF16)
        state_ref[...] = st * cd_ref[...] + jnp.dot(kd_t, v, preferred_element_type=F32)
        on = o * lax.rsqrt(jnp.mean(o * o, axis=-1, keepdims=True) + EPS)
        g = g_ref[rows, :].astype(F32)
        out_ref[rows, :] = (on * (g * jax.nn.sigmoid(g))).astype(BF16)
        return carry

    lax.fori_loop(0, seq // RET_CHUNK, chunk, 0)


def _retention(proj, cos, sin, intra, qd, kd, cd, batch, seq):
    t = batch * seq
    hd = RET_HEAD_DIM
    col = lambda off: (lambda b, h: (b, off + h))
    tab = lambda b, h: (h, 0, 0)
    return pl.pallas_call(
        _retention_kernel,
        grid=(batch, RET_HEADS),
        in_specs=[
            pl.BlockSpec((seq, hd), col(0)),
            pl.BlockSpec((seq, hd), col(RET_HEADS)),
            pl.BlockSpec((seq, hd), col(2 * RET_HEADS)),
            pl.BlockSpec((seq, hd), col(3 * RET_HEADS)),
            pl.BlockSpec((seq, hd // 2), lambda b, h: (0, 0)),
            pl.BlockSpec((seq, hd // 2), lambda b, h: (0, 0)),
            pl.BlockSpec((None, RET_CHUNK, RET_CHUNK), tab),
            pl.BlockSpec((None, RET_CHUNK, hd), tab),
            pl.BlockSpec((None, RET_CHUNK, hd), tab),
            pl.BlockSpec((None, 1, hd), tab),
        ],
        out_specs=pl.BlockSpec((seq, hd), lambda b, h: (b, h)),
        out_shape=jax.ShapeDtypeStruct((t, RET_HEADS * hd), BF16),
        scratch_shapes=[pltpu.VMEM((hd, hd), F32)],
        compiler_params=_params("parallel", "parallel"),
        name="retention",
    )(proj, proj, proj, proj, cos, sin, intra, qd, kd, cd)


def _retention_tables(seq):
    h = jnp.arange(RET_HEADS, dtype=F32)
    log_gamma = jnp.log1p(-(2.0 ** (-5.0 - h)))
    idx = jnp.arange(RET_CHUNK, dtype=F32)
    diff = idx[:, None] - idx[None, :]
    scale = RET_HEAD_DIM ** -0.5
    intra = jnp.where(diff >= 0, jnp.exp(log_gamma[:, None, None] * jnp.maximum(diff, 0.0)), 0.0) * scale
    q_decay = jnp.exp(log_gamma[:, None] * (idx + 1.0))
    k_decay = jnp.exp(log_gamma[:, None] * (RET_CHUNK - 1.0 - idx)) * scale
    chunk_decay = jnp.exp(log_gamma * RET_CHUNK)
    qd = jnp.broadcast_to(q_decay[:, :, None], (RET_HEADS, RET_CHUNK, RET_HEAD_DIM))
    kd = jnp.broadcast_to(k_decay[:, :, None], (RET_HEADS, RET_CHUNK, RET_HEAD_DIM))
    cd = jnp.broadcast_to(chunk_decay[:, None, None], (RET_HEADS, 1, RET_HEAD_DIM))
    inv_freq = ROPE_BASE ** (-jnp.arange(0, RET_HEAD_DIM, 2, dtype=F32) / RET_HEAD_DIM)
    ang = jnp.arange(seq, dtype=F32)[:, None] * inv_freq[None, :]
    return jnp.cos(ang), jnp.sin(ang), intra, qd, kd, cd


FOX_TQ = 512
FOX_TK = 512
assert FOX_TQ == FOX_TK


def _pair_rms(x, lo_mask, gain_row):
    x2 = x * x
    s_all = jnp.sum(x2, axis=-1, keepdims=True)
    s_lo = jnp.sum(jnp.where(lo_mask, x2, 0.0), axis=-1, keepdims=True)
    ms = jnp.where(lo_mask, s_lo, s_all - s_lo) * (1.0 / FOX_HEAD_DIM)
    return x * lax.rsqrt(ms + EPS) * gain_row


AUG_CQ = 0
AUG_CK = 3
AUG_M = 6
AUG_SPLIT = 3
FOX_EXACT_ABOVE = 30.0


def _split3(x):
    hi = x.astype(BF16)
    r1 = x - hi.astype(F32)
    mid = r1.astype(BF16)
    lo = (r1 - mid.astype(F32)).astype(BF16)
    return hi, mid, lo


def _lanes_in(lane, start, count=AUG_SPLIT):
    return (lane >= start) & (lane < start + count)


def _fox_kernel(bound_ref, q_ref, k_ref, v_ref, c_ref, qg_ref, kg_ref, out_ref,
                kaug_ref, caq_ref, vat_ref, qb_ref, qat_ref, acc_ref, m_ref):
    seq = k_ref.shape[0]
    hp = pl.program_id(1)
    qi = pl.program_id(2)
    lane = lax.broadcasted_iota(jnp.int32, (1, LANES), 1)
    lo = lane < FOX_HEAD_DIM
    head_lanes = (lo, jnp.logical_not(lo))
    spare = (FOX_HEAD_DIM, 0)
    bound = bound_ref[0]

    @pl.when(qi == 0)
    def _prepare_keys_values():
        r = lax.broadcasted_iota(jnp.int32, (LANES, LANES), 0)
        c = lax.broadcasted_iota(jnp.int32, (LANES, LANES), 1)
        place = []
        for k in range(AUG_SPLIT):
            pk = jnp.zeros((LANES, LANES), F32)
            for h in range(2):
                src = r == 2 * hp + h
                pk = pk + jnp.where(src & (c == spare[h] + AUG_CQ + k), 1.0, 0.0)
                pk = pk - jnp.where(src & (c == spare[h] + AUG_CK + k), 1.0, 0.0)
            place.append(pk.astype(BF16))
        bound_parts = [p.astype(F32) for p in _split3(jnp.full((1, LANES), -bound, F32))]

        def prep(i, carry):
            rows = pl.ds(pl.multiple_of(i * FOX_TK, FOX_TK), FOX_TK)
            parts = _split3(c_ref[rows, :])
            e = sum(jnp.dot(parts[k], place[k], preferred_element_type=F32)
                    for k in range(AUG_SPLIT))
            kn = _pair_rms(k_ref[rows, :].astype(F32), lo, kg_ref[...])
            v = v_ref[rows, :]
            one = jnp.ones_like(v)
            for h in range(2):
                ones_k = jnp.where(_lanes_in(lane, spare[h] + AUG_CQ)
                                   | _lanes_in(lane, spare[h] + AUG_M), 1.0, 0.0)
                kaug_ref[h, rows, :] = jnp.where(
                    head_lanes[h], kn,
                    jnp.where(_lanes_in(lane, spare[h] + AUG_CK), e, ones_k)).astype(BF16)
                const_q = jnp.where(_lanes_in(lane, spare[h] + AUG_CK), 1.0, 0.0)
                for k in range(AUG_SPLIT):
                    const_q = jnp.where(lane == spare[h] + AUG_M + k, bound_parts[k], const_q)
                caq_ref[h, rows, :] = jnp.where(
                    _lanes_in(lane, spare[h] + AUG_CQ), e, const_q).astype(BF16)
                va = jnp.where(head_lanes[h], v, one)
                vat_ref[h, i] = va.astype(F32).T.astype(BF16)
            return carry
        lax.fori_loop(0, seq // FOX_TK, prep, 0)

    t0 = pl.multiple_of(qi * FOX_TQ, FOX_TQ)
    q_rows = pl.ds(t0, FOX_TQ)
    qn = _pair_rms(q_ref[...].astype(F32), lo, qg_ref[...]) * (FOX_HEAD_DIM ** -0.5)
    kq = lax.broadcasted_iota(jnp.int32, (FOX_TK, FOX_TQ), 0)
    qq = lax.broadcasted_iota(jnp.int32, (FOX_TK, FOX_TQ), 1)

    q_aug = [jnp.where(head_lanes[h], qn, caq_ref[h, q_rows, :].astype(F32)) for h in range(2)]
    for h in range(2):
        qat_ref[h] = q_aug[h].T.astype(BF16)

    @pl.when(bound > FOX_EXACT_ABOVE)
    def _exact_row_max():
        for h in range(2):
            qb_ref[h] = jnp.where(_lanes_in(lane, spare[h] + AUG_M), 0.0, q_aug[h]).astype(BF16)
            m_ref[h] = jnp.full((FOX_TQ, 1), NEG_BIG, F32)

        def scan(j, masked):
            keys = pl.ds(pl.multiple_of(j * FOX_TK, FOX_TK), FOX_TK)
            for h in range(2):
                s = lax.dot_general(qb_ref[h], kaug_ref[h, keys, :], NT_DIMS,
                                    preferred_element_type=F32)
                if masked:
                    s = jnp.where(qq <= kq, s, NEG_BIG)
                m_ref[h] = jnp.maximum(m_ref[h], jnp.max(s, axis=-1, keepdims=True))

        def scan_body(j, carry):
            scan(j, False)
            return carry
        lax.fori_loop(0, qi, scan_body, 0)
        scan(qi, True)
        for h in range(2):
            m_parts = _split3(-m_ref[h])
            qa = q_aug[h]
            for k in range(AUG_SPLIT):
                qa = jnp.where(lane == spare[h] + AUG_M + k, m_parts[k].astype(F32), qa)
            qat_ref[h] = qa.T.astype(BF16)

    acc_ref[...] = jnp.zeros_like(acc_ref)

    def probs(h, keys, q_cols=slice(None)):
        st = jnp.dot(kaug_ref[h, keys, :], qat_ref[h, :, q_cols], preferred_element_type=F32)
        return jnp.exp(st)

    def kv_blocks(blocks):
        for h in range(2):
            total = None
            for j in blocks:
                keys = pl.ds(pl.multiple_of(j * FOX_TK, FOX_TK), FOX_TK)
                o = jnp.dot(vat_ref[h, j], probs(h, keys).astype(BF16), preferred_element_type=F32)
                total = o if total is None else total + o
            acc_ref[h] += total

    def pair(i, carry):
        kv_blocks((2 * i, 2 * i + 1))
        return carry

    lax.fori_loop(0, lax.shift_right_logical(qi, 1), pair, 0)

    @pl.when((qi & 1) == 1)
    def _odd_block():
        kv_blocks((qi - 1,))

    half = FOX_TK // 2
    first, second = slice(0, half), slice(half, FOX_TK)
    causal_a = (lax.broadcasted_iota(jnp.int32, (half, FOX_TQ), 0)
                <= lax.broadcasted_iota(jnp.int32, (half, FOX_TQ), 1))
    causal_b = (lax.broadcasted_iota(jnp.int32, (half, half), 0)
                <= lax.broadcasted_iota(jnp.int32, (half, half), 1))
    for h in range(2):
        p_a = jnp.where(causal_a, probs(h, pl.ds(t0, half)), 0.0)
        p_b = jnp.where(causal_b, probs(h, pl.ds(t0 + half, half), second), 0.0)
        acc_ref[h] += jnp.dot(vat_ref[h, qi, :, first], p_a.astype(BF16),
                              preferred_element_type=F32)
        acc_ref[h, :, second] += jnp.dot(vat_ref[h, qi, :, second], p_b.astype(BF16),
                                         preferred_element_type=F32)

    o0 = acc_ref[0]
    o1 = acc_ref[1]
    l0 = o0[FOX_HEAD_DIM:FOX_HEAD_DIM + 1, :]
    l1 = o1[0:1, :]
    row = lax.broadcasted_iota(jnp.int32, (LANES, 1), 0)
    out_ref[...] = jnp.where(row < FOX_HEAD_DIM, o0 / l0, o1 / l1).T.astype(BF16)


def _fox_attention(proj, cum, bound, qg, kg, batch, seq):
    t = batch * seq
    nq = seq // FOX_TQ
    pairs = FOX_HEADS // 2
    base = 4 * (D_MODEL // LANES)
    return pl.pallas_call(
        _fox_kernel,
        grid=(batch, pairs, nq),
        in_specs=[
            pl.BlockSpec(memory_space=pltpu.SMEM),
            pl.BlockSpec((FOX_TQ, LANES), lambda b, p, i: (b * nq + i, base + p)),
            pl.BlockSpec((seq, LANES), lambda b, p, i: (b, base + pairs + p)),
            pl.BlockSpec((seq, LANES), lambda b, p, i: (b, base + 2 * pairs + p)),
            pl.BlockSpec((seq, LANES), lambda b, p, i: (b, 0)),
            pl.BlockSpec((1, LANES), lambda b, p, i: (0, 0)),
            pl.BlockSpec((1, LANES), lambda b, p, i: (0, 0)),
        ],
        out_specs=pl.BlockSpec((FOX_TQ, LANES), lambda b, p, i: (b * nq + i, p)),
        out_shape=jax.ShapeDtypeStruct((t, FOX_HEADS * FOX_HEAD_DIM), BF16),
        scratch_shapes=[
            pltpu.VMEM((2, seq, LANES), BF16),
            pltpu.VMEM((2, seq, LANES), BF16),
            pltpu.VMEM((2, seq // FOX_TK, LANES, FOX_TK), BF16),
            pltpu.VMEM((2, FOX_TQ, LANES), BF16),
            pltpu.VMEM((2, LANES, FOX_TQ), BF16),
            pltpu.VMEM((2, LANES, FOX_TQ), F32),
            pltpu.VMEM((2, FOX_TQ, 1), F32),
        ],
        compiler_params=_params("parallel", "parallel", "arbitrary"),
        name="fox_attention",
    )(bound, proj, proj, proj, cum, qg, kg)


MEMKV_TM = 512
MEMATT_TQ = 1024


def _mem_kv_kernel(mem_ref, g_ref, w_ref, kg_ref, k_out, v_out):
    hb = _rms_rows(mem_ref[...], g_ref[...]).astype(BF16)
    kv = jnp.dot(hb, w_ref[...], preferred_element_type=F32)
    width = MEM_HEADS * MEM_HEAD_DIM
    for h in range(MEM_HEADS):
        cols = slice(h * MEM_HEAD_DIM, (h + 1) * MEM_HEAD_DIM)
        k_out[:, cols] = _rms_rows(kv[:, cols], kg_ref[...]).astype(BF16)
    v_out[...] = kv[:, width:].astype(BF16)


def _mem_kv(mem2, gain, w_kv, k_gain):
    rows = mem2.shape[0]
    width = MEM_HEADS * MEM_HEAD_DIM
    return pl.pallas_call(
        _mem_kv_kernel,
        grid=(rows // MEMKV_TM,),
        in_specs=[
            pl.BlockSpec((MEMKV_TM, D_MODEL), lambda i: (i, 0)),
            pl.BlockSpec((1, D_MODEL), lambda i: (0, 0)),
            pl.BlockSpec((D_MODEL, 2 * width), lambda i: (0, 0)),
            pl.BlockSpec((1, MEM_HEAD_DIM), lambda i: (0, 0)),
        ],
        out_specs=[pl.BlockSpec((MEMKV_TM, width), lambda i: (i, 0))] * 2,
        out_shape=[jax.ShapeDtypeStruct((rows, width), BF16)] * 2,
        compiler_params=_params("parallel"),
        name="mem_kv",
    )(mem2, gain, w_kv, k_gain)


def _mem_attn_kernel(q_ref, k_ref, v_ref, qg_ref, out_ref):
    for h in range(MEM_HEADS):
        cols = slice(h * MEM_HEAD_DIM, (h + 1) * MEM_HEAD_DIM)
        qn = _rms_rows(q_ref[:, cols].astype(F32), qg_ref[...]) * (MEM_HEAD_DIM ** -0.5)
        s = lax.dot_general(qn.astype(BF16), k_ref[:, cols], NT_DIMS, preferred_element_type=F32)
        p = jnp.exp(s - jnp.max(s, axis=-1, keepdims=True))
        denom = jnp.sum(p, axis=-1, keepdims=True)
        o = jnp.dot(p.astype(BF16), v_ref[:, cols], preferred_element_type=F32)
        out_ref[:, cols] = (o / denom).astype(BF16)


def _mem_attention(proj, mk, mv, q_gain, batch, seq, mem_len):
    t = batch * seq
    nq = seq // MEMATT_TQ
    width = MEM_HEADS * MEM_HEAD_DIM
    return pl.pallas_call(
        _mem_attn_kernel,
        grid=(batch, nq),
        in_specs=[
            pl.BlockSpec((MEMATT_TQ, width), lambda b, i: (b * nq + i, 7)),
            pl.BlockSpec((mem_len, width), lambda b, i: (b, 0)),
            pl.BlockSpec((mem_len, width), lambda b, i: (b, 0)),
            pl.BlockSpec((1, MEM_HEAD_DIM), lambda b, i: (0, 0)),
        ],
        out_specs=pl.BlockSpec((MEMATT_TQ, width), lambda b, i: (b * nq + i, 0)),
        out_shape=jax.ShapeDtypeStruct((t, width), BF16),
        compiler_params=_params("parallel", "parallel"),
        name="mem_attention",
    )(proj, mk, mv, q_gain)


MIX_TM = 512
ROUTER_ROWS = 32
PAIRS = ((0, 1), (0, 2), (0, 3), (1, 2), (1, 3), (2, 3))
PAIRS_PER_GROUP = len(PAIRS)
N_CLASSES = N_GROUPS * PAIRS_PER_GROUP
ROW_WIDTH = D_MODEL + LANES


def _top2_sum(b0, b1, b2, b3):
    p, q = jnp.maximum(b0, b1), jnp.minimum(b0, b1)
    r, s = jnp.maximum(b2, b3), jnp.minimum(b2, b3)
    return jnp.maximum(p, r) + jnp.maximum(jnp.minimum(p, r), jnp.maximum(q, s))


def _mix_kernel(ro_ref, fo_ref, mo_ref, gr_ref, gf_ref, gm_ref, x_ref,
                wr_ref, wf_ref, wm_ref, wo_ref, nf_ref, rw_ref, rb_ref,
                xa_ref, cls_ref):
    def branch(a_ref, w_ref, g_ref):
        y = jnp.dot(a_ref[...], w_ref[...], preferred_element_type=F32)
        return jax.nn.sigmoid(g_ref[...].astype(F32)) * y

    merged = (branch(ro_ref, wr_ref, gr_ref) + branch(fo_ref, wf_ref, gf_ref)
              + branch(mo_ref, wm_ref, gm_ref))
    xn = x_ref[...] + jnp.dot(merged.astype(BF16), wo_ref[...], preferred_element_type=F32)
    xa_ref[:, :D_MODEL] = xn
    h = _rms_rows(xn, nf_ref[...])
    h_hi = h.astype(BF16)

    h_lo = (h - h_hi.astype(F32)).astype(BF16)
    rw = rw_ref[...]
    rw_hi = rw.astype(BF16)
    rw_lo = (rw - rw_hi.astype(F32)).astype(BF16)
    dot_nt = lambda a, b: lax.dot_general(a, b, NT_DIMS, preferred_element_type=F32)
    logits = dot_nt(rw_hi, h_hi) + dot_nt(rw_hi, h_lo) + dot_nt(rw_lo, h_hi)
    scores = jax.nn.sigmoid(logits)
    biased = scores + rb_ref[...]
    n = EXPERTS_PER_GROUP
    sc = [scores[SUBLANES * j:SUBLANES * (j + 1)] for j in range(n)]
    bi = [biased[SUBLANES * j:SUBLANES * (j + 1)] for j in range(n)]
    group_score = _top2_sum(*bi)
    rows = lax.broadcasted_iota(jnp.int32, group_score.shape, 0)
    best = jnp.max(group_score, axis=0, keepdims=True)
    top_group = jnp.min(jnp.where(group_score == best, rows, SUBLANES), axis=0, keepdims=True)
    in_group = rows == top_group
    picked, chosen = [], []
    for j in range(n):
        rank = jnp.zeros(group_score.shape, jnp.int32)
        for i in range(n):
            if i == j:
                continue
            ahead = (bi[i] > bi[j]) | ((bi[i] == bi[j]) & (i < j))
            rank = rank + ahead.astype(jnp.int32)
        sel = in_group & (rank < 2)
        picked.append(jnp.where(sel, sc[j], 0.0))
        chosen.append(jnp.max(jnp.where(sel, 1.0, 0.0), axis=0, keepdims=True) > 0.5)
    denom = jnp.sum(picked[0] + picked[1] + picked[2] + picked[3], axis=0, keepdims=True)
    gate = [jnp.sum(picked[j], axis=0, keepdims=True) / denom for j in range(n)]

    first = jnp.full(top_group.shape, n, jnp.int32)
    second = jnp.full(top_group.shape, -1, jnp.int32)
    for j in range(n):
        first = jnp.minimum(first, jnp.where(chosen[j], j, n))
        second = jnp.maximum(second, jnp.where(chosen[j], j, -1))
    pair_base = jnp.where(first == 0, 0, jnp.where(first == 1, 3, 5))
    cls = top_group * PAIRS_PER_GROUP + pair_base + second - first - 1
    cls_ref[...] = jnp.clip(cls, 0, N_CLASSES - 1)
    g_first = sum(jnp.where(first == j, gate[j], 0.0) for j in range(n))
    g_second = sum(jnp.where(second == j, gate[j], 0.0) for j in range(n))
    srow = lax.broadcasted_iota(jnp.int32, (SUBLANES, 1), 0)
    gates8 = jnp.where(srow == 0, g_first, jnp.where(srow == 1, g_second, 0.0))
    pad = jnp.zeros((LANES - SUBLANES, gates8.shape[1]), F32)
    xa_ref[:, D_MODEL:] = jnp.concatenate([gates8, pad], axis=0).T


def _mix_and_route(ro, fo, mo, proj, x2, w_r, w_f, w_m, w_o, norm_ffn, rw_pad, rb_pad):
    t = x2.shape[0]
    tok = lambda i: (i, 0)
    const = lambda i: (0, 0)
    act = pl.BlockSpec((MIX_TM, D_MODEL), tok)
    wspec = pl.BlockSpec((D_MODEL, D_MODEL), const)
    return pl.pallas_call(
        _mix_kernel,
        grid=(t // MIX_TM,),
        in_specs=[
            act, act, act,
            pl.BlockSpec((MIX_TM, D_MODEL), lambda i: (i, 8)),
            pl.BlockSpec((MIX_TM, D_MODEL), lambda i: (i, 9)),
            pl.BlockSpec((MIX_TM, D_MODEL), lambda i: (i, 10)),
            act,
            wspec, wspec, wspec, wspec,
            pl.BlockSpec((1, D_MODEL), const),
            pl.BlockSpec((ROUTER_ROWS, D_MODEL), const),
            pl.BlockSpec((ROUTER_ROWS, 1), const),
        ],
        out_specs=[
            pl.BlockSpec((MIX_TM, ROW_WIDTH), tok),
            pl.BlockSpec((1, MIX_TM), lambda i: (0, i)),
        ],
        out_shape=[
            jax.ShapeDtypeStruct((t, ROW_WIDTH), F32),
            jax.ShapeDtypeStruct((1, t), jnp.int32),
        ],
        compiler_params=_params("parallel"),
        name="mix_and_route",
    )(ro, fo, mo, proj, proj, proj, x2, w_r, w_f, w_m, w_o, norm_ffn, rw_pad, rb_pad)


EXP_TR = 256
POS_TM = 512
MOVE_TM = 2048
CLASS_ROWS = 32


def _position_kernel(cls_ref, pos_ref, offs_ref, size_ref, counts_ref, running_ref):
    phase = pl.program_id(0)
    i = pl.program_id(1)
    rows = lax.broadcasted_iota(jnp.int32, (CLASS_ROWS, POS_TM), 0)
    onehot = jnp.where(rows == cls_ref[...], 1.0, 0.0)
    per_class = jnp.sum(onehot, axis=1, keepdims=True)

    @pl.when(phase == 0)
    def _count():
        @pl.when(i == 0)
        def _():
            counts_ref[...] = jnp.zeros_like(counts_ref)
        counts_ref[...] += per_class
        pos_ref[...] = jnp.zeros_like(pos_ref)

    @pl.when(phase == 1)
    def _place():
        @pl.when(i == 0)
        def _():
            padded = jnp.floor((counts_ref[...] + (EXP_TR - 1)) * (1.0 / EXP_TR)) * EXP_TR
            r = lax.broadcasted_iota(jnp.int32, (CLASS_ROWS, CLASS_ROWS), 0)
            c = lax.broadcasted_iota(jnp.int32, (CLASS_ROWS, CLASS_ROWS), 1)
            below = jnp.where(c < r, 1.0, 0.0)
            offs = jnp.dot(below, padded, preferred_element_type=F32,
                           precision=lax.Precision.HIGHEST)
            offs_ref[...] = offs
            size_ref[...] = padded
            running_ref[...] = offs

        r = lax.broadcasted_iota(jnp.int32, (POS_TM, POS_TM), 0)
        c = lax.broadcasted_iota(jnp.int32, (POS_TM, POS_TM), 1)
        earlier = jnp.where(r < c, 1.0, 0.0).astype(BF16)
        prefix = jnp.dot(onehot.astype(BF16), earlier, preferred_element_type=F32)
        place = prefix + running_ref[:, 0:1]
        pos_ref[...] = jnp.sum(onehot * place, axis=0, keepdims=True).astype(jnp.int32)
        running_ref[...] += per_class


def _positions(cls):
    t = cls.shape[1]
    meta = jax.ShapeDtypeStruct((CLASS_ROWS, LANES), F32)
    return pl.pallas_call(
        _position_kernel,
        grid=(2, t // POS_TM),
        in_specs=[pl.BlockSpec((1, POS_TM), lambda p, i: (0, i))],
        out_specs=[
            pl.BlockSpec((1, POS_TM), lambda p, i: (0, i * p)),
            pl.BlockSpec((CLASS_ROWS, LANES), lambda p, i: (0, 0)),
            pl.BlockSpec((CLASS_ROWS, LANES), lambda p, i: (0, 0)),
        ],
        out_shape=[jax.ShapeDtypeStruct((1, t), jnp.int32), meta, meta],
        scratch_shapes=[pltpu.VMEM((CLASS_ROWS, LANES), F32), pltpu.VMEM((CLASS_ROWS, LANES), F32)],
        compiler_params=_params("arbitrary", "arbitrary"),
        name="moe_positions",
    )(cls)


def _move_rows(copy_for_row):
    def start(r, carry):
        copy_for_row(r).start()
        return carry

    def wait(r, carry):
        copy_for_row(r).wait()
        return carry

    lax.fori_loop(0, MOVE_TM, start, 0, unroll=8)
    lax.fori_loop(0, MOVE_TM, wait, 0, unroll=8)


def _scatter_rows_kernel(pos_ref, src_ref, init_ref, dst_ref, sem):
    del init_ref
    base = pl.program_id(0) * MOVE_TM
    _move_rows(lambda r: pltpu.make_async_copy(
        src_ref.at[pl.ds(base + r, 1)], dst_ref.at[pl.ds(pos_ref[r], 1)], sem))


def _scatter_rows(pos, src, init):
    t = src.shape[0]
    return pl.pallas_call(
        _scatter_rows_kernel,
        grid=(t // MOVE_TM,),
        in_specs=[
            pl.BlockSpec((MOVE_TM,), lambda i: (i,), memory_space=pltpu.SMEM),
            pl.BlockSpec(memory_space=pl.ANY),
            pl.BlockSpec(memory_space=pl.ANY),
        ],
        out_specs=pl.BlockSpec(memory_space=pl.ANY),
        out_shape=jax.ShapeDtypeStruct(init.shape, init.dtype),
        scratch_shapes=[pltpu.SemaphoreType.DMA(())],
        input_output_aliases={2: 0},
        compiler_params=_params("arbitrary"),
        name="moe_scatter_rows",
    )(pos, src, init)


def _gather_rows_kernel(pos_ref, src_ref, dst_ref, sem):
    base = pl.program_id(0) * MOVE_TM
    _move_rows(lambda r: pltpu.make_async_copy(
        src_ref.at[pl.ds(pos_ref[r], 1)], dst_ref.at[pl.ds(base + r, 1)], sem))


def _gather_rows(pos, src):
    t = pos.shape[0]
    return pl.pallas_call(
        _gather_rows_kernel,
        grid=(t // MOVE_TM,),
        in_specs=[
            pl.BlockSpec((MOVE_TM,), lambda i: (i,), memory_space=pltpu.SMEM),
            pl.BlockSpec(memory_space=pl.ANY),
        ],
        out_specs=pl.BlockSpec(memory_space=pl.ANY),
        out_shape=jax.ShapeDtypeStruct((t, src.shape[1]), src.dtype),
        scratch_shapes=[pltpu.SemaphoreType.DMA(())],
        compiler_params=_params("arbitrary"),
        name="moe_gather_rows",
    )(pos, src)


def _experts_kernel(ea_ref, eb_ref, used_ref, xs_ref, nf_ref,
                    wga_ref, wua_ref, wda_ref, wgb_ref, wub_ref, wdb_ref, ys_ref):
    del ea_ref, eb_ref
    k = pl.program_id(0)

    @pl.when(k < used_ref[0])
    def _():
        x = xs_ref[:, :D_MODEL]
        gates = xs_ref[:, D_MODEL:]
        h = _rms_rows(x, nf_ref[...]).astype(BF16)

        def ffn(wg_ref, wu_ref, wd_ref):
            a = jnp.dot(h, wg_ref[...], preferred_element_type=F32)
            u = jnp.dot(h, wu_ref[...], preferred_element_type=F32)
            return jnp.dot((a * jax.nn.sigmoid(a) * u).astype(BF16), wd_ref[...],
                           preferred_element_type=F32)

        moe = (gates[:, 0:1] * ffn(wga_ref, wua_ref, wda_ref)
               + gates[:, 1:2] * ffn(wgb_ref, wub_ref, wdb_ref))
        ys_ref[...] = x + moe

    @pl.when(k >= used_ref[0])
    def _():
        ys_ref[...] = jnp.zeros_like(ys_ref)


def _experts(xs, norm_ffn, w_gate, w_up, w_down, expert_a, expert_b, n_used):
    n_tiles = xs.shape[0] // EXP_TR
    up = lambda sel: pl.BlockSpec((None, D_MODEL, D_FF), lambda k, ea, eb, nu: (sel(ea, eb)[k], 0, 0))
    down = lambda sel: pl.BlockSpec((None, D_FF, D_MODEL), lambda k, ea, eb, nu: (sel(ea, eb)[k], 0, 0))
    first = lambda ea, eb: ea
    second = lambda ea, eb: eb
    grid_spec = pltpu.PrefetchScalarGridSpec(
        num_scalar_prefetch=3,
        grid=(n_tiles,),
        in_specs=[
            pl.BlockSpec((EXP_TR, ROW_WIDTH), lambda k, ea, eb, nu: (k, 0)),
            pl.BlockSpec((1, D_MODEL), lambda k, ea, eb, nu: (0, 0)),
            up(first), up(first), down(first), up(second), up(second), down(second),
        ],
        out_specs=pl.BlockSpec((EXP_TR, D_MODEL), lambda k, ea, eb, nu: (k, 0)),
    )
    return pl.pallas_call(
        _experts_kernel,
        grid_spec=grid_spec,
        out_shape=jax.ShapeDtypeStruct((xs.shape[0], D_MODEL), F32),
        compiler_params=_params("arbitrary"),
        name="experts",
    )(expert_a, expert_b, n_used, xs, norm_ffn, w_gate, w_up, w_down, w_gate, w_up, w_down)


def _tile_experts(offs, sizes, n_tiles):
    ends = (offs[:N_CLASSES, 0] + sizes[:N_CLASSES, 0]).astype(jnp.int32)
    starts = jnp.arange(n_tiles, dtype=jnp.int32) * EXP_TR
    tile_cls = jnp.minimum(jnp.searchsorted(ends, starts, side="right"), N_CLASSES - 1)
    group = tile_cls // PAIRS_PER_GROUP
    pair = tile_cls % PAIRS_PER_GROUP
    slot_a = jnp.array([a for a, _ in PAIRS], jnp.int32)[pair]
    slot_b = jnp.array([b for _, b in PAIRS], jnp.int32)[pair]
    expert_a = (group * EXPERTS_PER_GROUP + slot_a).astype(jnp.int32)
    expert_b = (group * EXPERTS_PER_GROUP + slot_b).astype(jnp.int32)
    return expert_a, expert_b, (ends[-1] // EXP_TR).reshape(1)


def _router_layout(router_w, router_bias):
    w = router_w.T.reshape(N_GROUPS, EXPERTS_PER_GROUP, D_MODEL)
    w = jnp.transpose(w, (1, 0, 2))
    w = jnp.pad(w, ((0, 0), (0, SUBLANES - N_GROUPS), (0, 0))).reshape(ROUTER_ROWS, D_MODEL)
    b = jnp.transpose(router_bias.reshape(N_GROUPS, EXPERTS_PER_GROUP), (1, 0))
    b = jnp.pad(b, ((0, 0), (0, SUBLANES - N_GROUPS)), constant_values=NEG_BIG)
    return w.astype(F32), b.reshape(ROUTER_ROWS, 1).astype(F32)


def kernel(x, mem, norm_mix, norm_mem, w_in, b_forget, fox_q_norm, fox_k_norm, mem_q_norm,
           mem_k_norm, w_mem_kv, w_o_ret, w_o_fox, w_o_mem, w_out, norm_ffn, router_w,
           router_bias, w_gate, w_up, w_down):
    batch, seq, d = x.shape
    mem_len = mem.shape[1]
    depth = w_in.shape[0]
    t = batch * seq
    ff_lo = 7 * D_MODEL
    ff_hi = ff_lo + FOX_HEADS

    cos, sin, intra, qd, kd, cd = _retention_tables(seq)
    rw_pad, rb_pad = _router_layout(router_w, router_bias)
    row = lambda v: v.reshape(1, -1).astype(F32)

    x2 = x.reshape(t, d)
    mem2 = mem.reshape(batch * mem_len, d)
    n_tiles = t // EXP_TR + N_CLASSES
    sorted_rows = jnp.zeros((n_tiles * EXP_TR, ROW_WIDTH), F32)
    for l in range(depth):
        w_main = jnp.concatenate([w_in[l, :, :ff_lo], w_in[l, :, ff_hi:]], axis=1).astype(BF16)
        w_ff = jnp.pad(w_in[l, :, ff_lo:ff_hi], ((0, 0), (0, LANES - FOX_HEADS))).astype(BF16)
        b_pad = jnp.pad(b_forget[l], (0, LANES - FOX_HEADS)).reshape(1, LANES)

        proj, ff = _in_projection(x2, row(norm_mix[l]), w_main, w_ff)
        cum = _forget_cumsum(ff, b_pad, batch, seq)
        ro = _retention(proj, cos, sin, intra, qd, kd, cd, batch, seq)
        bound = (8.16 * jnp.max(jnp.abs(fox_q_norm[l])) * jnp.max(jnp.abs(fox_k_norm[l])))
        fo = _fox_attention(proj, cum, bound.reshape(1).astype(F32),
                            row(jnp.tile(fox_q_norm[l], 2)), row(jnp.tile(fox_k_norm[l], 2)),
                            batch, seq)
        mk, mv = _mem_kv(mem2, row(norm_mem[l]), w_mem_kv[l].astype(BF16), row(mem_k_norm[l]))
        mo = _mem_attention(proj, mk, mv, row(mem_q_norm[l]), batch, seq, mem_len)
        xa, cls = _mix_and_route(
            ro, fo, mo, proj, x2, w_o_ret[l].astype(BF16), w_o_fox[l].astype(BF16),
            w_o_mem[l].astype(BF16), w_out[l].astype(BF16), row(norm_ffn[l]), rw_pad, rb_pad)
        pos, offs, sizes = _positions(cls)
        pos = pos.reshape(t)
        expert_a, expert_b, n_used = _tile_experts(offs, sizes, n_tiles)
        sorted_rows = _scatter_rows(pos, xa, sorted_rows)
        ys = _experts(sorted_rows, row(norm_ffn[l]), w_gate[l].astype(BF16), w_up[l].astype(BF16),
                      w_down[l].astype(BF16), expert_a, expert_b, n_used)
        x2 = _gather_rows(pos, ys)
    return x2.reshape(batch, seq, d)
```

```python
import functools

import jax
import jax.numpy as jnp
from jax import lax
from jax.experimental import pallas as pl
from jax.experimental.pallas import tpu as pltpu

F32 = jnp.float32
BF16 = jnp.bfloat16

D_MODEL = 1024
EPS = 1e-6
RET_HEADS = 4
RET_HEAD_DIM = 256
RET_CHUNK = 128
ROPE_BASE = 10000.0
FOX_HEADS = 16
FOX_HEAD_DIM = 64
MEM_HEADS = 4
MEM_HEAD_DIM = 256
N_EXPERTS = 16
N_GROUPS = 4
EXPERTS_PER_GROUP = 4
D_FF = 512

LANES = 128
SUBLANES = 8
VMEM_LIMIT = 56 * 1024 * 1024
NEG_BIG = -1e30

NT_DIMS = (((1,), (1,)), ((), ()))


def _params(*sem):
    return pltpu.CompilerParams(dimension_semantics=sem, vmem_limit_bytes=VMEM_LIMIT)


def _rms_rows(x, gain_row):
    ms = jnp.mean(x * x, axis=-1, keepdims=True)
    return x * lax.rsqrt(ms + EPS) * gain_row


IN_TM = 2048
IN_BLOCKS = 11


def _inproj_kernel(x_ref, g_ref, w_ref, wff_ref, out_ref, ff_ref, hn_ref):
    @pl.when(pl.program_id(1) == 0)
    def _():
        hb = _rms_rows(x_ref[...], g_ref[...]).astype(BF16)
        hn_ref[...] = hb
        ff_ref[...] = jnp.dot(hb, wff_ref[...], preferred_element_type=F32)

    out_ref[...] = jnp.dot(hn_ref[...], w_ref[...], preferred_element_type=F32).astype(BF16)


def _in_projection(x2, gain, w_main, w_ff):
    t = x2.shape[0]
    return pl.pallas_call(
        _inproj_kernel,
        grid=(t // IN_TM, IN_BLOCKS),
        in_specs=[
            pl.BlockSpec((IN_TM, D_MODEL), lambda i, j: (i, 0)),
            pl.BlockSpec((1, D_MODEL), lambda i, j: (0, 0)),
            pl.BlockSpec((D_MODEL, D_MODEL), lambda i, j: (0, j)),
            pl.BlockSpec((D_MODEL, LANES), lambda i, j: (0, 0)),
        ],
        out_specs=[
            pl.BlockSpec((IN_TM, D_MODEL), lambda i, j: (i, j)),
            pl.BlockSpec((IN_TM, LANES), lambda i, j: (i, 0)),
        ],
        out_shape=[
            jax.ShapeDtypeStruct((t, IN_BLOCKS * D_MODEL), BF16),
            jax.ShapeDtypeStruct((t, LANES), F32),
        ],
        scratch_shapes=[pltpu.VMEM((IN_TM, D_MODEL), BF16)],
        compiler_params=_params("parallel", "arbitrary"),
        name="in_projection",
    )(x2, gain, w_main, w_ff)


CUM_BLOCK = 512


def _cumsum_kernel(ff_ref, b_ref, out_ref):
    s = ff_ref.shape[0]
    r = lax.broadcasted_iota(jnp.int32, (CUM_BLOCK, CUM_BLOCK), 0)
    c = lax.broadcasted_iota(jnp.int32, (CUM_BLOCK, CUM_BLOCK), 1)
    lower = (c <= r).astype(F32)
    carry = jnp.zeros((1, LANES), F32)
    for blk in range(s // CUM_BLOCK):
        rows = slice(blk * CUM_BLOCK, (blk + 1) * CUM_BLOCK)
        z = ff_ref[rows, :] + b_ref[...]
        log_f = jnp.minimum(z, 0.0) - jnp.log1p(jnp.exp(-jnp.abs(z)))
        cs = jnp.dot(lower, log_f, preferred_element_type=F32,
                     precision=lax.Precision.HIGHEST) + carry
        out_ref[rows, :] = cs
        carry = cs[CUM_BLOCK - 1:CUM_BLOCK, :]


def _forget_cumsum(ff, b_pad, batch, seq):
    return pl.pallas_call(
        _cumsum_kernel,
        grid=(batch,),
        in_specs=[
            pl.BlockSpec((seq, LANES), lambda b: (b, 0)),
            pl.BlockSpec((1, LANES), lambda b: (0, 0)),
        ],
        out_specs=pl.BlockSpec((seq, LANES), lambda b: (b, 0)),
        out_shape=jax.ShapeDtypeStruct((batch * seq, LANES), F32),
        compiler_params=_params("parallel"),
        name="forget_cumsum",
    )(ff, b_pad)


def _retention_kernel(q_ref, k_ref, v_ref, g_ref, cos_ref, sin_ref, intra_ref, qd_ref, kd_ref,
                      cd_ref, out_ref, state_ref):
    seq = q_ref.shape[0]
    half = RET_HEAD_DIM // 2
    state_ref[...] = jnp.zeros_like(state_ref)

    def chunk(c, carry):
        r0 = pl.multiple_of(c * RET_CHUNK, RET_CHUNK)
        rows = pl.ds(r0, RET_CHUNK)
        cs = cos_ref[rows, :]
        sn = sin_ref[rows, :]

        def rot(x):
            x1 = x[:, :half]
            x2 = x[:, half:]
            return jnp.concatenate([x1 * cs - x2 * sn, x2 * cs + x1 * sn], axis=-1)

        qr = rot(q_ref[rows, :].astype(F32))
        kr = rot(k_ref[rows, :].astype(F32))
        v = v_ref[rows, :]
        scores = lax.dot_general(qr.astype(BF16), kr.astype(BF16), NT_DIMS,
                                 preferred_element_type=F32) * intra_ref[...]
        st = state_ref[...]
        o = (jnp.dot(scores.astype(BF16), v, preferred_element_type=F32)
             + jnp.dot((qr * qd_ref[...]).astype(BF16), st.astype(BF16),
                       preferred_element_type=F32))
        kd_t = (kr * kd_ref[...]).T.astype(BF16)
        state_ref[...] = st * cd_ref[...] + jnp.dot(kd_t, v, preferred_element_type=F32)
        on = o * lax.rsqrt(jnp.mean(o * o, axis=-1, keepdims=True) + EPS)
        g = g_ref[rows, :].astype(F32)
        out_ref[rows, :] = (on * (g * jax.nn.sigmoid(g))).astype(BF16)
        return carry

    lax.fori_loop(0, seq // RET_CHUNK, chunk, 0)


def _retention(proj, cos, sin, intra, qd, kd, cd, batch, seq):
    t = batch * seq
    hd = RET_HEAD_DIM
    col = lambda off: (lambda b, h: (b, off + h))
    tab = lambda b, h: (h, 0, 0)
    return pl.pallas_call(
        _retention_kernel,
        grid=(batch, RET_HEADS),
        in_specs=[
            pl.BlockSpec((seq, hd), col(0)),
            pl.BlockSpec((seq, hd), col(RET_HEADS)),
            pl.BlockSpec((seq, hd), col(2 * RET_HEADS)),
            pl.BlockSpec((seq, hd), col(3 * RET_HEADS)),
            pl.BlockSpec((seq, hd // 2), lambda b, h: (0, 0)),
            pl.BlockSpec((seq, hd // 2), lambda b, h: (0, 0)),
            pl.BlockSpec((None, RET_CHUNK, RET_CHUNK), tab),
            pl.BlockSpec((None, RET_CHUNK, hd), tab),
            pl.BlockSpec((None, RET_CHUNK, hd), tab),
            pl.BlockSpec((None, 1, hd), tab),
        ],
        out_specs=pl.BlockSpec((seq, hd), lambda b, h: (b, h)),
        out_shape=jax.ShapeDtypeStruct((t, RET_HEADS * hd), BF16),
        scratch_shapes=[pltpu.VMEM((hd, hd), F32)],
        compiler_params=_params("parallel", "parallel"),
        name="retention",
    )(proj, proj, proj, proj, cos, sin, intra, qd, kd, cd)


def _retention_tables(seq):
    h = jnp.arange(RET_HEADS, dtype=F32)
    log_gamma = jnp.log1p(-(2.0 ** (-5.0 - h)))
    idx = jnp.arange(RET_CHUNK, dtype=F32)
    diff = idx[:, None] - idx[None, :]
    scale = RET_HEAD_DIM ** -0.5
    intra = jnp.where(diff >= 0, jnp.exp(log_gamma[:, None, None] * jnp.maximum(diff, 0.0)), 0.0) * scale
    q_decay = jnp.exp(log_gamma[:, None] * (idx + 1.0))
    k_decay = jnp.exp(log_gamma[:, None] * (RET_CHUNK - 1.0 - idx)) * scale
    chunk_decay = jnp.exp(log_gamma * RET_CHUNK)
    qd = jnp.broadcast_to(q_decay[:, :, None], (RET_HEADS, RET_CHUNK, RET_HEAD_DIM))
    kd = jnp.broadcast_to(k_decay[:, :, None], (RET_HEADS, RET_CHUNK, RET_HEAD_DIM))
    cd = jnp.broadcast_to(chunk_decay[:, None, None], (RET_HEADS, 1, RET_HEAD_DIM))
    inv_freq = ROPE_BASE ** (-jnp.arange(0, RET_HEAD_DIM, 2, dtype=F32) / RET_HEAD_DIM)
    ang = jnp.arange(seq, dtype=F32)[:, None] * inv_freq[None, :]
    return jnp.cos(ang), jnp.sin(ang), intra, qd, kd, cd


FOX_TQ = 512
FOX_TK = 512
assert FOX_TQ == FOX_TK


def _pair_rms(x, lo_mask, gain_row):
    x2 = x * x
    s_all = jnp.sum(x2, axis=-1, keepdims=True)
    s_lo = jnp.sum(jnp.where(lo_mask, x2, 0.0), axis=-1, keepdims=True)
    ms = jnp.where(lo_mask, s_lo, s_all - s_lo) * (1.0 / FOX_HEAD_DIM)
    return x * lax.rsqrt(ms + EPS) * gain_row


AUG_CQ = 0
AUG_CK = 3
AUG_M = 6
AUG_SPLIT = 3
FOX_EXACT_ABOVE = 30.0


def _split3(x):
    hi = x.astype(BF16)
    r1 = x - hi.astype(F32)
    mid = r1.astype(BF16)
    lo = (r1 - mid.astype(F32)).astype(BF16)
    return hi, mid, lo


def _lanes_in(lane, start, count=AUG_SPLIT):
    return (lane >= start) & (lane < start + count)


def _fox_kernel(bound_ref, q_ref, k_ref, v_ref, c_ref, qg_ref, kg_ref, out_ref,
                kaug_ref, caq_ref, vat_ref, qb_ref, qat_ref, acc_ref, m_ref):
    seq = k_ref.shape[0]
    hp = pl.program_id(1)
    qi = pl.program_id(2)
    lane = lax.broadcasted_iota(jnp.int32, (1, LANES), 1)
    lo = lane < FOX_HEAD_DIM
    head_lanes = (lo, jnp.logical_not(lo))
    spare = (FOX_HEAD_DIM, 0)
    bound = bound_ref[0]

    @pl.when(qi == 0)
    def _prepare_keys_values():
        r = lax.broadcasted_iota(jnp.int32, (LANES, LANES), 0)
        c = lax.broadcasted_iota(jnp.int32, (LANES, LANES), 1)
        place = []
        for k in range(AUG_SPLIT):
            pk = jnp.zeros((LANES, LANES), F32)
            for h in range(2):
                src = r == 2 * hp + h
                pk = pk + jnp.where(src & (c == spare[h] + AUG_CQ + k), 1.0, 0.0)
                pk = pk - jnp.where(src & (c == spare[h] + AUG_CK + k), 1.0, 0.0)
            place.append(pk.astype(BF16))
        bound_parts = [p.astype(F32) for p in _split3(jnp.full((1, LANES), -bound, F32))]

        def prep(i, carry):
            rows = pl.ds(pl.multiple_of(i * FOX_TK, FOX_TK), FOX_TK)
            parts = _split3(c_ref[rows, :])
            e = sum(jnp.dot(parts[k], place[k], preferred_element_type=F32)
                    for k in range(AUG_SPLIT))
            kn = _pair_rms(k_ref[rows, :].astype(F32), lo, kg_ref[...])
            v = v_ref[rows, :]
            one = jnp.ones_like(v)
            for h in range(2):
                ones_k = jnp.where(_lanes_in(lane, spare[h] + AUG_CQ)
                                   | _lanes_in(lane, spare[h] + AUG_M), 1.0, 0.0)
                kaug_ref[h, rows, :] = jnp.where(
                    head_lanes[h], kn,
                    jnp.where(_lanes_in(lane, spare[h] + AUG_CK), e, ones_k)).astype(BF16)
                const_q = jnp.where(_lanes_in(lane, spare[h] + AUG_CK), 1.0, 0.0)
                for k in range(AUG_SPLIT):
                    const_q = jnp.where(lane == spare[h] + AUG_M + k, bound_parts[k], const_q)
                caq_ref[h, rows, :] = jnp.where(
                    _lanes_in(lane, spare[h] + AUG_CQ), e, const_q).astype(BF16)
                va = jnp.where(head_lanes[h], v, one)
                vat_ref[h, i] = va.astype(F32).T.astype(BF16)
            return carry
        lax.fori_loop(0, seq // FOX_TK, prep, 0)

    t0 = pl.multiple_of(qi * FOX_TQ, FOX_TQ)
    q_rows = pl.ds(t0, FOX_TQ)
    qn = _pair_rms(q_ref[...].astype(F32), lo, qg_ref[...]) * (FOX_HEAD_DIM ** -0.5)
    kq = lax.broadcasted_iota(jnp.int32, (FOX_TK, FOX_TQ), 0)
    qq = lax.broadcasted_iota(jnp.int32, (FOX_TK, FOX_TQ), 1)

    q_aug = [jnp.where(head_lanes[h], qn, caq_ref[h, q_rows, :].astype(F32)) for h in range(2)]
    for h in range(2):
        qat_ref[h] = q_aug[h].T.astype(BF16)

    @pl.when(bound > FOX_EXACT_ABOVE)
    def _exact_row_max():
        for h in range(2):
            qb_ref[h] = jnp.where(_lanes_in(lane, spare[h] + AUG_M), 0.0, q_aug[h]).astype(BF16)
            m_ref[h] = jnp.full((FOX_TQ, 1), NEG_BIG, F32)

        def scan(j, masked):
            keys = pl.ds(pl.multiple_of(j * FOX_TK, FOX_TK), FOX_TK)
            for h in range(2):
                s = lax.dot_general(qb_ref[h], kaug_ref[h, keys, :], NT_DIMS,
                                    preferred_element_type=F32)
                if masked:
                    s = jnp.where(qq <= kq, s, NEG_BIG)
                m_ref[h] = jnp.maximum(m_ref[h], jnp.max(s, axis=-1, keepdims=True))

        def scan_body(j, carry):
            scan(j, False)
            return carry
        lax.fori_loop(0, qi, scan_body, 0)
        scan(qi, True)
        for h in range(2):
            m_parts = _split3(-m_ref[h])
            qa = q_aug[h]
            for k in range(AUG_SPLIT):
                qa = jnp.where(lane == spare[h] + AUG_M + k, m_parts[k].astype(F32), qa)
            qat_ref[h] = qa.T.astype(BF16)

    acc_ref[...] = jnp.zeros_like(acc_ref)

    def probs(h, keys, q_cols=slice(None)):
        st = jnp.dot(kaug_ref[h, keys, :], qat_ref[h, :, q_cols], preferred_element_type=F32)
        return jnp.exp(st)

    def kv_blocks(blocks):
        for h in range(2):
            total = None
            for j in blocks:
                keys = pl.ds(pl.multiple_of(j * FOX_TK, FOX_TK), FOX_TK)
                o = jnp.dot(vat_ref[h, j], probs(h, keys).astype(BF16), preferred_element_type=F32)
                total = o if total is None else total + o
            acc_ref[h] += total

    def pair(i, carry):
        kv_blocks((2 * i, 2 * i + 1))
        return carry

    lax.fori_loop(0, lax.shift_right_logical(qi, 1), pair, 0)

    @pl.when((qi & 1) == 1)
    def _odd_block():
        kv_blocks((qi - 1,))

    half = FOX_TK // 2
    first, second = slice(0, half), slice(half, FOX_TK)
    causal_a = (lax.broadcasted_iota(jnp.int32, (half, FOX_TQ), 0)
                <= lax.broadcasted_iota(jnp.int32, (half, FOX_TQ), 1))
    causal_b = (lax.broadcasted_iota(jnp.int32, (half, half), 0)
                <= lax.broadcasted_iota(jnp.int32, (half, half), 1))
    for h in range(2):
        p_a = jnp.where(causal_a, probs(h, pl.ds(t0, half)), 0.0)
        p_b = jnp.where(causal_b, probs(h, pl.ds(t0 + half, half), second), 0.0)
        acc_ref[h] += jnp.dot(vat_ref[h, qi, :, first], p_a.astype(BF16),
                              preferred_element_type=F32)
        acc_ref[h, :, second] += jnp.dot(vat_ref[h, qi, :, second], p_b.astype(BF16),
                                         preferred_element_type=F32)

    o0 = acc_ref[0]
    o1 = acc_ref[1]
    l0 = o0[FOX_HEAD_DIM:FOX_HEAD_DIM + 1, :]
    l1 = o1[0:1, :]
    row = lax.broadcasted_iota(jnp.int32, (LANES, 1), 0)
    out_ref[...] = jnp.where(row < FOX_HEAD_DIM, o0 / l0, o1 / l1).T.astype(BF16)


def _fox_attention(proj, cum, bound, qg, kg, batch, seq):
    t = batch * seq
    nq = seq // FOX_TQ
    pairs = FOX_HEADS // 2
    base = 4 * (D_MODEL // LANES)
    return pl.pallas_call(
        _fox_kernel,
        grid=(batch, pairs, nq),
        in_specs=[
            pl.BlockSpec(memory_space=pltpu.SMEM),
            pl.BlockSpec((FOX_TQ, LANES), lambda b, p, i: (b * nq + i, base + p)),
            pl.BlockSpec((seq, LANES), lambda b, p, i: (b, base + pairs + p)),
            pl.BlockSpec((seq, LANES), lambda b, p, i: (b, base + 2 * pairs + p)),
            pl.BlockSpec((seq, LANES), lambda b, p, i: (b, 0)),
            pl.BlockSpec((1, LANES), lambda b, p, i: (0, 0)),
            pl.BlockSpec((1, LANES), lambda b, p, i: (0, 0)),
        ],
        out_specs=pl.BlockSpec((FOX_TQ, LANES), lambda b, p, i: (b * nq + i, p)),
        out_shape=jax.ShapeDtypeStruct((t, FOX_HEADS * FOX_HEAD_DIM), BF16),
        scratch_shapes=[
            pltpu.VMEM((2, seq, LANES), BF16),
            pltpu.VMEM((2, seq, LANES), BF16),
            pltpu.VMEM((2, seq // FOX_TK, LANES, FOX_TK), BF16),
            pltpu.VMEM((2, FOX_TQ, LANES), BF16),
            pltpu.VMEM((2, LANES, FOX_TQ), BF16),
            pltpu.VMEM((2, LANES, FOX_TQ), F32),
            pltpu.VMEM((2, FOX_TQ, 1), F32),
        ],
        compiler_params=_params("parallel", "parallel", "arbitrary"),
        name="fox_attention",
    )(bound, proj, proj, proj, cum, qg, kg)


MEMKV_TM = 512
MEMATT_TQ = 1024


def _mem_kv_kernel(mem_ref, g_ref, w_ref, kg_ref, k_out, v_out):
    hb = _rms_rows(mem_ref[...], g_ref[...]).astype(BF16)
    kv = jnp.dot(hb, w_ref[...], preferred_element_type=F32)
    width = MEM_HEADS * MEM_HEAD_DIM
    for h in range(MEM_HEADS):
        cols = slice(h * MEM_HEAD_DIM, (h + 1) * MEM_HEAD_DIM)
        k_out[:, cols] = _rms_rows(kv[:, cols], kg_ref[...]).astype(BF16)
    v_out[...] = kv[:, width:].astype(BF16)


def _mem_kv(mem2, gain, w_kv, k_gain):
    rows = mem2.shape[0]
    width = MEM_HEADS * MEM_HEAD_DIM
    return pl.pallas_call(
        _mem_kv_kernel,
        grid=(rows // MEMKV_TM,),
        in_specs=[
            pl.BlockSpec((MEMKV_TM, D_MODEL), lambda i: (i, 0)),
            pl.BlockSpec((1, D_MODEL), lambda i: (0, 0)),
            pl.BlockSpec((D_MODEL, 2 * width), lambda i: (0, 0)),
            pl.BlockSpec((1, MEM_HEAD_DIM), lambda i: (0, 0)),
        ],
        out_specs=[pl.BlockSpec((MEMKV_TM, width), lambda i: (i, 0))] * 2,
        out_shape=[jax.ShapeDtypeStruct((rows, width), BF16)] * 2,
        compiler_params=_params("parallel"),
        name="mem_kv",
    )(mem2, gain, w_kv, k_gain)


def _mem_attn_kernel(q_ref, k_ref, v_ref, qg_ref, out_ref):
    for h in range(MEM_HEADS):
        cols = slice(h * MEM_HEAD_DIM, (h + 1) * MEM_HEAD_DIM)
        qn = _rms_rows(q_ref[:, cols].astype(F32), qg_ref[...]) * (MEM_HEAD_DIM ** -0.5)
        s = lax.dot_general(qn.astype(BF16), k_ref[:, cols], NT_DIMS, preferred_element_type=F32)
        p = jnp.exp(s - jnp.max(s, axis=-1, keepdims=True))
        denom = jnp.sum(p, axis=-1, keepdims=True)
        o = jnp.dot(p.astype(BF16), v_ref[:, cols], preferred_element_type=F32)
        out_ref[:, cols] = (o / denom).astype(BF16)


def _mem_attention(proj, mk, mv, q_gain, batch, seq, mem_len):
    t = batch * seq
    nq = seq // MEMATT_TQ
    width = MEM_HEADS * MEM_HEAD_DIM
    return pl.pallas_call(
        _mem_attn_kernel,
        grid=(batch, nq),
        in_specs=[
            pl.BlockSpec((MEMATT_TQ, width), lambda b, i: (b * nq + i, 7)),
            pl.BlockSpec((mem_len, width), lambda b, i: (b, 0)),
            pl.BlockSpec((mem_len, width), lambda b, i: (b, 0)),
            pl.BlockSpec((1, MEM_HEAD_DIM), lambda b, i: (0, 0)),
        ],
        out_specs=pl.BlockSpec((MEMATT_TQ, width), lambda b, i: (b * nq + i, 0)),
        out_shape=jax.ShapeDtypeStruct((t, width), BF16),
        compiler_params=_params("parallel", "parallel"),
        name="mem_attention",
    )(proj, mk, mv, q_gain)


MIX_TM = 512
ROUTER_ROWS = 32
PAIRS = ((0, 1), (0, 2), (0, 3), (1, 2), (1, 3), (2, 3))
PAIRS_PER_GROUP = len(PAIRS)
N_CLASSES = N_GROUPS * PAIRS_PER_GROUP
ROW_WIDTH = D_MODEL + LANES


def _top2_sum(b0, b1, b2, b3):
    p, q = jnp.maximum(b0, b1), jnp.minimum(b0, b1)
    r, s = jnp.maximum(b2, b3), jnp.minimum(b2, b3)
    return jnp.maximum(p, r) + jnp.maximum(jnp.minimum(p, r), jnp.maximum(q, s))


def _mix_kernel(ro_ref, fo_ref, mo_ref, gr_ref, gf_ref, gm_ref, x_ref,
                wr_ref, wf_ref, wm_ref, wo_ref, nf_ref, rw_ref, rb_ref,
                xa_ref, cls_ref):
    def branch(a_ref, w_ref, g_ref):
        y = jnp.dot(a_ref[...], w_ref[...], preferred_element_type=F32)
        return jax.nn.sigmoid(g_ref[...].astype(F32)) * y

    merged = (branch(ro_ref, wr_ref, gr_ref) + branch(fo_ref, wf_ref, gf_ref)
              + branch(mo_ref, wm_ref, gm_ref))
    xn = x_ref[...] + jnp.dot(merged.astype(BF16), wo_ref[...], preferred_element_type=F32)
    xa_ref[:, :D_MODEL] = xn
    h = _rms_rows(xn, nf_ref[...])
    h_hi = h.astype(BF16)

    h_lo = (h - h_hi.astype(F32)).astype(BF16)
    rw = rw_ref[...]
    rw_hi = rw.astype(BF16)
    rw_lo = (rw - rw_hi.astype(F32)).astype(BF16)
    dot_nt = lambda a, b: lax.dot_general(a, b, NT_DIMS, preferred_element_type=F32)
    logits = dot_nt(rw_hi, h_hi) + dot_nt(rw_hi, h_lo) + dot_nt(rw_lo, h_hi)
    scores = jax.nn.sigmoid(logits)
    biased = scores + rb_ref[...]
    n = EXPERTS_PER_GROUP
    sc = [scores[SUBLANES * j:SUBLANES * (j + 1)] for j in range(n)]
    bi = [biased[SUBLANES * j:SUBLANES * (j + 1)] for j in range(n)]
    group_score = _top2_sum(*bi)
    rows = lax.broadcasted_iota(jnp.int32, group_score.shape, 0)
    best = jnp.max(group_score, axis=0, keepdims=True)
    top_group = jnp.min(jnp.where(group_score == best, rows, SUBLANES), axis=0, keepdims=True)
    in_group = rows == top_group
    picked, chosen = [], []
    for j in range(n):
        rank = jnp.zeros(group_score.shape, jnp.int32)
        for i in range(n):
            if i == j:
                continue
            ahead = (bi[i] > bi[j]) | ((bi[i] == bi[j]) & (i < j))
            rank = rank + ahead.astype(jnp.int32)
        sel = in_group & (rank < 2)
        picked.append(jnp.where(sel, sc[j], 0.0))
        chosen.append(jnp.max(jnp.where(sel, 1.0, 0.0), axis=0, keepdims=True) > 0.5)
    denom = jnp.sum(picked[0] + picked[1] + picked[2] + picked[3], axis=0, keepdims=True)
    gate = [jnp.sum(picked[j], axis=0, keepdims=True) / denom for j in range(n)]

    first = jnp.full(top_group.shape, n, jnp.int32)
    second = jnp.full(top_group.shape, -1, jnp.int32)
    for j in range(n):
        first = jnp.minimum(first, jnp.where(chosen[j], j, n))
        second = jnp.maximum(second, jnp.where(chosen[j], j, -1))
    pair_base = jnp.where(first == 0, 0, jnp.where(first == 1, 3, 5))
    cls = top_group * PAIRS_PER_GROUP + pair_base + second - first - 1
    cls_ref[...] = jnp.clip(cls, 0, N_CLASSES - 1)
    g_first = sum(jnp.where(first == j, gate[j], 0.0) for j in range(n))
    g_second = sum(jnp.where(second == j, gate[j], 0.0) for j in range(n))
    srow = lax.broadcasted_iota(jnp.int32, (SUBLANES, 1), 0)
    gates8 = jnp.where(srow == 0, g_first, jnp.where(srow == 1, g_second, 0.0))
    pad = jnp.zeros((LANES - SUBLANES, gates8.shape[1]), F32)
    xa_ref[:, D_MODEL:] = jnp.concatenate([gates8, pad], axis=0).T


def _mix_and_route(ro, fo, mo, proj, x2, w_r, w_f, w_m, w_o, norm_ffn, rw_pad, rb_pad):
    t = x2.shape[0]
    tok = lambda i: (i, 0)
    const = lambda i: (0, 0)
    act = pl.BlockSpec((MIX_TM, D_MODEL), tok)
    wspec = pl.BlockSpec((D_MODEL, D_MODEL), const)
    return pl.pallas_call(
        _mix_kernel,
        grid=(t // MIX_TM,),
        in_specs=[
            act, act, act,
            pl.BlockSpec((MIX_TM, D_MODEL), lambda i: (i, 8)),
            pl.BlockSpec((MIX_TM, D_MODEL), lambda i: (i, 9)),
            pl.BlockSpec((MIX_TM, D_MODEL), lambda i: (i, 10)),
            act,
            wspec, wspec, wspec, wspec,
            pl.BlockSpec((1, D_MODEL), const),
            pl.BlockSpec((ROUTER_ROWS, D_MODEL), const),
            pl.BlockSpec((ROUTER_ROWS, 1), const),
        ],
        out_specs=[
            pl.BlockSpec((MIX_TM, ROW_WIDTH), tok),
            pl.BlockSpec((1, MIX_TM), lambda i: (0, i)),
        ],
        out_shape=[
            jax.ShapeDtypeStruct((t, ROW_WIDTH), F32),
            jax.ShapeDtypeStruct((1, t), jnp.int32),
        ],
        compiler_params=_params("parallel"),
        name="mix_and_route",
    )(ro, fo, mo, proj, proj, proj, x2, w_r, w_f, w_m, w_o, norm_ffn, rw_pad, rb_pad)


EXP_TR = 256
POS_TM = 512
MOVE_TM = 1024
CLASS_ROWS = 32


def _position_kernel(cls_ref, pos_ref, offs_ref, size_ref, counts_ref, running_ref):
    phase = pl.program_id(0)
    i = pl.program_id(1)
    rows = lax.broadcasted_iota(jnp.int32, (CLASS_ROWS, POS_TM), 0)
    onehot = jnp.where(rows == cls_ref[...], 1.0, 0.0)
    per_class = jnp.sum(onehot, axis=1, keepdims=True)

    @pl.when(phase == 0)
    def _count():
        @pl.when(i == 0)
        def _():
            counts_ref[...] = jnp.zeros_like(counts_ref)
        counts_ref[...] += per_class
        pos_ref[...] = jnp.zeros_like(pos_ref)

    @pl.when(phase == 1)
    def _place():
        @pl.when(i == 0)
        def _():
            padded = jnp.floor((counts_ref[...] + (EXP_TR - 1)) * (1.0 / EXP_TR)) * EXP_TR
            r = lax.broadcasted_iota(jnp.int32, (CLASS_ROWS, CLASS_ROWS), 0)
            c = lax.broadcasted_iota(jnp.int32, (CLASS_ROWS, CLASS_ROWS), 1)
            below = jnp.where(c < r, 1.0, 0.0)
            offs = jnp.dot(below, padded, preferred_element_type=F32,
                           precision=lax.Precision.HIGHEST)
            offs_ref[...] = offs
            size_ref[...] = padded
            running_ref[...] = offs

        r = lax.broadcasted_iota(jnp.int32, (POS_TM, POS_TM), 0)
        c = lax.broadcasted_iota(jnp.int32, (POS_TM, POS_TM), 1)
        earlier = jnp.where(r < c, 1.0, 0.0).astype(BF16)
        prefix = jnp.dot(onehot.astype(BF16), earlier, preferred_element_type=F32)
        place = prefix + running_ref[:, 0:1]
        pos_ref[...] = jnp.sum(onehot * place, axis=0, keepdims=True).astype(jnp.int32)
        running_ref[...] += per_class


def _positions(cls):
    t = cls.shape[1]
    meta = jax.ShapeDtypeStruct((CLASS_ROWS, LANES), F32)
    return pl.pallas_call(
        _position_kernel,
        grid=(2, t // POS_TM),
        in_specs=[pl.BlockSpec((1, POS_TM), lambda p, i: (0, i))],
        out_specs=[
            pl.BlockSpec((1, POS_TM), lambda p, i: (0, i * p)),
            pl.BlockSpec((CLASS_ROWS, LANES), lambda p, i: (0, 0)),
            pl.BlockSpec((CLASS_ROWS, LANES), lambda p, i: (0, 0)),
        ],
        out_shape=[jax.ShapeDtypeStruct((1, t), jnp.int32), meta, meta],
        scratch_shapes=[pltpu.VMEM((CLASS_ROWS, LANES), F32), pltpu.VMEM((CLASS_ROWS, LANES), F32)],
        compiler_params=_params("arbitrary", "arbitrary"),
        name="moe_positions",
    )(cls)


def _move_rows(copy_for_row):
    def start(k, carry):
        copy_for_row(2 * k).start(priority=0)
        copy_for_row(2 * k + 1).start(priority=1)
        return carry

    def wait(r, carry):
        copy_for_row(r).wait()
        return carry

    lax.fori_loop(0, MOVE_TM // 2, start, 0, unroll=4)
    lax.fori_loop(0, MOVE_TM, wait, 0, unroll=8)


def _scatter_rows_kernel(pos_ref, src_ref, init_ref, dst_ref, sem):
    del init_ref
    _move_rows(lambda r: pltpu.make_async_copy(
        src_ref.at[pl.ds(r, 1)], dst_ref.at[pl.ds(pos_ref[r], 1)], sem))


def _scatter_rows(pos, src, init):
    t = src.shape[0]
    return pl.pallas_call(
        _scatter_rows_kernel,
        grid=(t // MOVE_TM,),
        in_specs=[
            pl.BlockSpec((MOVE_TM,), lambda i: (i,), memory_space=pltpu.SMEM),
            pl.BlockSpec((MOVE_TM, src.shape[1]), lambda i: (i, 0)),
            pl.BlockSpec(memory_space=pl.ANY),
        ],
        out_specs=pl.BlockSpec(memory_space=pl.ANY),
        out_shape=jax.ShapeDtypeStruct(init.shape, init.dtype),
        scratch_shapes=[pltpu.SemaphoreType.DMA(())],
        input_output_aliases={2: 0},
        compiler_params=_params("arbitrary"),
        name="moe_scatter_rows",
    )(pos, src, init)


def _gather_rows_kernel(pos_ref, src_ref, dst_ref, sem):
    _move_rows(lambda r: pltpu.make_async_copy(
        src_ref.at[pl.ds(pos_ref[r], 1)], dst_ref.at[pl.ds(r, 1)], sem))


def _gather_rows(pos, src):
    t = pos.shape[0]
    return pl.pallas_call(
        _gather_rows_kernel,
        grid=(t // MOVE_TM,),
        in_specs=[
            pl.BlockSpec((MOVE_TM,), lambda i: (i,), memory_space=pltpu.SMEM),
            pl.BlockSpec(memory_space=pl.ANY),
        ],
        out_specs=pl.BlockSpec((MOVE_TM, src.shape[1]), lambda i: (i, 0)),
        out_shape=jax.ShapeDtypeStruct((t, src.shape[1]), src.dtype),
        scratch_shapes=[pltpu.SemaphoreType.DMA(())],
        compiler_params=_params("arbitrary"),
        name="moe_gather_rows",
    )(pos, src)


def _experts_kernel(ea_ref, eb_ref, used_ref, xs_ref, nf_ref,
                    wga_ref, wua_ref, wda_ref, wgb_ref, wub_ref, wdb_ref, ys_ref):
    del ea_ref, eb_ref
    k = pl.program_id(0)

    @pl.when(k < used_ref[0])
    def _():
        x = xs_ref[:, :D_MODEL]
        gates = xs_ref[:, D_MODEL:]
        h = _rms_rows(x, nf_ref[...]).astype(BF16)

        def ffn(wg_ref, wu_ref, wd_ref):
            a = jnp.dot(h, wg_ref[...], preferred_element_type=F32)
            u = jnp.dot(h, wu_ref[...], preferred_element_type=F32)
            return jnp.dot((a * jax.nn.sigmoid(a) * u).astype(BF16), wd_ref[...],
                           preferred_element_type=F32)

        moe = (gates[:, 0:1] * ffn(wga_ref, wua_ref, wda_ref)
               + gates[:, 1:2] * ffn(wgb_ref, wub_ref, wdb_ref))
        ys_ref[...] = x + moe

    @pl.when(k >= used_ref[0])
    def _():
        ys_ref[...] = jnp.zeros_like(ys_ref)


def _experts(xs, norm_ffn, w_gate, w_up, w_down, expert_a, expert_b, n_used):
    n_tiles = xs.shape[0] // EXP_TR
    up = lambda sel: pl.BlockSpec((None, D_MODEL, D_FF), lambda k, ea, eb, nu: (sel(ea, eb)[k], 0, 0))
    down = lambda sel: pl.BlockSpec((None, D_FF, D_MODEL), lambda k, ea, eb, nu: (sel(ea, eb)[k], 0, 0))
    first = lambda ea, eb: ea
    second = lambda ea, eb: eb
    grid_spec = pltpu.PrefetchScalarGridSpec(
        num_scalar_prefetch=3,
        grid=(n_tiles,),
        in_specs=[
            pl.BlockSpec((EXP_TR, ROW_WIDTH), lambda k, ea, eb, nu: (k, 0)),
            pl.BlockSpec((1, D_MODEL), lambda k, ea, eb, nu: (0, 0)),
            up(first), up(first), down(first), up(second), up(second), down(second),
        ],
        out_specs=pl.BlockSpec((EXP_TR, D_MODEL), lambda k, ea, eb, nu: (k, 0)),
    )
    return pl.pallas_call(
        _experts_kernel,
        grid_spec=grid_spec,
        out_shape=jax.ShapeDtypeStruct((xs.shape[0], D_MODEL), F32),
        compiler_params=_params("arbitrary"),
        name="experts",
    )(expert_a, expert_b, n_used, xs, norm_ffn, w_gate, w_up, w_down, w_gate, w_up, w_down)


def _tile_experts(offs, sizes, n_tiles):
    ends = (offs[:N_CLASSES, 0] + sizes[:N_CLASSES, 0]).astype(jnp.int32)
    starts = jnp.arange(n_tiles, dtype=jnp.int32) * EXP_TR
    tile_cls = jnp.minimum(jnp.sum(ends[None, :] <= starts[:, None], axis=1), N_CLASSES - 1)
    group = tile_cls // PAIRS_PER_GROUP
    pair = tile_cls % PAIRS_PER_GROUP
    slot_a = jnp.array([a for a, _ in PAIRS], jnp.int32)[pair]
    slot_b = jnp.array([b for _, b in PAIRS], jnp.int32)[pair]
    expert_a = (group * EXPERTS_PER_GROUP + slot_a).astype(jnp.int32)
    expert_b = (group * EXPERTS_PER_GROUP + slot_b).astype(jnp.int32)
    return expert_a, expert_b, (ends[-1] // EXP_TR).reshape(1)


def _router_layout(router_w, router_bias):
    w = router_w.T.reshape(N_GROUPS, EXPERTS_PER_GROUP, D_MODEL)
    w = jnp.transpose(w, (1, 0, 2))
    w = jnp.pad(w, ((0, 0), (0, SUBLANES - N_GROUPS), (0, 0))).reshape(ROUTER_ROWS, D_MODEL)
    b = jnp.transpose(router_bias.reshape(N_GROUPS, EXPERTS_PER_GROUP), (1, 0))
    b = jnp.pad(b, ((0, 0), (0, SUBLANES - N_GROUPS)), constant_values=NEG_BIG)
    return w.astype(F32), b.reshape(ROUTER_ROWS, 1).astype(F32)


def kernel(x, mem, norm_mix, norm_mem, w_in, b_forget, fox_q_norm, fox_k_norm, mem_q_norm,
           mem_k_norm, w_mem_kv, w_o_ret, w_o_fox, w_o_mem, w_out, norm_ffn, router_w,
           router_bias, w_gate, w_up, w_down):
    batch, seq, d = x.shape
    mem_len = mem.shape[1]
    depth = w_in.shape[0]
    t = batch * seq
    ff_lo = 7 * D_MODEL
    ff_hi = ff_lo + FOX_HEADS

    cos, sin, intra, qd, kd, cd = _retention_tables(seq)
    rw_pad, rb_pad = _router_layout(router_w, router_bias)
    row = lambda v: v.reshape(1, -1).astype(F32)

    x2 = x.reshape(t, d)
    mem2 = mem.reshape(batch * mem_len, d)
    n_tiles = t // EXP_TR + N_CLASSES
    sorted_rows = jnp.zeros((n_tiles * EXP_TR, ROW_WIDTH), F32)
    for l in range(depth):
        w_main = jnp.concatenate([w_in[l, :, :ff_lo], w_in[l, :, ff_hi:]], axis=1).astype(BF16)
        w_ff = jnp.pad(w_in[l, :, ff_lo:ff_hi], ((0, 0), (0, LANES - FOX_HEADS))).astype(BF16)
        b_pad = jnp.pad(b_forget[l], (0, LANES - FOX_HEADS)).reshape(1, LANES)

        proj, ff = _in_projection(x2, row(norm_mix[l]), w_main, w_ff)
        cum = _forget_cumsum(ff, b_pad, batch, seq)
        ro = _retention(proj, cos, sin, intra, qd, kd, cd, batch, seq)
        bound = (8.16 * jnp.max(jnp.abs(fox_q_norm[l])) * jnp.max(jnp.abs(fox_k_norm[l])))
        fo = _fox_attention(proj, cum, bound.reshape(1).astype(F32),
                            row(jnp.tile(fox_q_norm[l], 2)), row(jnp.tile(fox_k_norm[l], 2)),
                            batch, seq)
        mk, mv = _mem_kv(mem2, row(norm_mem[l]), w_mem_kv[l].astype(BF16), row(mem_k_norm[l]))
        mo = _mem_attention(proj, mk, mv, row(mem_q_norm[l]), batch, seq, mem_len)
        xa, cls = _mix_and_route(
            ro, fo, mo, proj, x2, w_o_ret[l].astype(BF16), w_o_fox[l].astype(BF16),
            w_o_mem[l].astype(BF16), w_out[l].astype(BF16), row(norm_ffn[l]), rw_pad, rb_pad)
        pos, offs, sizes = _positions(cls)
        pos = pos.reshape(t)
        expert_a, expert_b, n_used = _tile_experts(offs, sizes, n_tiles)
        sorted_rows = _scatter_rows(pos, xa, sorted_rows)
        ys = _experts(sorted_rows, row(norm_ffn[l]), w_gate[l].astype(BF16), w_up[l].astype(BF16),
                      w_down[l].astype(BF16), expert_a, expert_b, n_used)
        x2 = _gather_rows(pos, ys)
    return x2.reshape(batch, seq, d)
```

```python
import functools

import jax
import jax.numpy as jnp
from jax import lax
from jax.experimental import pallas as pl
from jax.experimental.pallas import tpu as pltpu

F32 = jnp.float32
BF16 = jnp.bfloat16

D_MODEL = 1024
EPS = 1e-6
RET_HEADS = 4
RET_HEAD_DIM = 256
RET_CHUNK = 128
ROPE_BASE = 10000.0
FOX_HEADS = 16
FOX_HEAD_DIM = 64
MEM_HEADS = 4
MEM_HEAD_DIM = 256
N_EXPERTS = 16
N_GROUPS = 4
EXPERTS_PER_GROUP = 4
D_FF = 512

LANES = 128
SUBLANES = 8
VMEM_LIMIT = 56 * 1024 * 1024
NEG_BIG = -1e30

NT_DIMS = (((1,), (1,)), ((), ()))


def _params(*sem):
    return pltpu.CompilerParams(dimension_semantics=sem, vmem_limit_bytes=VMEM_LIMIT)


def _rms_rows(x, gain_row):
    ms = jnp.mean(x * x, axis=-1, keepdims=True)
    return x * lax.rsqrt(ms + EPS) * gain_row


IN_TM = 2048
IN_BLOCKS = 11


def _inproj_kernel(x_ref, g_ref, w_ref, wff_ref, out_ref, ff_ref, hn_ref):
    @pl.when(pl.program_id(1) == 0)
    def _():
        hb = _rms_rows(x_ref[...], g_ref[...]).astype(BF16)
        hn_ref[...] = hb
        ff_ref[...] = jnp.dot(hb, wff_ref[...], preferred_element_type=F32)

    out_ref[...] = jnp.dot(hn_ref[...], w_ref[...], preferred_element_type=F32).astype(BF16)


def _in_projection(x2, gain, w_main, w_ff):
    t = x2.shape[0]
    return pl.pallas_call(
        _inproj_kernel,
        grid=(t // IN_TM, IN_BLOCKS),
        in_specs=[
            pl.BlockSpec((IN_TM, D_MODEL), lambda i, j: (i, 0)),
            pl.BlockSpec((1, D_MODEL), lambda i, j: (0, 0)),
            pl.BlockSpec((D_MODEL, D_MODEL), lambda i, j: (0, j)),
            pl.BlockSpec((D_MODEL, LANES), lambda i, j: (0, 0)),
        ],
        out_specs=[
            pl.BlockSpec((IN_TM, D_MODEL), lambda i, j: (i, j)),
            pl.BlockSpec((IN_TM, LANES), lambda i, j: (i, 0)),
        ],
        out_shape=[
            jax.ShapeDtypeStruct((t, IN_BLOCKS * D_MODEL), BF16),
            jax.ShapeDtypeStruct((t, LANES), F32),
        ],
        scratch_shapes=[pltpu.VMEM((IN_TM, D_MODEL), BF16)],
        compiler_params=_params("parallel", "arbitrary"),
        name="in_projection",
    )(x2, gain, w_main, w_ff)


CUM_BLOCK = 512


def _cumsum_kernel(ff_ref, b_ref, out_ref):
    s = ff_ref.shape[0]
    r = lax.broadcasted_iota(jnp.int32, (CUM_BLOCK, CUM_BLOCK), 0)
    c = lax.broadcasted_iota(jnp.int32, (CUM_BLOCK, CUM_BLOCK), 1)
    lower = (c <= r).astype(F32)
    carry = jnp.zeros((1, LANES), F32)
    for blk in range(s // CUM_BLOCK):
        rows = slice(blk * CUM_BLOCK, (blk + 1) * CUM_BLOCK)
        z = ff_ref[rows, :] + b_ref[...]
        log_f = jnp.minimum(z, 0.0) - jnp.log1p(jnp.exp(-jnp.abs(z)))
        cs = jnp.dot(lower, log_f, preferred_element_type=F32,
                     precision=lax.Precision.HIGHEST) + carry
        out_ref[rows, :] = cs
        carry = cs[CUM_BLOCK - 1:CUM_BLOCK, :]


def _forget_cumsum(ff, b_pad, batch, seq):
    return pl.pallas_call(
        _cumsum_kernel,
        grid=(batch,),
        in_specs=[
            pl.BlockSpec((seq, LANES), lambda b: (b, 0)),
            pl.BlockSpec((1, LANES), lambda b: (0, 0)),
        ],
        out_specs=pl.BlockSpec((seq, LANES), lambda b: (b, 0)),
        out_shape=jax.ShapeDtypeStruct((batch * seq, LANES), F32),
        compiler_params=_params("parallel"),
        name="forget_cumsum",
    )(ff, b_pad)


RET_ROWS = 2048
RET_GROUP = 2


def _retention_kernel(q_ref, k_ref, v_ref, g_ref, cos_ref, sin_ref, intra_ref, qd_ref, kd_ref,
                      cd_ref, out_ref, state_ref):
    hd = RET_HEAD_DIM
    half = hd // 2

    @pl.when(pl.program_id(2) == 0)
    def _():
        state_ref[...] = jnp.zeros_like(state_ref)

    def chunk(c, carry):
        r0 = pl.multiple_of(c * RET_CHUNK, RET_CHUNK)
        rows = pl.ds(r0, RET_CHUNK)
        cs = cos_ref[rows, :]
        sn = sin_ref[rows, :]

        def rot(x):
            x1 = x[:, :half]
            x2 = x[:, half:]
            return jnp.concatenate([x1 * cs - x2 * sn, x2 * cs + x1 * sn], axis=-1)

        for h in range(RET_GROUP):
            cols = slice(h * hd, (h + 1) * hd)
            qr = rot(q_ref[rows, cols].astype(F32))
            kr = rot(k_ref[rows, cols].astype(F32))
            v = v_ref[rows, cols]
            scores = lax.dot_general(qr.astype(BF16), kr.astype(BF16), NT_DIMS,
                                     preferred_element_type=F32) * intra_ref[h]
            st = state_ref[h]
            o = (jnp.dot(scores.astype(BF16), v, preferred_element_type=F32)
                 + jnp.dot((qr * qd_ref[h]).astype(BF16), st.astype(BF16),
                           preferred_element_type=F32))
            kd_t = (kr * kd_ref[h]).T.astype(BF16)
            state_ref[h] = st * cd_ref[h] + jnp.dot(kd_t, v, preferred_element_type=F32)
            on = o * lax.rsqrt(jnp.mean(o * o, axis=-1, keepdims=True) + EPS)
            g = g_ref[rows, cols].astype(F32)
            out_ref[rows, cols] = (on * (g * jax.nn.sigmoid(g))).astype(BF16)
        return carry

    lax.fori_loop(0, RET_ROWS // RET_CHUNK, chunk, 0)


def _retention(proj, cos, sin, intra, qd, kd, cd, batch, seq):
    t = batch * seq
    hd = RET_HEAD_DIM
    width = RET_GROUP * hd
    groups = RET_HEADS // RET_GROUP
    steps = seq // RET_ROWS
    col = lambda off: (lambda b, p, s: (b * steps + s, off + p))
    tab = lambda b, p, s: (p, 0, 0)
    angle = pl.BlockSpec((RET_ROWS, hd // 2), lambda b, p, s: (s, 0))
    return pl.pallas_call(
        _retention_kernel,
        grid=(batch, groups, steps),
        in_specs=[
            pl.BlockSpec((RET_ROWS, width), col(0)),
            pl.BlockSpec((RET_ROWS, width), col(groups)),
            pl.BlockSpec((RET_ROWS, width), col(2 * groups)),
            pl.BlockSpec((RET_ROWS, width), col(3 * groups)),
            angle, angle,
            pl.BlockSpec((RET_GROUP, RET_CHUNK, RET_CHUNK), tab),
            pl.BlockSpec((RET_GROUP, RET_CHUNK, hd), tab),
            pl.BlockSpec((RET_GROUP, RET_CHUNK, hd), tab),
            pl.BlockSpec((RET_GROUP, 1, hd), tab),
        ],
        out_specs=pl.BlockSpec((RET_ROWS, width), lambda b, p, s: (b * steps + s, p)),
        out_shape=jax.ShapeDtypeStruct((t, RET_HEADS * hd), BF16),
        scratch_shapes=[pltpu.VMEM((RET_GROUP, hd, hd), F32)],
        compiler_params=_params("parallel", "parallel", "arbitrary"),
        name="retention",
    )(proj, proj, proj, proj, cos, sin, intra, qd, kd, cd)


def _retention_tables(seq):
    h = jnp.arange(RET_HEADS, dtype=F32)
    log_gamma = jnp.log1p(-(2.0 ** (-5.0 - h)))
    idx = jnp.arange(RET_CHUNK, dtype=F32)
    diff = idx[:, None] - idx[None, :]
    scale = RET_HEAD_DIM ** -0.5
    intra = jnp.where(diff >= 0, jnp.exp(log_gamma[:, None, None] * jnp.maximum(diff, 0.0)), 0.0) * scale
    q_decay = jnp.exp(log_gamma[:, None] * (idx + 1.0))
    k_decay = jnp.exp(log_gamma[:, None] * (RET_CHUNK - 1.0 - idx)) * scale
    chunk_decay = jnp.exp(log_gamma * RET_CHUNK)
    qd = jnp.broadcast_to(q_decay[:, :, None], (RET_HEADS, RET_CHUNK, RET_HEAD_DIM))
    kd = jnp.broadcast_to(k_decay[:, :, None], (RET_HEADS, RET_CHUNK, RET_HEAD_DIM))
    cd = jnp.broadcast_to(chunk_decay[:, None, None], (RET_HEADS, 1, RET_HEAD_DIM))
    inv_freq = ROPE_BASE ** (-jnp.arange(0, RET_HEAD_DIM, 2, dtype=F32) / RET_HEAD_DIM)
    ang = jnp.arange(seq, dtype=F32)[:, None] * inv_freq[None, :]
    return jnp.cos(ang), jnp.sin(ang), intra, qd, kd, cd


FOX_TQ = 512
FOX_TK = 512
assert FOX_TQ == FOX_TK


def _pair_rms(x, lo_mask, gain_row):
    x2 = x * x
    s_all = jnp.sum(x2, axis=-1, keepdims=True)
    s_lo = jnp.sum(jnp.where(lo_mask, x2, 0.0), axis=-1, keepdims=True)
    ms = jnp.where(lo_mask, s_lo, s_all - s_lo) * (1.0 / FOX_HEAD_DIM)
    return x * lax.rsqrt(ms + EPS) * gain_row


AUG_CQ = 0
AUG_CK = 3
AUG_M = 6
AUG_SPLIT = 3
FOX_EXACT_ABOVE = 30.0
FOX_UNDERFLOW_LOG = -88.0


def _split3(x):
    hi = x.astype(BF16)
    r1 = x - hi.astype(F32)
    mid = r1.astype(BF16)
    lo = (r1 - mid.astype(F32)).astype(BF16)
    return hi, mid, lo


def _lanes_in(lane, start, count=AUG_SPLIT):
    return (lane >= start) & (lane < start + count)


def _fox_kernel(bound_ref, first_ref, q_ref, k_ref, v_ref, c_ref, qg_ref, kg_ref, out_ref,
                kaug_ref, caq_ref, vat_ref, qb_ref, qat_ref, acc_ref, m_ref):
    seq = k_ref.shape[0]
    hp = pl.program_id(1)
    qi = pl.program_id(2)
    lane = lax.broadcasted_iota(jnp.int32, (1, LANES), 1)
    lo = lane < FOX_HEAD_DIM
    head_lanes = (lo, jnp.logical_not(lo))
    spare = (FOX_HEAD_DIM, 0)
    bound = bound_ref[0]
    first = first_ref[(pl.program_id(0) * pl.num_programs(1) + hp) * pl.num_programs(2) + qi]

    @pl.when(qi == 0)
    def _prepare_keys_values():
        r = lax.broadcasted_iota(jnp.int32, (LANES, LANES), 0)
        c = lax.broadcasted_iota(jnp.int32, (LANES, LANES), 1)
        place = []
        for k in range(AUG_SPLIT):
            pk = jnp.zeros((LANES, LANES), F32)
            for h in range(2):
                src = r == 2 * hp + h
                pk = pk + jnp.where(src & (c == spare[h] + AUG_CQ + k), 1.0, 0.0)
                pk = pk - jnp.where(src & (c == spare[h] + AUG_CK + k), 1.0, 0.0)
            place.append(pk.astype(BF16))
        bound_parts = [p.astype(F32) for p in _split3(jnp.full((1, LANES), -bound, F32))]

        def prep(i, carry):
            rows = pl.ds(pl.multiple_of(i * FOX_TK, FOX_TK), FOX_TK)
            parts = _split3(c_ref[rows, :])
            e = sum(jnp.dot(parts[k], place[k], preferred_element_type=F32)
                    for k in range(AUG_SPLIT))
            kn = _pair_rms(k_ref[rows, :].astype(F32), lo, kg_ref[...])
            v = v_ref[rows, :]
            one = jnp.ones_like(v)
            for h in range(2):
                ones_k = jnp.where(_lanes_in(lane, spare[h] + AUG_CQ)
                                   | _lanes_in(lane, spare[h] + AUG_M), 1.0, 0.0)
                kaug_ref[h, rows, :] = jnp.where(
                    head_lanes[h], kn,
                    jnp.where(_lanes_in(lane, spare[h] + AUG_CK), e, ones_k)).astype(BF16)
                const_q = jnp.where(_lanes_in(lane, spare[h] + AUG_CK), 1.0, 0.0)
                for k in range(AUG_SPLIT):
                    const_q = jnp.where(lane == spare[h] + AUG_M + k, bound_parts[k], const_q)
                caq_ref[h, rows, :] = jnp.where(
                    _lanes_in(lane, spare[h] + AUG_CQ), e, const_q).astype(BF16)
                va = jnp.where(head_lanes[h], v, one)
                vat_ref[h, i] = va.astype(F32).T.astype(BF16)
            return carry
        lax.fori_loop(0, seq // FOX_TK, prep, 0)

    t0 = pl.multiple_of(qi * FOX_TQ, FOX_TQ)
    q_rows = pl.ds(t0, FOX_TQ)
    qn = _pair_rms(q_ref[...].astype(F32), lo, qg_ref[...]) * (FOX_HEAD_DIM ** -0.5)
    kq = lax.broadcasted_iota(jnp.int32, (FOX_TK, FOX_TQ), 0)
    qq = lax.broadcasted_iota(jnp.int32, (FOX_TK, FOX_TQ), 1)

    q_aug = [jnp.where(head_lanes[h], qn, caq_ref[h, q_rows, :].astype(F32)) for h in range(2)]
    for h in range(2):
        qat_ref[h] = q_aug[h].T.astype(BF16)

    @pl.when(bound > FOX_EXACT_ABOVE)
    def _exact_row_max():
        for h in range(2):
            qb_ref[h] = jnp.where(_lanes_in(lane, spare[h] + AUG_M), 0.0, q_aug[h]).astype(BF16)
            m_ref[h] = jnp.full((FOX_TQ, 1), NEG_BIG, F32)

        def scan(j, masked):
            keys = pl.ds(pl.multiple_of(j * FOX_TK, FOX_TK), FOX_TK)
            for h in range(2):
                s = lax.dot_general(qb_ref[h], kaug_ref[h, keys, :], NT_DIMS,
                                    preferred_element_type=F32)
                if masked:
                    s = jnp.where(qq <= kq, s, NEG_BIG)
                m_ref[h] = jnp.maximum(m_ref[h], jnp.max(s, axis=-1, keepdims=True))

        def scan_body(j, carry):
            scan(j, False)
            return carry
        lax.fori_loop(first, qi, scan_body, 0)
        scan(qi, True)
        for h in range(2):
            m_parts = _split3(-m_ref[h])
            qa = q_aug[h]
            for k in range(AUG_SPLIT):
                qa = jnp.where(lane == spare[h] + AUG_M + k, m_parts[k].astype(F32), qa)
            qat_ref[h] = qa.T.astype(BF16)

    acc_ref[...] = jnp.zeros_like(acc_ref)

    def probs(h, keys, q_cols=slice(None)):
        st = jnp.dot(kaug_ref[h, keys, :], qat_ref[h, :, q_cols], preferred_element_type=F32)
        return jnp.exp(st)

    def kv_blocks(blocks):
        for h in range(2):
            total = None
            for j in blocks:
                keys = pl.ds(pl.multiple_of(j * FOX_TK, FOX_TK), FOX_TK)
                o = jnp.dot(vat_ref[h, j], probs(h, keys).astype(BF16), preferred_element_type=F32)
                total = o if total is None else total + o
            acc_ref[h] += total

    def pair(i, carry):
        kv_blocks((first + 2 * i, first + 2 * i + 1))
        return carry

    full_blocks = qi - first
    lax.fori_loop(0, lax.shift_right_logical(full_blocks, 1), pair, 0)

    @pl.when((full_blocks & 1) == 1)
    def _odd_block():
        kv_blocks((qi - 1,))

    half = FOX_TK // 2
    first, second = slice(0, half), slice(half, FOX_TK)
    causal_a = (lax.broadcasted_iota(jnp.int32, (half, FOX_TQ), 0)
                <= lax.broadcasted_iota(jnp.int32, (half, FOX_TQ), 1))
    causal_b = (lax.broadcasted_iota(jnp.int32, (half, half), 0)
                <= lax.broadcasted_iota(jnp.int32, (half, half), 1))
    for h in range(2):
        p_a = jnp.where(causal_a, probs(h, pl.ds(t0, half)), 0.0)
        p_b = jnp.where(causal_b, probs(h, pl.ds(t0 + half, half), second), 0.0)
        acc_ref[h] += jnp.dot(vat_ref[h, qi, :, first], p_a.astype(BF16),
                              preferred_element_type=F32)
        acc_ref[h, :, second] += jnp.dot(vat_ref[h, qi, :, second], p_b.astype(BF16),
                                         preferred_element_type=F32)

    o0 = acc_ref[0]
    o1 = acc_ref[1]
    l0 = o0[FOX_HEAD_DIM:FOX_HEAD_DIM + 1, :]
    l1 = o1[0:1, :]
    row = lax.broadcasted_iota(jnp.int32, (LANES, 1), 0)
    out_ref[...] = jnp.where(row < FOX_HEAD_DIM, o0 / l0, o1 / l1).T.astype(BF16)


def _fox_first_blocks(cum, bound, batch, seq):
    blocks = seq // FOX_TK
    c = cum.reshape(batch, seq, LANES)[:, :, :FOX_HEADS]
    c_start = c[:, 0::FOX_TQ, :]
    c_end = c[:, FOX_TK - 1::FOX_TK, :]
    negligible = (c_start[:, :, None, :] - c_end[:, None, :, :] + 2.0 * bound) < FOX_UNDERFLOW_LOG
    idx = jnp.arange(blocks)
    earlier = (idx[None, :] < idx[:, None])[None, :, :, None]
    count = jnp.sum(negligible & earlier, axis=2)
    first = jnp.min(count.reshape(batch, blocks, FOX_HEADS // 2, 2), axis=-1)
    return jnp.transpose(first, (0, 2, 1)).reshape(-1).astype(jnp.int32)


def _fox_attention(proj, cum, bound, first, qg, kg, batch, seq):
    t = batch * seq
    nq = seq // FOX_TQ
    pairs = FOX_HEADS // 2
    base = 4 * (D_MODEL // LANES)
    return pl.pallas_call(
        _fox_kernel,
        grid=(batch, pairs, nq),
        in_specs=[
            pl.BlockSpec(memory_space=pltpu.SMEM),
            pl.BlockSpec(memory_space=pltpu.SMEM),
            pl.BlockSpec((FOX_TQ, LANES), lambda b, p, i: (b * nq + i, base + p)),
            pl.BlockSpec((seq, LANES), lambda b, p, i: (b, base + pairs + p)),
            pl.BlockSpec((seq, LANES), lambda b, p, i: (b, base + 2 * pairs + p)),
            pl.BlockSpec((seq, LANES), lambda b, p, i: (b, 0)),
            pl.BlockSpec((1, LANES), lambda b, p, i: (0, 0)),
            pl.BlockSpec((1, LANES), lambda b, p, i: (0, 0)),
        ],
        out_specs=pl.BlockSpec((FOX_TQ, LANES), lambda b, p, i: (b * nq + i, p)),
        out_shape=jax.ShapeDtypeStruct((t, FOX_HEADS * FOX_HEAD_DIM), BF16),
        scratch_shapes=[
            pltpu.VMEM((2, seq, LANES), BF16),
            pltpu.VMEM((2, seq, LANES), BF16),
            pltpu.VMEM((2, seq // FOX_TK, LANES, FOX_TK), BF16),
            pltpu.VMEM((2, FOX_TQ, LANES), BF16),
            pltpu.VMEM((2, LANES, FOX_TQ), BF16),
            pltpu.VMEM((2, LANES, FOX_TQ), F32),
            pltpu.VMEM((2, FOX_TQ, 1), F32),
        ],
        compiler_params=_params("parallel", "parallel", "arbitrary"),
        name="fox_attention",
    )(bound, first, proj, proj, proj, cum, qg, kg)


MEMKV_TM = 512
MEMATT_TQ = 1024


def _mem_kv_kernel(mem_ref, g_ref, w_ref, kg_ref, k_out, v_out):
    hb = _rms_rows(mem_ref[...], g_ref[...]).astype(BF16)
    kv = jnp.dot(hb, w_ref[...], preferred_element_type=F32)
    width = MEM_HEADS * MEM_HEAD_DIM
    for h in range(MEM_HEADS):
        cols = slice(h * MEM_HEAD_DIM, (h + 1) * MEM_HEAD_DIM)
        k_out[:, cols] = _rms_rows(kv[:, cols], kg_ref[...]).astype(BF16)
    v_out[...] = kv[:, width:].astype(BF16)


def _mem_kv(mem2, gain, w_kv, k_gain):
    rows = mem2.shape[0]
    width = MEM_HEADS * MEM_HEAD_DIM
    return pl.pallas_call(
        _mem_kv_kernel,
        grid=(rows // MEMKV_TM,),
        in_specs=[
            pl.BlockSpec((MEMKV_TM, D_MODEL), lambda i: (i, 0)),
            pl.BlockSpec((1, D_MODEL), lambda i: (0, 0)),
            pl.BlockSpec((D_MODEL, 2 * width), lambda i: (0, 0)),
            pl.BlockSpec((1, MEM_HEAD_DIM), lambda i: (0, 0)),
        ],
        out_specs=[pl.BlockSpec((MEMKV_TM, width), lambda i: (i, 0))] * 2,
        out_shape=[jax.ShapeDtypeStruct((rows, width), BF16)] * 2,
        compiler_params=_params("parallel"),
        name="mem_kv",
    )(mem2, gain, w_kv, k_gain)


def _mem_attn_kernel(q_ref, k_ref, v_ref, qg_ref, out_ref):
    for h in range(MEM_HEADS):
        cols = slice(h * MEM_HEAD_DIM, (h + 1) * MEM_HEAD_DIM)
        qn = _rms_rows(q_ref[:, cols].astype(F32), qg_ref[...]) * (MEM_HEAD_DIM ** -0.5)
        s = lax.dot_general(qn.astype(BF16), k_ref[:, cols], NT_DIMS, preferred_element_type=F32)
        p = jnp.exp(s - jnp.max(s, axis=-1, keepdims=True))
        denom = jnp.sum(p, axis=-1, keepdims=True)
        o = jnp.dot(p.astype(BF16), v_ref[:, cols], preferred_element_type=F32)
        out_ref[:, cols] = (o / denom).astype(BF16)


def _mem_attention(proj, mk, mv, q_gain, batch, seq, mem_len):
    t = batch * seq
    nq = seq // MEMATT_TQ
    width = MEM_HEADS * MEM_HEAD_DIM
    return pl.pallas_call(
        _mem_attn_kernel,
        grid=(batch, nq),
        in_specs=[
            pl.BlockSpec((MEMATT_TQ, width), lambda b, i: (b * nq + i, 7)),
            pl.BlockSpec((mem_len, width), lambda b, i: (b, 0)),
            pl.BlockSpec((mem_len, width), lambda b, i: (b, 0)),
            pl.BlockSpec((1, MEM_HEAD_DIM), lambda b, i: (0, 0)),
        ],
        out_specs=pl.BlockSpec((MEMATT_TQ, width), lambda b, i: (b * nq + i, 0)),
        out_shape=jax.ShapeDtypeStruct((t, width), BF16),
        compiler_params=_params("parallel", "parallel"),
        name="mem_attention",
    )(proj, mk, mv, q_gain)


MIX_TM = 512
ROUTER_ROWS = 32
PAIRS = ((0, 1), (0, 2), (0, 3), (1, 2), (1, 3), (2, 3))
PAIRS_PER_GROUP = len(PAIRS)
N_CLASSES = N_GROUPS * PAIRS_PER_GROUP
ROW_WIDTH = D_MODEL + LANES


def _top2_sum(b0, b1, b2, b3):
    p, q = jnp.maximum(b0, b1), jnp.minimum(b0, b1)
    r, s = jnp.maximum(b2, b3), jnp.minimum(b2, b3)
    return jnp.maximum(p, r) + jnp.maximum(jnp.minimum(p, r), jnp.maximum(q, s))


def _mix_kernel(ro_ref, fo_ref, mo_ref, gr_ref, gf_ref, gm_ref, x_ref,
                wr_ref, wf_ref, wm_ref, wo_ref, nf_ref, rw_ref, rb_ref,
                xa_ref, cls_ref):
    def branch(a_ref, w_ref, g_ref):
        y = jnp.dot(a_ref[...], w_ref[...], preferred_element_type=F32)
        return jax.nn.sigmoid(g_ref[...].astype(F32)) * y

    merged = (branch(ro_ref, wr_ref, gr_ref) + branch(fo_ref, wf_ref, gf_ref)
              + branch(mo_ref, wm_ref, gm_ref))
    xn = x_ref[...] + jnp.dot(merged.astype(BF16), wo_ref[...], preferred_element_type=F32)
    xa_ref[:, :D_MODEL] = xn
    h = _rms_rows(xn, nf_ref[...])
    h_hi = h.astype(BF16)

    h_lo = (h - h_hi.astype(F32)).astype(BF16)
    rw = rw_ref[...]
    rw_hi = rw.astype(BF16)
    rw_lo = (rw - rw_hi.astype(F32)).astype(BF16)
    dot_nt = lambda a, b: lax.dot_general(a, b, NT_DIMS, preferred_element_type=F32)
    logits = dot_nt(rw_hi, h_hi) + dot_nt(rw_hi, h_lo) + dot_nt(rw_lo, h_hi)
    scores = jax.nn.sigmoid(logits)
    biased = scores + rb_ref[...]
    n = EXPERTS_PER_GROUP
    sc = [scores[SUBLANES * j:SUBLANES * (j + 1)] for j in range(n)]
    bi = [biased[SUBLANES * j:SUBLANES * (j + 1)] for j in range(n)]
    group_score = _top2_sum(*bi)
    rows = lax.broadcasted_iota(jnp.int32, group_score.shape, 0)
    best = jnp.max(group_score, axis=0, keepdims=True)
    top_group = jnp.min(jnp.where(group_score == best, rows, SUBLANES), axis=0, keepdims=True)
    in_group = rows == top_group
    picked, chosen = [], []
    for j in range(n):
        rank = jnp.zeros(group_score.shape, jnp.int32)
        for i in range(n):
            if i == j:
                continue
            ahead = (bi[i] > bi[j]) | ((bi[i] == bi[j]) & (i < j))
            rank = rank + ahead.astype(jnp.int32)
        sel = in_group & (rank < 2)
        picked.append(jnp.where(sel, sc[j], 0.0))
        chosen.append(jnp.max(jnp.where(sel, 1.0, 0.0), axis=0, keepdims=True) > 0.5)
    denom = jnp.sum(picked[0] + picked[1] + picked[2] + picked[3], axis=0, keepdims=True)
    gate = [jnp.sum(picked[j], axis=0, keepdims=True) / denom for j in range(n)]

    first = jnp.full(top_group.shape, n, jnp.int32)
    second = jnp.full(top_group.shape, -1, jnp.int32)
    for j in range(n):
        first = jnp.minimum(first, jnp.where(chosen[j], j, n))
        second = jnp.maximum(second, jnp.where(chosen[j], j, -1))
    pair_base = jnp.where(first == 0, 0, jnp.where(first == 1, 3, 5))
    cls = top_group * PAIRS_PER_GROUP + pair_base + second - first - 1
    cls_ref[...] = jnp.clip(cls, 0, N_CLASSES - 1)
    g_first = sum(jnp.where(first == j, gate[j], 0.0) for j in range(n))
    g_second = sum(jnp.where(second == j, gate[j], 0.0) for j in range(n))
    srow = lax.broadcasted_iota(jnp.int32, (SUBLANES, 1), 0)
    gates8 = jnp.where(srow == 0, g_first, jnp.where(srow == 1, g_second, 0.0))
    pad = jnp.zeros((LANES - SUBLANES, gates8.shape[1]), F32)
    xa_ref[:, D_MODEL:] = jnp.concatenate([gates8, pad], axis=0).T


def _mix_and_route(ro, fo, mo, proj, x2, w_r, w_f, w_m, w_o, norm_ffn, rw_pad, rb_pad):
    t = x2.shape[0]
    tok = lambda i: (i, 0)
    const = lambda i: (0, 0)
    act = pl.BlockSpec((MIX_TM, D_MODEL), tok)
    wspec = pl.BlockSpec((D_MODEL, D_MODEL), const)
    return pl.pallas_call(
        _mix_kernel,
        grid=(t // MIX_TM,),
        in_specs=[
            act, act, act,
            pl.BlockSpec((MIX_TM, D_MODEL), lambda i: (i, 8)),
            pl.BlockSpec((MIX_TM, D_MODEL), lambda i: (i, 9)),
            pl.BlockSpec((MIX_TM, D_MODEL), lambda i: (i, 10)),
            act,
            wspec, wspec, wspec, wspec,
            pl.BlockSpec((1, D_MODEL), const),
            pl.BlockSpec((ROUTER_ROWS, D_MODEL), const),
            pl.BlockSpec((ROUTER_ROWS, 1), const),
        ],
        out_specs=[
            pl.BlockSpec((MIX_TM, ROW_WIDTH), tok),
            pl.BlockSpec((1, MIX_TM), lambda i: (0, i)),
        ],
        out_shape=[
            jax.ShapeDtypeStruct((t, ROW_WIDTH), F32),
            jax.ShapeDtypeStruct((1, t), jnp.int32),
        ],
        compiler_params=_params("parallel"),
        name="mix_and_route",
    )(ro, fo, mo, proj, proj, proj, x2, w_r, w_f, w_m, w_o, norm_ffn, rw_pad, rb_pad)


EXP_TR = 256
POS_TM = 512
MOVE_TM = 1024
CLASS_ROWS = 32


def _position_kernel(cls_ref, pos_ref, offs_ref, size_ref, counts_ref, running_ref):
    phase = pl.program_id(0)
    i = pl.program_id(1)
    rows = lax.broadcasted_iota(jnp.int32, (CLASS_ROWS, POS_TM), 0)
    onehot = jnp.where(rows == cls_ref[...], 1.0, 0.0)
    per_class = jnp.sum(onehot, axis=1, keepdims=True)

    @pl.when(phase == 0)
    def _count():
        @pl.when(i == 0)
        def _():
            counts_ref[...] = jnp.zeros_like(counts_ref)
        counts_ref[...] += per_class
        pos_ref[...] = jnp.zeros_like(pos_ref)

    @pl.when(phase == 1)
    def _place():
        @pl.when(i == 0)
        def _():
            padded = jnp.floor((counts_ref[...] + (EXP_TR - 1)) * (1.0 / EXP_TR)) * EXP_TR
            r = lax.broadcasted_iota(jnp.int32, (CLASS_ROWS, CLASS_ROWS), 0)
            c = lax.broadcasted_iota(jnp.int32, (CLASS_ROWS, CLASS_ROWS), 1)
            below = jnp.where(c < r, 1.0, 0.0)
            offs = jnp.dot(below, padded, preferred_element_type=F32,
                           precision=lax.Precision.HIGHEST)
            offs_ref[...] = offs
            size_ref[...] = padded
            running_ref[...] = offs

        r = lax.broadcasted_iota(jnp.int32, (POS_TM, POS_TM), 0)
        c = lax.broadcasted_iota(jnp.int32, (POS_TM, POS_TM), 1)
        earlier = jnp.where(r < c, 1.0, 0.0).astype(BF16)
        prefix = jnp.dot(onehot.astype(BF16), earlier, preferred_element_type=F32)
        place = prefix + running_ref[:, 0:1]
        pos_ref[...] = jnp.sum(onehot * place, axis=0, keepdims=True).astype(jnp.int32)
        running_ref[...] += per_class


def _positions(cls):
    t = cls.shape[1]
    meta = jax.ShapeDtypeStruct((CLASS_ROWS, LANES), F32)
    return pl.pallas_call(
        _position_kernel,
        grid=(2, t // POS_TM),
        in_specs=[pl.BlockSpec((1, POS_TM), lambda p, i: (0, i))],
        out_specs=[
            pl.BlockSpec((1, POS_TM), lambda p, i: (0, i * p)),
            pl.BlockSpec((CLASS_ROWS, LANES), lambda p, i: (0, 0)),
            pl.BlockSpec((CLASS_ROWS, LANES), lambda p, i: (0, 0)),
        ],
        out_shape=[jax.ShapeDtypeStruct((1, t), jnp.int32), meta, meta],
        scratch_shapes=[pltpu.VMEM((CLASS_ROWS, LANES), F32), pltpu.VMEM((CLASS_ROWS, LANES), F32)],
        compiler_params=_params("arbitrary", "arbitrary"),
        name="moe_positions",
    )(cls)


def _move_rows(copy_for_row):
    def start(k, carry):
        copy_for_row(2 * k).start(priority=0)
        copy_for_row(2 * k + 1).start(priority=1)
        return carry

    def wait(r, carry):
        copy_for_row(r).wait()
        return carry

    lax.fori_loop(0, MOVE_TM // 2, start, 0, unroll=4)
    lax.fori_loop(0, MOVE_TM, wait, 0, unroll=8)


def _scatter_rows_kernel(pos_ref, src_ref, init_ref, dst_ref, sem):
    del init_ref
    _move_rows(lambda r: pltpu.make_async_copy(
        src_ref.at[pl.ds(r, 1)], dst_ref.at[pl.ds(pos_ref[r], 1)], sem))


def _scatter_rows(pos, src, init):
    t = src.shape[0]
    return pl.pallas_call(
        _scatter_rows_kernel,
        grid=(t // MOVE_TM,),
        in_specs=[
            pl.BlockSpec((MOVE_TM,), lambda i: (i,), memory_space=pltpu.SMEM),
            pl.BlockSpec((MOVE_TM, src.shape[1]), lambda i: (i, 0)),
            pl.BlockSpec(memory_space=pl.ANY),
        ],
        out_specs=pl.BlockSpec(memory_space=pl.ANY),
        out_shape=jax.ShapeDtypeStruct(init.shape, init.dtype),
        scratch_shapes=[pltpu.SemaphoreType.DMA(())],
        input_output_aliases={2: 0},
        compiler_params=_params("arbitrary"),
        name="moe_scatter_rows",
    )(pos, src, init)


def _gather_rows_kernel(pos_ref, src_ref, dst_ref, sem):
    _move_rows(lambda r: pltpu.make_async_copy(
        src_ref.at[pl.ds(pos_ref[r], 1)], dst_ref.at[pl.ds(r, 1)], sem))


def _gather_rows(pos, src):
    t = pos.shape[0]
    return pl.pallas_call(
        _gather_rows_kernel,
        grid=(t // MOVE_TM,),
        in_specs=[
            pl.BlockSpec((MOVE_TM,), lambda i: (i,), memory_space=pltpu.SMEM),
            pl.BlockSpec(memory_space=pl.ANY),
        ],
        out_specs=pl.BlockSpec((MOVE_TM, src.shape[1]), lambda i: (i, 0)),
        out_shape=jax.ShapeDtypeStruct((t, src.shape[1]), src.dtype),
        scratch_shapes=[pltpu.SemaphoreType.DMA(())],
        compiler_params=_params("arbitrary"),
        name="moe_gather_rows",
    )(pos, src)


def _experts_kernel(ea_ref, eb_ref, used_ref, xs_ref, nf_ref,
                    wga_ref, wua_ref, wda_ref, wgb_ref, wub_ref, wdb_ref, ys_ref):
    del ea_ref, eb_ref
    k = pl.program_id(0)

    @pl.when(k < used_ref[0])
    def _():
        x = xs_ref[:, :D_MODEL]
        gates = xs_ref[:, D_MODEL:]
        h = _rms_rows(x, nf_ref[...]).astype(BF16)

        def ffn(wg_ref, wu_ref, wd_ref):
            a = jnp.dot(h, wg_ref[...], preferred_element_type=F32)
            u = jnp.dot(h, wu_ref[...], preferred_element_type=F32)
            return jnp.dot((a * jax.nn.sigmoid(a) * u).astype(BF16), wd_ref[...],
                           preferred_element_type=F32)

        moe = (gates[:, 0:1] * ffn(wga_ref, wua_ref, wda_ref)
               + gates[:, 1:2] * ffn(wgb_ref, wub_ref, wdb_ref))
        ys_ref[...] = x + moe

    @pl.when(k >= used_ref[0])
    def _():
        ys_ref[...] = jnp.zeros_like(ys_ref)


def _experts(xs, norm_ffn, w_gate, w_up, w_down, expert_a, expert_b, n_used):
    n_tiles = xs.shape[0] // EXP_TR
    up = lambda sel: pl.BlockSpec((None, D_MODEL, D_FF), lambda k, ea, eb, nu: (sel(ea, eb)[k], 0, 0))
    down = lambda sel: pl.BlockSpec((None, D_FF, D_MODEL), lambda k, ea, eb, nu: (sel(ea, eb)[k], 0, 0))
    first = lambda ea, eb: ea
    second = lambda ea, eb: eb
    grid_spec = pltpu.PrefetchScalarGridSpec(
        num_scalar_prefetch=3,
        grid=(n_tiles,),
        in_specs=[
            pl.BlockSpec((EXP_TR, ROW_WIDTH), lambda k, ea, eb, nu: (k, 0)),
            pl.BlockSpec((1, D_MODEL), lambda k, ea, eb, nu: (0, 0)),
            up(first), up(first), down(first), up(second), up(second), down(second),
        ],
        out_specs=pl.BlockSpec((EXP_TR, D_MODEL), lambda k, ea, eb, nu: (k, 0)),
    )
    return pl.pallas_call(
        _experts_kernel,
        grid_spec=grid_spec,
        out_shape=jax.ShapeDtypeStruct((xs.shape[0], D_MODEL), F32),
        compiler_params=_params("arbitrary"),
        name="experts",
    )(expert_a, expert_b, n_used, xs, norm_ffn, w_gate, w_up, w_down, w_gate, w_up, w_down)


def _tile_experts(offs, sizes, n_tiles):
    ends = (offs[:N_CLASSES, 0] + sizes[:N_CLASSES, 0]).astype(jnp.int32)
    starts = jnp.arange(n_tiles, dtype=jnp.int32) * EXP_TR
    tile_cls = jnp.minimum(jnp.sum(ends[None, :] <= starts[:, None], axis=1), N_CLASSES - 1)
    group = tile_cls // PAIRS_PER_GROUP
    pair = tile_cls % PAIRS_PER_GROUP
    slot_a = jnp.array([a for a, _ in PAIRS], jnp.int32)[pair]
    slot_b = jnp.array([b for _, b in PAIRS], jnp.int32)[pair]
    expert_a = (group * EXPERTS_PER_GROUP + slot_a).astype(jnp.int32)
    expert_b = (group * EXPERTS_PER_GROUP + slot_b).astype(jnp.int32)
    return expert_a, expert_b, (ends[-1] // EXP_TR).reshape(1)


def _router_layout(router_w, router_bias):
    w = router_w.T.reshape(N_GROUPS, EXPERTS_PER_GROUP, D_MODEL)
    w = jnp.transpose(w, (1, 0, 2))
    w = jnp.pad(w, ((0, 0), (0, SUBLANES - N_GROUPS), (0, 0))).reshape(ROUTER_ROWS, D_MODEL)
    b = jnp.transpose(router_bias.reshape(N_GROUPS, EXPERTS_PER_GROUP), (1, 0))
    b = jnp.pad(b, ((0, 0), (0, SUBLANES - N_GROUPS)), constant_values=NEG_BIG)
    return w.astype(F32), b.reshape(ROUTER_ROWS, 1).astype(F32)


def kernel(x, mem, norm_mix, norm_mem, w_in, b_forget, fox_q_norm, fox_k_norm, mem_q_norm,
           mem_k_norm, w_mem_kv, w_o_ret, w_o_fox, w_o_mem, w_out, norm_ffn, router_w,
           router_bias, w_gate, w_up, w_down):
    batch, seq, d = x.shape
    mem_len = mem.shape[1]
    depth = w_in.shape[0]
    t = batch * seq
    ff_lo = 7 * D_MODEL
    ff_hi = ff_lo + FOX_HEADS

    cos, sin, intra, qd, kd, cd = _retention_tables(seq)
    rw_pad, rb_pad = _router_layout(router_w, router_bias)
    row = lambda v: v.reshape(1, -1).astype(F32)

    x2 = x.reshape(t, d)
    mem2 = mem.reshape(batch * mem_len, d)
    n_tiles = t // EXP_TR + N_CLASSES
    sorted_rows = jnp.zeros((n_tiles * EXP_TR, ROW_WIDTH), F32)
    for l in range(depth):
        w_main = jnp.concatenate([w_in[l, :, :ff_lo], w_in[l, :, ff_hi:]], axis=1).astype(BF16)
        w_ff = jnp.pad(w_in[l, :, ff_lo:ff_hi], ((0, 0), (0, LANES - FOX_HEADS))).astype(BF16)
        b_pad = jnp.pad(b_forget[l], (0, LANES - FOX_HEADS)).reshape(1, LANES)

        proj, ff = _in_projection(x2, row(norm_mix[l]), w_main, w_ff)
        cum = _forget_cumsum(ff, b_pad, batch, seq)
        ro = _retention(proj, cos, sin, intra, qd, kd, cd, batch, seq)
        bound = (8.16 * jnp.max(jnp.abs(fox_q_norm[l])) * jnp.max(jnp.abs(fox_k_norm[l])))
        first = _fox_first_blocks(cum, bound, batch, seq)
        fo = _fox_attention(proj, cum, bound.reshape(1).astype(F32), first,
                            row(jnp.tile(fox_q_norm[l], 2)), row(jnp.tile(fox_k_norm[l], 2)),
                            batch, seq)
        mk, mv = _mem_kv(mem2, row(norm_mem[l]), w_mem_kv[l].astype(BF16), row(mem_k_norm[l]))
        mo = _mem_attention(proj, mk, mv, row(mem_q_norm[l]), batch, seq, mem_len)
        xa, cls = _mix_and_route(
            ro, fo, mo, proj, x2, w_o_ret[l].astype(BF16), w_o_fox[l].astype(BF16),
            w_o_mem[l].astype(BF16), w_out[l].astype(BF16), row(norm_ffn[l]), rw_pad, rb_pad)
        pos, offs, sizes = _positions(cls)
        pos = pos.reshape(t)
        expert_a, expert_b, n_used = _tile_experts(offs, sizes, n_tiles)
        sorted_rows = _scatter_rows(pos, xa, sorted_rows)
        ys = _experts(sorted_rows, row(norm_ffn[l]), w_gate[l].astype(BF16), w_up[l].astype(BF16),
                      w_down[l].astype(BF16), expert_a, expert_b, n_used)
        x2 = _gather_rows(pos, ys)
    return x2.reshape(batch, seq, d)
```

```python
import functools

import jax
import jax.numpy as jnp
from jax import lax
from jax.experimental import pallas as pl
from jax.experimental.pallas import tpu as pltpu

F32 = jnp.float32
BF16 = jnp.bfloat16

D_MODEL = 1024
EPS = 1e-6
RET_HEADS = 4
RET_HEAD_DIM = 256
RET_CHUNK = 128
ROPE_BASE = 10000.0
FOX_HEADS = 16
FOX_HEAD_DIM = 64
MEM_HEADS = 4
MEM_HEAD_DIM = 256
N_EXPERTS = 16
N_GROUPS = 4
EXPERTS_PER_GROUP = 4
D_FF = 512

LANES = 128
SUBLANES = 8
VMEM_LIMIT = 56 * 1024 * 1024
NEG_BIG = -1e30

NT_DIMS = (((1,), (1,)), ((), ()))


def _params(*sem):
    return pltpu.CompilerParams(dimension_semantics=sem, vmem_limit_bytes=VMEM_LIMIT)


def _rms_rows(x, gain_row):
    ms = jnp.mean(x * x, axis=-1, keepdims=True)
    return x * lax.rsqrt(ms + EPS) * gain_row


IN_TM = 2048
IN_BLOCKS = 11


def _inproj_kernel(x_ref, g_ref, w_ref, wff_ref, out_ref, ff_ref, hn_ref):
    @pl.when(pl.program_id(1) == 0)
    def _():
        hb = _rms_rows(x_ref[...], g_ref[...]).astype(BF16)
        hn_ref[...] = hb
        ff_ref[...] = jnp.dot(hb, wff_ref[...], preferred_element_type=F32)

    out_ref[...] = jnp.dot(hn_ref[...], w_ref[...], preferred_element_type=F32).astype(BF16)


def _in_projection(x2, gain, w_main, w_ff):
    t = x2.shape[0]
    return pl.pallas_call(
        _inproj_kernel,
        grid=(t // IN_TM, IN_BLOCKS),
        in_specs=[
            pl.BlockSpec((IN_TM, D_MODEL), lambda i, j: (i, 0)),
            pl.BlockSpec((1, D_MODEL), lambda i, j: (0, 0)),
            pl.BlockSpec((D_MODEL, D_MODEL), lambda i, j: (0, j)),
            pl.BlockSpec((D_MODEL, LANES), lambda i, j: (0, 0)),
        ],
        out_specs=[
            pl.BlockSpec((IN_TM, D_MODEL), lambda i, j: (i, j)),
            pl.BlockSpec((IN_TM, LANES), lambda i, j: (i, 0)),
        ],
        out_shape=[
            jax.ShapeDtypeStruct((t, IN_BLOCKS * D_MODEL), BF16),
            jax.ShapeDtypeStruct((t, LANES), F32),
        ],
        scratch_shapes=[pltpu.VMEM((IN_TM, D_MODEL), BF16)],
        compiler_params=_params("parallel", "arbitrary"),
        name="in_projection",
    )(x2, gain, w_main, w_ff)


CUM_BLOCK = 512


def _cumsum_kernel(ff_ref, b_ref, out_ref):
    s = ff_ref.shape[0]
    r = lax.broadcasted_iota(jnp.int32, (CUM_BLOCK, CUM_BLOCK), 0)
    c = lax.broadcasted_iota(jnp.int32, (CUM_BLOCK, CUM_BLOCK), 1)
    lower = (c <= r).astype(F32)
    carry = jnp.zeros((1, LANES), F32)
    for blk in range(s // CUM_BLOCK):
        rows = slice(blk * CUM_BLOCK, (blk + 1) * CUM_BLOCK)
        z = ff_ref[rows, :] + b_ref[...]
        log_f = jnp.minimum(z, 0.0) - jnp.log1p(jnp.exp(-jnp.abs(z)))
        cs = jnp.dot(lower, log_f, preferred_element_type=F32,
                     precision=lax.Precision.HIGHEST) + carry
        out_ref[rows, :] = cs
        carry = cs[CUM_BLOCK - 1:CUM_BLOCK, :]


def _forget_cumsum(ff, b_pad, batch, seq):
    return pl.pallas_call(
        _cumsum_kernel,
        grid=(batch,),
        in_specs=[
            pl.BlockSpec((seq, LANES), lambda b: (b, 0)),
            pl.BlockSpec((1, LANES), lambda b: (0, 0)),
        ],
        out_specs=pl.BlockSpec((seq, LANES), lambda b: (b, 0)),
        out_shape=jax.ShapeDtypeStruct((batch * seq, LANES), F32),
        compiler_params=_params("parallel"),
        name="forget_cumsum",
    )(ff, b_pad)


RET_ROWS = 1024
RET_GROUP = 4


def _retention_kernel(q_ref, k_ref, v_ref, g_ref, cos_ref, sin_ref, intra_ref, qd_ref, kd_ref,
                      cd_ref, out_ref, state_ref):
    hd = RET_HEAD_DIM
    half = hd // 2

    @pl.when(pl.program_id(2) == 0)
    def _():
        state_ref[...] = jnp.zeros_like(state_ref)

    def chunk(c, carry):
        r0 = pl.multiple_of(c * RET_CHUNK, RET_CHUNK)
        rows = pl.ds(r0, RET_CHUNK)
        cs = cos_ref[rows, :]
        sn = sin_ref[rows, :]

        def rot(x):
            x1 = x[:, :half]
            x2 = x[:, half:]
            return jnp.concatenate([x1 * cs - x2 * sn, x2 * cs + x1 * sn], axis=-1)

        for h in range(RET_GROUP):
            cols = slice(h * hd, (h + 1) * hd)
            qr = rot(q_ref[rows, cols].astype(F32))
            kr = rot(k_ref[rows, cols].astype(F32))
            v = v_ref[rows, cols]
            scores = lax.dot_general(qr.astype(BF16), kr.astype(BF16), NT_DIMS,
                                     preferred_element_type=F32) * intra_ref[h]
            st = state_ref[h]
            o = (jnp.dot(scores.astype(BF16), v, preferred_element_type=F32)
                 + jnp.dot((qr * qd_ref[h]).astype(BF16), st.astype(BF16),
                           preferred_element_type=F32))
            kd_t = (kr * kd_ref[h]).T.astype(BF16)
            state_ref[h] = st * cd_ref[h] + jnp.dot(kd_t, v, preferred_element_type=F32)
            on = o * lax.rsqrt(jnp.mean(o * o, axis=-1, keepdims=True) + EPS)
            g = g_ref[rows, cols].astype(F32)
            out_ref[rows, cols] = (on * (g * jax.nn.sigmoid(g))).astype(BF16)
        return carry

    lax.fori_loop(0, RET_ROWS // RET_CHUNK, chunk, 0)


def _retention(proj, cos, sin, intra, qd, kd, cd, batch, seq):
    t = batch * seq
    hd = RET_HEAD_DIM
    width = RET_GROUP * hd
    groups = RET_HEADS // RET_GROUP
    steps = seq // RET_ROWS
    col = lambda off: (lambda b, p, s: (b * steps + s, off + p))
    tab = lambda b, p, s: (p, 0, 0)
    angle = pl.BlockSpec((RET_ROWS, hd // 2), lambda b, p, s: (s, 0))
    return pl.pallas_call(
        _retention_kernel,
        grid=(batch, groups, steps),
        in_specs=[
            pl.BlockSpec((RET_ROWS, width), col(0)),
            pl.BlockSpec((RET_ROWS, width), col(groups)),
            pl.BlockSpec((RET_ROWS, width), col(2 * groups)),
            pl.BlockSpec((RET_ROWS, width), col(3 * groups)),
            angle, angle,
            pl.BlockSpec((RET_GROUP, RET_CHUNK, RET_CHUNK), tab),
            pl.BlockSpec((RET_GROUP, RET_CHUNK, hd), tab),
            pl.BlockSpec((RET_GROUP, RET_CHUNK, hd), tab),
            pl.BlockSpec((RET_GROUP, 1, hd), tab),
        ],
        out_specs=pl.BlockSpec((RET_ROWS, width), lambda b, p, s: (b * steps + s, p)),
        out_shape=jax.ShapeDtypeStruct((t, RET_HEADS * hd), BF16),
        scratch_shapes=[pltpu.VMEM((RET_GROUP, hd, hd), F32)],
        compiler_params=_params("parallel", "parallel", "arbitrary"),
        name="retention",
    )(proj, proj, proj, proj, cos, sin, intra, qd, kd, cd)


def _retention_tables(seq):
    h = jnp.arange(RET_HEADS, dtype=F32)
    log_gamma = jnp.log1p(-(2.0 ** (-5.0 - h)))
    idx = jnp.arange(RET_CHUNK, dtype=F32)
    diff = idx[:, None] - idx[None, :]
    scale = RET_HEAD_DIM ** -0.5
    intra = jnp.where(diff >= 0, jnp.exp(log_gamma[:, None, None] * jnp.maximum(diff, 0.0)), 0.0) * scale
    q_decay = jnp.exp(log_gamma[:, None] * (idx + 1.0))
    k_decay = jnp.exp(log_gamma[:, None] * (RET_CHUNK - 1.0 - idx)) * scale
    chunk_decay = jnp.exp(log_gamma * RET_CHUNK)
    qd = jnp.broadcast_to(q_decay[:, :, None], (RET_HEADS, RET_CHUNK, RET_HEAD_DIM))
    kd = jnp.broadcast_to(k_decay[:, :, None], (RET_HEADS, RET_CHUNK, RET_HEAD_DIM))
    cd = jnp.broadcast_to(chunk_decay[:, None, None], (RET_HEADS, 1, RET_HEAD_DIM))
    inv_freq = ROPE_BASE ** (-jnp.arange(0, RET_HEAD_DIM, 2, dtype=F32) / RET_HEAD_DIM)
    ang = jnp.arange(seq, dtype=F32)[:, None] * inv_freq[None, :]
    return jnp.cos(ang), jnp.sin(ang), intra, qd, kd, cd


FOX_TQ = 512
FOX_TK = 512
assert FOX_TQ == FOX_TK


def _pair_rms(x, lo_mask, gain_row):
    x2 = x * x
    s_all = jnp.sum(x2, axis=-1, keepdims=True)
    s_lo = jnp.sum(jnp.where(lo_mask, x2, 0.0), axis=-1, keepdims=True)
    ms = jnp.where(lo_mask, s_lo, s_all - s_lo) * (1.0 / FOX_HEAD_DIM)
    return x * lax.rsqrt(ms + EPS) * gain_row


AUG_CQ = 0
AUG_CK = 3
AUG_M = 6
AUG_SPLIT = 3
FOX_EXACT_ABOVE = 30.0
FOX_UNDERFLOW_LOG = -88.0


def _split3(x):
    hi = x.astype(BF16)
    r1 = x - hi.astype(F32)
    mid = r1.astype(BF16)
    lo = (r1 - mid.astype(F32)).astype(BF16)
    return hi, mid, lo


def _lanes_in(lane, start, count=AUG_SPLIT):
    return (lane >= start) & (lane < start + count)


def _fox_kernel(bound_ref, first_ref, q_ref, k_ref, v_ref, c_ref, qg_ref, kg_ref, out_ref,
                kaug_ref, caq_ref, vat_ref, qb_ref, qat_ref, acc_ref, m_ref):
    seq = k_ref.shape[0]
    hp = pl.program_id(1)
    qi = pl.program_id(2)
    lane = lax.broadcasted_iota(jnp.int32, (1, LANES), 1)
    lo = lane < FOX_HEAD_DIM
    head_lanes = (lo, jnp.logical_not(lo))
    spare = (FOX_HEAD_DIM, 0)
    bound = bound_ref[0]
    first = first_ref[(pl.program_id(0) * pl.num_programs(1) + hp) * pl.num_programs(2) + qi]

    @pl.when(qi == 0)
    def _prepare_keys_values():
        r = lax.broadcasted_iota(jnp.int32, (LANES, LANES), 0)
        c = lax.broadcasted_iota(jnp.int32, (LANES, LANES), 1)
        place = []
        for k in range(AUG_SPLIT):
            pk = jnp.zeros((LANES, LANES), F32)
            for h in range(2):
                src = r == 2 * hp + h
                pk = pk + jnp.where(src & (c == spare[h] + AUG_CQ + k), 1.0, 0.0)
                pk = pk - jnp.where(src & (c == spare[h] + AUG_CK + k), 1.0, 0.0)
            place.append(pk.astype(BF16))
        bound_parts = [p.astype(F32) for p in _split3(jnp.full((1, LANES), -bound, F32))]

        def prep(i, carry):
            rows = pl.ds(pl.multiple_of(i * FOX_TK, FOX_TK), FOX_TK)
            parts = _split3(c_ref[rows, :])
            e = sum(jnp.dot(parts[k], place[k], preferred_element_type=F32)
                    for k in range(AUG_SPLIT))
            kn = _pair_rms(k_ref[rows, :].astype(F32), lo, kg_ref[...])
            v = v_ref[rows, :]
            one = jnp.ones_like(v)
            for h in range(2):
                ones_k = jnp.where(_lanes_in(lane, spare[h] + AUG_CQ)
                                   | _lanes_in(lane, spare[h] + AUG_M), 1.0, 0.0)
                kaug_ref[h, rows, :] = jnp.where(
                    head_lanes[h], kn,
                    jnp.where(_lanes_in(lane, spare[h] + AUG_CK), e, ones_k)).astype(BF16)
                const_q = jnp.where(_lanes_in(lane, spare[h] + AUG_CK), 1.0, 0.0)
                for k in range(AUG_SPLIT):
                    const_q = jnp.where(lane == spare[h] + AUG_M + k, bound_parts[k], const_q)
                caq_ref[h, rows, :] = jnp.where(
                    _lanes_in(lane, spare[h] + AUG_CQ), e, const_q).astype(BF16)
                va = jnp.where(head_lanes[h], v, one)
                vat_ref[h, i] = va.astype(F32).T.astype(BF16)
            return carry
        lax.fori_loop(0, seq // FOX_TK, prep, 0)

    t0 = pl.multiple_of(qi * FOX_TQ, FOX_TQ)
    q_rows = pl.ds(t0, FOX_TQ)
    qn = _pair_rms(q_ref[...].astype(F32), lo, qg_ref[...]) * (FOX_HEAD_DIM ** -0.5)
    kq = lax.broadcasted_iota(jnp.int32, (FOX_TK, FOX_TQ), 0)
    qq = lax.broadcasted_iota(jnp.int32, (FOX_TK, FOX_TQ), 1)

    q_aug = [jnp.where(head_lanes[h], qn, caq_ref[h, q_rows, :].astype(F32)) for h in range(2)]
    for h in range(2):
        qat_ref[h] = q_aug[h].T.astype(BF16)

    @pl.when(bound > FOX_EXACT_ABOVE)
    def _exact_row_max():
        for h in range(2):
            qb_ref[h] = jnp.where(_lanes_in(lane, spare[h] + AUG_M), 0.0, q_aug[h]).astype(BF16)
            m_ref[h] = jnp.full((FOX_TQ, 1), NEG_BIG, F32)

        def scan(j, masked):
            keys = pl.ds(pl.multiple_of(j * FOX_TK, FOX_TK), FOX_TK)
            for h in range(2):
                s = lax.dot_general(qb_ref[h], kaug_ref[h, keys, :], NT_DIMS,
                                    preferred_element_type=F32)
                if masked:
                    s = jnp.where(qq <= kq, s, NEG_BIG)
                m_ref[h] = jnp.maximum(m_ref[h], jnp.max(s, axis=-1, keepdims=True))

        def scan_body(j, carry):
            scan(j, False)
            return carry
        lax.fori_loop(first, qi, scan_body, 0)
        scan(qi, True)
        for h in range(2):
            m_parts = _split3(-m_ref[h])
            qa = q_aug[h]
            for k in range(AUG_SPLIT):
                qa = jnp.where(lane == spare[h] + AUG_M + k, m_parts[k].astype(F32), qa)
            qat_ref[h] = qa.T.astype(BF16)

    acc_ref[...] = jnp.zeros_like(acc_ref)

    def probs(h, keys, q_cols=slice(None)):
        st = jnp.dot(kaug_ref[h, keys, :], qat_ref[h, :, q_cols], preferred_element_type=F32)
        return jnp.exp(st)

    def kv_blocks(blocks):
        for h in range(2):
            total = None
            for j in blocks:
                keys = pl.ds(pl.multiple_of(j * FOX_TK, FOX_TK), FOX_TK)
                o = jnp.dot(vat_ref[h, j], probs(h, keys).astype(BF16), preferred_element_type=F32)
                total = o if total is None else total + o
            acc_ref[h] += total

    def pair(i, carry):
        kv_blocks((first + 2 * i, first + 2 * i + 1))
        return carry

    full_blocks = qi - first
    lax.fori_loop(0, lax.shift_right_logical(full_blocks, 1), pair, 0)

    @pl.when((full_blocks & 1) == 1)
    def _odd_block():
        kv_blocks((qi - 1,))

    half = FOX_TK // 2
    lower, upper = slice(0, half), slice(half, FOX_TK)
    causal_a = (lax.broadcasted_iota(jnp.int32, (half, FOX_TQ), 0)
                <= lax.broadcasted_iota(jnp.int32, (half, FOX_TQ), 1))
    causal_b = (lax.broadcasted_iota(jnp.int32, (half, half), 0)
                <= lax.broadcasted_iota(jnp.int32, (half, half), 1))
    for h in range(2):
        p_a = jnp.where(causal_a, probs(h, pl.ds(t0, half)), 0.0)
        p_b = jnp.where(causal_b, probs(h, pl.ds(t0 + half, half), upper), 0.0)
        acc_ref[h] += jnp.dot(vat_ref[h, qi, :, lower], p_a.astype(BF16),
                              preferred_element_type=F32)
        acc_ref[h, :, upper] += jnp.dot(vat_ref[h, qi, :, upper], p_b.astype(BF16),
                                        preferred_element_type=F32)

    o0 = acc_ref[0]
    o1 = acc_ref[1]
    l0 = o0[FOX_HEAD_DIM:FOX_HEAD_DIM + 1, :]
    l1 = o1[0:1, :]
    row = lax.broadcasted_iota(jnp.int32, (LANES, 1), 0)
    out_ref[...] = jnp.where(row < FOX_HEAD_DIM, o0 / l0, o1 / l1).T.astype(BF16)


def _fox_first_blocks(cum, bound, batch, seq):
    blocks = seq // FOX_TK
    c = cum.reshape(batch, seq, LANES)[:, :, :FOX_HEADS]
    c_start = c[:, 0::FOX_TQ, :]
    c_end = c[:, FOX_TK - 1::FOX_TK, :]
    negligible = (c_start[:, :, None, :] - c_end[:, None, :, :] + 2.0 * bound) < FOX_UNDERFLOW_LOG
    idx = jnp.arange(blocks)
    earlier = (idx[None, :] < idx[:, None])[None, :, :, None]
    count = jnp.sum(negligible & earlier, axis=2)
    first = jnp.min(count.reshape(batch, blocks, FOX_HEADS // 2, 2), axis=-1)
    return jnp.transpose(first, (0, 2, 1)).reshape(-1).astype(jnp.int32)


def _fox_attention(proj, cum, bound, first, qg, kg, batch, seq):
    t = batch * seq
    nq = seq // FOX_TQ
    pairs = FOX_HEADS // 2
    base = 4 * (D_MODEL // LANES)
    return pl.pallas_call(
        _fox_kernel,
        grid=(batch, pairs, nq),
        in_specs=[
            pl.BlockSpec(memory_space=pltpu.SMEM),
            pl.BlockSpec(memory_space=pltpu.SMEM),
            pl.BlockSpec((FOX_TQ, LANES), lambda b, p, i: (b * nq + i, base + p)),
            pl.BlockSpec((seq, LANES), lambda b, p, i: (b, base + pairs + p)),
            pl.BlockSpec((seq, LANES), lambda b, p, i: (b, base + 2 * pairs + p)),
            pl.BlockSpec((seq, LANES), lambda b, p, i: (b, 0)),
            pl.BlockSpec((1, LANES), lambda b, p, i: (0, 0)),
            pl.BlockSpec((1, LANES), lambda b, p, i: (0, 0)),
        ],
        out_specs=pl.BlockSpec((FOX_TQ, LANES), lambda b, p, i: (b * nq + i, p)),
        out_shape=jax.ShapeDtypeStruct((t, FOX_HEADS * FOX_HEAD_DIM), BF16),
        scratch_shapes=[
            pltpu.VMEM((2, seq, LANES), BF16),
            pltpu.VMEM((2, seq, LANES), BF16),
            pltpu.VMEM((2, seq // FOX_TK, LANES, FOX_TK), BF16),
            pltpu.VMEM((2, FOX_TQ, LANES), BF16),
            pltpu.VMEM((2, LANES, FOX_TQ), BF16),
            pltpu.VMEM((2, LANES, FOX_TQ), F32),
            pltpu.VMEM((2, FOX_TQ, 1), F32),
        ],
        compiler_params=_params("parallel", "parallel", "arbitrary"),
        name="fox_attention",
    )(bound, first, proj, proj, proj, cum, qg, kg)


MEMKV_TM = 512
MEMATT_TQ = 1024


def _mem_kv_kernel(mem_ref, g_ref, w_ref, kg_ref, k_out, v_out):
    hb = _rms_rows(mem_ref[...], g_ref[...]).astype(BF16)
    kv = jnp.dot(hb, w_ref[...], preferred_element_type=F32)
    width = MEM_HEADS * MEM_HEAD_DIM
    for h in range(MEM_HEADS):
        cols = slice(h * MEM_HEAD_DIM, (h + 1) * MEM_HEAD_DIM)
        k_out[:, cols] = _rms_rows(kv[:, cols], kg_ref[...]).astype(BF16)
    v_out[...] = kv[:, width:].astype(BF16)


def _mem_kv(mem2, gain, w_kv, k_gain):
    rows = mem2.shape[0]
    width = MEM_HEADS * MEM_HEAD_DIM
    return pl.pallas_call(
        _mem_kv_kernel,
        grid=(rows // MEMKV_TM,),
        in_specs=[
            pl.BlockSpec((MEMKV_TM, D_MODEL), lambda i: (i, 0)),
            pl.BlockSpec((1, D_MODEL), lambda i: (0, 0)),
            pl.BlockSpec((D_MODEL, 2 * width), lambda i: (0, 0)),
            pl.BlockSpec((1, MEM_HEAD_DIM), lambda i: (0, 0)),
        ],
        out_specs=[pl.BlockSpec((MEMKV_TM, width), lambda i: (i, 0))] * 2,
        out_shape=[jax.ShapeDtypeStruct((rows, width), BF16)] * 2,
        compiler_params=_params("parallel"),
        name="mem_kv",
    )(mem2, gain, w_kv, k_gain)


def _mem_attn_kernel(q_ref, k_ref, v_ref, qg_ref, out_ref):
    for h in range(MEM_HEADS):
        cols = slice(h * MEM_HEAD_DIM, (h + 1) * MEM_HEAD_DIM)
        qn = _rms_rows(q_ref[:, cols].astype(F32), qg_ref[...]) * (MEM_HEAD_DIM ** -0.5)
        s = lax.dot_general(qn.astype(BF16), k_ref[:, cols], NT_DIMS, preferred_element_type=F32)
        p = jnp.exp(s - jnp.max(s, axis=-1, keepdims=True))
        denom = jnp.sum(p, axis=-1, keepdims=True)
        o = jnp.dot(p.astype(BF16), v_ref[:, cols], preferred_element_type=F32)
        out_ref[:, cols] = (o / denom).astype(BF16)


def _mem_attention(proj, mk, mv, q_gain, batch, seq, mem_len):
    t = batch * seq
    nq = seq // MEMATT_TQ
    width = MEM_HEADS * MEM_HEAD_DIM
    return pl.pallas_call(
        _mem_attn_kernel,
        grid=(batch, nq),
        in_specs=[
            pl.BlockSpec((MEMATT_TQ, width), lambda b, i: (b * nq + i, 7)),
            pl.BlockSpec((mem_len, width), lambda b, i: (b, 0)),
            pl.BlockSpec((mem_len, width), lambda b, i: (b, 0)),
            pl.BlockSpec((1, MEM_HEAD_DIM), lambda b, i: (0, 0)),
        ],
        out_specs=pl.BlockSpec((MEMATT_TQ, width), lambda b, i: (b * nq + i, 0)),
        out_shape=jax.ShapeDtypeStruct((t, width), BF16),
        compiler_params=_params("parallel", "parallel"),
        name="mem_attention",
    )(proj, mk, mv, q_gain)


MIX_TM = 512
ROUTER_ROWS = 32
PAIRS = ((0, 1), (0, 2), (0, 3), (1, 2), (1, 3), (2, 3))
PAIRS_PER_GROUP = len(PAIRS)
N_CLASSES = N_GROUPS * PAIRS_PER_GROUP
ROW_WIDTH = D_MODEL + LANES


def _top2_sum(b0, b1, b2, b3):
    p, q = jnp.maximum(b0, b1), jnp.minimum(b0, b1)
    r, s = jnp.maximum(b2, b3), jnp.minimum(b2, b3)
    return jnp.maximum(p, r) + jnp.maximum(jnp.minimum(p, r), jnp.maximum(q, s))


def _mix_kernel(ro_ref, fo_ref, mo_ref, gr_ref, gf_ref, gm_ref, x_ref,
                wr_ref, wf_ref, wm_ref, wo_ref, nf_ref, rw_ref, rb_ref,
                xa_ref, cls_ref):
    def branch(a_ref, w_ref, g_ref):
        y = jnp.dot(a_ref[...], w_ref[...], preferred_element_type=F32)
        return jax.nn.sigmoid(g_ref[...].astype(F32)) * y

    merged = (branch(ro_ref, wr_ref, gr_ref) + branch(fo_ref, wf_ref, gf_ref)
              + branch(mo_ref, wm_ref, gm_ref))
    xn = x_ref[...] + jnp.dot(merged.astype(BF16), wo_ref[...], preferred_element_type=F32)
    xa_ref[:, :D_MODEL] = xn
    h = _rms_rows(xn, nf_ref[...])
    h_hi = h.astype(BF16)

    h_lo = (h - h_hi.astype(F32)).astype(BF16)
    rw = rw_ref[...]
    rw_hi = rw.astype(BF16)
    rw_lo = (rw - rw_hi.astype(F32)).astype(BF16)
    dot_nt = lambda a, b: lax.dot_general(a, b, NT_DIMS, preferred_element_type=F32)
    logits = dot_nt(rw_hi, h_hi) + dot_nt(rw_hi, h_lo) + dot_nt(rw_lo, h_hi)
    scores = jax.nn.sigmoid(logits)
    biased = scores + rb_ref[...]
    n = EXPERTS_PER_GROUP
    sc = [scores[SUBLANES * j:SUBLANES * (j + 1)] for j in range(n)]
    bi = [biased[SUBLANES * j:SUBLANES * (j + 1)] for j in range(n)]
    group_score = _top2_sum(*bi)
    rows = lax.broadcasted_iota(jnp.int32, group_score.shape, 0)
    best = jnp.max(group_score, axis=0, keepdims=True)
    top_group = jnp.min(jnp.where(group_score == best, rows, SUBLANES), axis=0, keepdims=True)
    in_group = rows == top_group
    picked, chosen = [], []
    for j in range(n):
        rank = jnp.zeros(group_score.shape, jnp.int32)
        for i in range(n):
            if i == j:
                continue
            ahead = (bi[i] > bi[j]) | ((bi[i] == bi[j]) & (i < j))
            rank = rank + ahead.astype(jnp.int32)
        sel = in_group & (rank < 2)
        picked.append(jnp.where(sel, sc[j], 0.0))
        chosen.append(jnp.max(jnp.where(sel, 1.0, 0.0), axis=0, keepdims=True) > 0.5)
    denom = jnp.sum(picked[0] + picked[1] + picked[2] + picked[3], axis=0, keepdims=True)
    gate = [jnp.sum(picked[j], axis=0, keepdims=True) / denom for j in range(n)]

    first = jnp.full(top_group.shape, n, jnp.int32)
    second = jnp.full(top_group.shape, -1, jnp.int32)
    for j in range(n):
        first = jnp.minimum(first, jnp.where(chosen[j], j, n))
        second = jnp.maximum(second, jnp.where(chosen[j], j, -1))
    pair_base = jnp.where(first == 0, 0, jnp.where(first == 1, 3, 5))
    cls = top_group * PAIRS_PER_GROUP + pair_base + second - first - 1
    cls_ref[...] = jnp.clip(cls, 0, N_CLASSES - 1)
    g_first = sum(jnp.where(first == j, gate[j], 0.0) for j in range(n))
    g_second = sum(jnp.where(second == j, gate[j], 0.0) for j in range(n))
    srow = lax.broadcasted_iota(jnp.int32, (SUBLANES, 1), 0)
    gates8 = jnp.where(srow == 0, g_first, jnp.where(srow == 1, g_second, 0.0))
    pad = jnp.zeros((LANES - SUBLANES, gates8.shape[1]), F32)
    xa_ref[:, D_MODEL:] = jnp.concatenate([gates8, pad], axis=0).T


def _mix_and_route(ro, fo, mo, proj, x2, w_r, w_f, w_m, w_o, norm_ffn, rw_pad, rb_pad):
    t = x2.shape[0]
    tok = lambda i: (i, 0)
    const = lambda i: (0, 0)
    act = pl.BlockSpec((MIX_TM, D_MODEL), tok)
    wspec = pl.BlockSpec((D_MODEL, D_MODEL), const)
    return pl.pallas_call(
        _mix_kernel,
        grid=(t // MIX_TM,),
        in_specs=[
            act, act, act,
            pl.BlockSpec((MIX_TM, D_MODEL), lambda i: (i, 8)),
            pl.BlockSpec((MIX_TM, D_MODEL), lambda i: (i, 9)),
            pl.BlockSpec((MIX_TM, D_MODEL), lambda i: (i, 10)),
            act,
            wspec, wspec, wspec, wspec,
            pl.BlockSpec((1, D_MODEL), const),
            pl.BlockSpec((ROUTER_ROWS, D_MODEL), const),
            pl.BlockSpec((ROUTER_ROWS, 1), const),
        ],
        out_specs=[
            pl.BlockSpec((MIX_TM, ROW_WIDTH), tok),
            pl.BlockSpec((1, MIX_TM), lambda i: (0, i)),
        ],
        out_shape=[
            jax.ShapeDtypeStruct((t, ROW_WIDTH), F32),
            jax.ShapeDtypeStruct((1, t), jnp.int32),
        ],
        compiler_params=_params("parallel"),
        name="mix_and_route",
    )(ro, fo, mo, proj, proj, proj, x2, w_r, w_f, w_m, w_o, norm_ffn, rw_pad, rb_pad)


EXP_TR = 256
POS_TM = 2048
POS_SUB = 512
MOVE_TM = 1024
CLASS_ROWS = 32


def _position_kernel(cls_ref, pos_ref, offs_ref, size_ref, counts_ref, running_ref):
    phase = pl.program_id(0)
    i = pl.program_id(1)

    def class_onehot(cls):
        rows = lax.broadcasted_iota(jnp.int32, (CLASS_ROWS, cls.shape[1]), 0)
        return jnp.where(rows == cls, 1.0, 0.0)

    @pl.when(phase == 0)
    def _count():
        @pl.when(i == 0)
        def _():
            counts_ref[...] = jnp.zeros_like(counts_ref)
        counts_ref[...] += jnp.sum(class_onehot(cls_ref[...]), axis=1, keepdims=True)
        pos_ref[...] = jnp.zeros_like(pos_ref)

    @pl.when(phase == 1)
    def _place():
        @pl.when(i == 0)
        def _():
            padded = jnp.floor((counts_ref[...] + (EXP_TR - 1)) * (1.0 / EXP_TR)) * EXP_TR
            r = lax.broadcasted_iota(jnp.int32, (CLASS_ROWS, CLASS_ROWS), 0)
            c = lax.broadcasted_iota(jnp.int32, (CLASS_ROWS, CLASS_ROWS), 1)
            below = jnp.where(c < r, 1.0, 0.0)
            offs = jnp.dot(below, padded, preferred_element_type=F32,
                           precision=lax.Precision.HIGHEST)
            offs_ref[...] = offs
            size_ref[...] = padded
            running_ref[...] = offs

        r = lax.broadcasted_iota(jnp.int32, (POS_SUB, POS_SUB), 0)
        c = lax.broadcasted_iota(jnp.int32, (POS_SUB, POS_SUB), 1)
        earlier = jnp.where(r < c, 1.0, 0.0).astype(BF16)
        running = running_ref[:, 0:1]
        for u in range(POS_TM // POS_SUB):
            cols = slice(u * POS_SUB, (u + 1) * POS_SUB)
            onehot = class_onehot(cls_ref[:, cols])
            prefix = jnp.dot(onehot.astype(BF16), earlier, preferred_element_type=F32)
            pos_ref[:, cols] = jnp.sum(onehot * (prefix + running), axis=0,
                                       keepdims=True).astype(jnp.int32)
            running = running + jnp.sum(onehot, axis=1, keepdims=True)
        running_ref[...] = jnp.broadcast_to(running, running_ref.shape)


def _positions(cls):
    t = cls.shape[1]
    meta = jax.ShapeDtypeStruct((CLASS_ROWS, LANES), F32)
    return pl.pallas_call(
        _position_kernel,
        grid=(2, t // POS_TM),
        in_specs=[pl.BlockSpec((1, POS_TM), lambda p, i: (0, i))],
        out_specs=[
            pl.BlockSpec((1, POS_TM), lambda p, i: (0, i * p)),
            pl.BlockSpec((CLASS_ROWS, LANES), lambda p, i: (0, 0)),
            pl.BlockSpec((CLASS_ROWS, LANES), lambda p, i: (0, 0)),
        ],
        out_shape=[jax.ShapeDtypeStruct((1, t), jnp.int32), meta, meta],
        scratch_shapes=[pltpu.VMEM((CLASS_ROWS, LANES), F32), pltpu.VMEM((CLASS_ROWS, LANES), F32)],
        compiler_params=_params("arbitrary", "arbitrary"),
        name="moe_positions",
    )(cls)


def _move_rows(copy_for_row):
    def start(k, carry):
        copy_for_row(2 * k).start(priority=0)
        copy_for_row(2 * k + 1).start(priority=1)
        return carry

    def wait(r, carry):
        copy_for_row(r).wait()
        return carry

    lax.fori_loop(0, MOVE_TM // 2, start, 0, unroll=4)
    lax.fori_loop(0, MOVE_TM, wait, 0, unroll=8)


def _scatter_rows_kernel(pos_ref, src_ref, init_ref, dst_ref, sem):
    del init_ref
    _move_rows(lambda r: pltpu.make_async_copy(
        src_ref.at[pl.ds(r, 1)], dst_ref.at[pl.ds(pos_ref[r], 1)], sem))


def _scatter_rows(pos, src, init):
    t = src.shape[0]
    return pl.pallas_call(
        _scatter_rows_kernel,
        grid=(t // MOVE_TM,),
        in_specs=[
            pl.BlockSpec((MOVE_TM,), lambda i: (i,), memory_space=pltpu.SMEM),
            pl.BlockSpec((MOVE_TM, src.shape[1]), lambda i: (i, 0)),
            pl.BlockSpec(memory_space=pl.ANY),
        ],
        out_specs=pl.BlockSpec(memory_space=pl.ANY),
        out_shape=jax.ShapeDtypeStruct(init.shape, init.dtype),
        scratch_shapes=[pltpu.SemaphoreType.DMA(())],
        input_output_aliases={2: 0},
        compiler_params=_params("arbitrary"),
        name="moe_scatter_rows",
    )(pos, src, init)


def _gather_rows_kernel(pos_ref, src_ref, dst_ref, sem):
    _move_rows(lambda r: pltpu.make_async_copy(
        src_ref.at[pl.ds(pos_ref[r], 1)], dst_ref.at[pl.ds(r, 1)], sem))


def _gather_rows(pos, src):
    t = pos.shape[0]
    return pl.pallas_call(
        _gather_rows_kernel,
        grid=(t // MOVE_TM,),
        in_specs=[
            pl.BlockSpec((MOVE_TM,), lambda i: (i,), memory_space=pltpu.SMEM),
            pl.BlockSpec(memory_space=pl.ANY),
        ],
        out_specs=pl.BlockSpec((MOVE_TM, src.shape[1]), lambda i: (i, 0)),
        out_shape=jax.ShapeDtypeStruct((t, src.shape[1]), src.dtype),
        scratch_shapes=[pltpu.SemaphoreType.DMA(())],
        compiler_params=_params("arbitrary"),
        name="moe_gather_rows",
    )(pos, src)


def _experts_kernel(ea_ref, eb_ref, used_ref, xs_ref, nf_ref,
                    wga_ref, wua_ref, wda_ref, wgb_ref, wub_ref, wdb_ref, ys_ref):
    del ea_ref, eb_ref
    k = pl.program_id(0)

    @pl.when(k < used_ref[0])
    def _():
        x = xs_ref[:, :D_MODEL]
        gates = xs_ref[:, D_MODEL:]
        h = _rms_rows(x, nf_ref[...]).astype(BF16)

        def ffn(wg_ref, wu_ref, wd_ref):
            a = jnp.dot(h, wg_ref[...], preferred_element_type=F32)
            u = jnp.dot(h, wu_ref[...], preferred_element_type=F32)
            return jnp.dot((a * jax.nn.sigmoid(a) * u).astype(BF16), wd_ref[...],
                           preferred_element_type=F32)

        moe = (gates[:, 0:1] * ffn(wga_ref, wua_ref, wda_ref)
               + gates[:, 1:2] * ffn(wgb_ref, wub_ref, wdb_ref))
        ys_ref[...] = x + moe

    @pl.when(k >= used_ref[0])
    def _():
        ys_ref[...] = jnp.zeros_like(ys_ref)


def _experts(xs, norm_ffn, w_gate, w_up, w_down, expert_a, expert_b, n_used):
    n_tiles = xs.shape[0] // EXP_TR
    up = lambda sel: pl.BlockSpec((None, D_MODEL, D_FF), lambda k, ea, eb, nu: (sel(ea, eb)[k], 0, 0))
    down = lambda sel: pl.BlockSpec((None, D_FF, D_MODEL), lambda k, ea, eb, nu: (sel(ea, eb)[k], 0, 0))
    first = lambda ea, eb: ea
    second = lambda ea, eb: eb
    grid_spec = pltpu.PrefetchScalarGridSpec(
        num_scalar_prefetch=3,
        grid=(n_tiles,),
        in_specs=[
            pl.BlockSpec((EXP_TR, ROW_WIDTH), lambda k, ea, eb, nu: (k, 0)),
            pl.BlockSpec((1, D_MODEL), lambda k, ea, eb, nu: (0, 0)),
            up(first), up(first), down(first), up(second), up(second), down(second),
        ],
        out_specs=pl.BlockSpec((EXP_TR, D_MODEL), lambda k, ea, eb, nu: (k, 0)),
    )
    return pl.pallas_call(
        _experts_kernel,
        grid_spec=grid_spec,
        out_shape=jax.ShapeDtypeStruct((xs.shape[0], D_MODEL), F32),
        compiler_params=_params("arbitrary"),
        name="experts",
    )(expert_a, expert_b, n_used, xs, norm_ffn, w_gate, w_up, w_down, w_gate, w_up, w_down)


def _tile_experts(offs, sizes, n_tiles):
    ends = (offs[:N_CLASSES, 0] + sizes[:N_CLASSES, 0]).astype(jnp.int32)
    starts = jnp.arange(n_tiles, dtype=jnp.int32) * EXP_TR
    tile_cls = jnp.minimum(jnp.sum(ends[None, :] <= starts[:, None], axis=1), N_CLASSES - 1)
    group = tile_cls // PAIRS_PER_GROUP
    pair = tile_cls % PAIRS_PER_GROUP
    slot_a = jnp.array([a for a, _ in PAIRS], jnp.int32)[pair]
    slot_b = jnp.array([b for _, b in PAIRS], jnp.int32)[pair]
    expert_a = (group * EXPERTS_PER_GROUP + slot_a).astype(jnp.int32)
    expert_b = (group * EXPERTS_PER_GROUP + slot_b).astype(jnp.int32)
    return expert_a, expert_b, (ends[-1] // EXP_TR).reshape(1)


def _router_layout(router_w, router_bias):
    w = router_w.T.reshape(N_GROUPS, EXPERTS_PER_GROUP, D_MODEL)
    w = jnp.transpose(w, (1, 0, 2))
    w = jnp.pad(w, ((0, 0), (0, SUBLANES - N_GROUPS), (0, 0))).reshape(ROUTER_ROWS, D_MODEL)
    b = jnp.transpose(router_bias.reshape(N_GROUPS, EXPERTS_PER_GROUP), (1, 0))
    b = jnp.pad(b, ((0, 0), (0, SUBLANES - N_GROUPS)), constant_values=NEG_BIG)
    return w.astype(F32), b.reshape(ROUTER_ROWS, 1).astype(F32)


def kernel(x, mem, norm_mix, norm_mem, w_in, b_forget, fox_q_norm, fox_k_norm, mem_q_norm,
           mem_k_norm, w_mem_kv, w_o_ret, w_o_fox, w_o_mem, w_out, norm_ffn, router_w,
           router_bias, w_gate, w_up, w_down):
    batch, seq, d = x.shape
    mem_len = mem.shape[1]
    depth = w_in.shape[0]
    t = batch * seq
    ff_lo = 7 * D_MODEL
    ff_hi = ff_lo + FOX_HEADS

    cos, sin, intra, qd, kd, cd = _retention_tables(seq)
    rw_pad, rb_pad = _router_layout(router_w, router_bias)
    row = lambda v: v.reshape(1, -1).astype(F32)

    x2 = x.reshape(t, d)
    mem2 = mem.reshape(batch * mem_len, d)
    n_tiles = t // EXP_TR + N_CLASSES
    sorted_rows = jnp.zeros((n_tiles * EXP_TR, ROW_WIDTH), F32)
    for l in range(depth):
        w_main = jnp.concatenate([w_in[l, :, :ff_lo], w_in[l, :, ff_hi:]], axis=1).astype(BF16)
        w_ff = jnp.pad(w_in[l, :, ff_lo:ff_hi], ((0, 0), (0, LANES - FOX_HEADS))).astype(BF16)
        b_pad = jnp.pad(b_forget[l], (0, LANES - FOX_HEADS)).reshape(1, LANES)

        proj, ff = _in_projection(x2, row(norm_mix[l]), w_main, w_ff)
        cum = _forget_cumsum(ff, b_pad, batch, seq)
        ro = _retention(proj, cos, sin, intra, qd, kd, cd, batch, seq)
        bound = (8.16 * jnp.max(jnp.abs(fox_q_norm[l])) * jnp.max(jnp.abs(fox_k_norm[l])))
        first = _fox_first_blocks(cum, bound, batch, seq)
        fo = _fox_attention(proj, cum, bound.reshape(1).astype(F32), first,
                            row(jnp.tile(fox_q_norm[l], 2)), row(jnp.tile(fox_k_norm[l], 2)),
                            batch, seq)
        mk, mv = _mem_kv(mem2, row(norm_mem[l]), w_mem_kv[l].astype(BF16), row(mem_k_norm[l]))
        mo = _mem_attention(proj, mk, mv, row(mem_q_norm[l]), batch, seq, mem_len)
        xa, cls = _mix_and_route(
            ro, fo, mo, proj, x2, w_o_ret[l].astype(BF16), w_o_fox[l].astype(BF16),
            w_o_mem[l].astype(BF16), w_out[l].astype(BF16), row(norm_ffn[l]), rw_pad, rb_pad)
        pos, offs, sizes = _positions(cls)
        pos = pos.reshape(t)
        expert_a, expert_b, n_used = _tile_experts(offs, sizes, n_tiles)
        sorted_rows = _scatter_rows(pos, xa, sorted_rows)
        ys = _experts(sorted_rows, row(norm_ffn[l]), w_gate[l].astype(BF16), w_up[l].astype(BF16),
                      w_down[l].astype(BF16), expert_a, expert_b, n_used)
        x2 = _gather_rows(pos, ys)
    return x2.reshape(batch, seq, d)
```

```python
import functools

import jax
import jax.numpy as jnp
from jax import lax
from jax.experimental import pallas as pl
from jax.experimental.pallas import tpu as pltpu

F32 = jnp.float32
BF16 = jnp.bfloat16

D_MODEL = 1024
EPS = 1e-6
RET_HEADS = 4
RET_HEAD_DIM = 256
RET_CHUNK = 128
ROPE_BASE = 10000.0
FOX_HEADS = 16
FOX_HEAD_DIM = 64
MEM_HEADS = 4
MEM_HEAD_DIM = 256
N_EXPERTS = 16
N_GROUPS = 4
EXPERTS_PER_GROUP = 4
D_FF = 512

LANES = 128
SUBLANES = 8
VMEM_LIMIT = 56 * 1024 * 1024
NEG_BIG = -1e30

NT_DIMS = (((1,), (1,)), ((), ()))


def _params(*sem):
    return pltpu.CompilerParams(dimension_semantics=sem, vmem_limit_bytes=VMEM_LIMIT)


def _rms_rows(x, gain_row):
    ms = jnp.mean(x * x, axis=-1, keepdims=True)
    return x * lax.rsqrt(ms + EPS) * gain_row


IN_TM = 2048
IN_BLOCKS = 11


def _inproj_kernel(x_ref, g_ref, w_ref, wff_ref, out_ref, ff_ref, hn_ref):
    @pl.when(pl.program_id(1) == 0)
    def _():
        hb = _rms_rows(x_ref[...], g_ref[...]).astype(BF16)
        hn_ref[...] = hb
        ff_ref[...] = jnp.dot(hb, wff_ref[...], preferred_element_type=F32)

    out_ref[...] = jnp.dot(hn_ref[...], w_ref[...], preferred_element_type=F32).astype(BF16)


def _in_projection(x2, gain, w_main, w_ff):
    t = x2.shape[0]
    return pl.pallas_call(
        _inproj_kernel,
        grid=(t // IN_TM, IN_BLOCKS),
        in_specs=[
            pl.BlockSpec((IN_TM, D_MODEL), lambda i, j: (i, 0)),
            pl.BlockSpec((1, D_MODEL), lambda i, j: (0, 0)),
            pl.BlockSpec((D_MODEL, D_MODEL), lambda i, j: (0, j)),
            pl.BlockSpec((D_MODEL, LANES), lambda i, j: (0, 0)),
        ],
        out_specs=[
            pl.BlockSpec((IN_TM, D_MODEL), lambda i, j: (i, j)),
            pl.BlockSpec((IN_TM, LANES), lambda i, j: (i, 0)),
        ],
        out_shape=[
            jax.ShapeDtypeStruct((t, IN_BLOCKS * D_MODEL), BF16),
            jax.ShapeDtypeStruct((t, LANES), F32),
        ],
        scratch_shapes=[pltpu.VMEM((IN_TM, D_MODEL), BF16)],
        compiler_params=_params("parallel", "arbitrary"),
        name="in_projection",
    )(x2, gain, w_main, w_ff)


CUM_BLOCK = 512


def _cumsum_kernel(ff_ref, b_ref, out_ref, split_ref):
    s = ff_ref.shape[0]
    r = lax.broadcasted_iota(jnp.int32, (CUM_BLOCK, CUM_BLOCK), 0)
    c = lax.broadcasted_iota(jnp.int32, (CUM_BLOCK, CUM_BLOCK), 1)
    lower = (c <= r).astype(F32)
    carry = jnp.zeros((1, LANES), F32)
    for blk in range(s // CUM_BLOCK):
        rows = slice(blk * CUM_BLOCK, (blk + 1) * CUM_BLOCK)
        z = ff_ref[rows, :] + b_ref[...]
        log_f = jnp.minimum(z, 0.0) - jnp.log1p(jnp.exp(-jnp.abs(z)))
        cs = jnp.dot(lower, log_f, preferred_element_type=F32,
                     precision=lax.Precision.HIGHEST) + carry
        out_ref[rows, :] = cs
        for k, part in enumerate(_split3(cs)):
            split_ref[k, rows, :] = part
        carry = cs[CUM_BLOCK - 1:CUM_BLOCK, :]


def _forget_cumsum(ff, b_pad, batch, seq):
    return pl.pallas_call(
        _cumsum_kernel,
        grid=(batch,),
        in_specs=[
            pl.BlockSpec((seq, LANES), lambda b: (b, 0)),
            pl.BlockSpec((1, LANES), lambda b: (0, 0)),
        ],
        out_specs=[
            pl.BlockSpec((seq, LANES), lambda b: (b, 0)),
            pl.BlockSpec((AUG_SPLIT, seq, LANES), lambda b: (0, b, 0)),
        ],
        out_shape=[
            jax.ShapeDtypeStruct((batch * seq, LANES), F32),
            jax.ShapeDtypeStruct((AUG_SPLIT, batch * seq, LANES), BF16),
        ],
        compiler_params=_params("parallel"),
        name="forget_cumsum",
    )(ff, b_pad)


RET_ROWS = 1024
RET_GROUP = 4


def _retention_kernel(q_ref, k_ref, v_ref, g_ref, cos_ref, sin_ref, intra_ref, qd_ref, kd_ref,
                      cd_ref, out_ref, state_ref):
    hd = RET_HEAD_DIM
    half = hd // 2

    @pl.when(pl.program_id(2) == 0)
    def _():
        state_ref[...] = jnp.zeros_like(state_ref)

    def chunk(c, carry):
        r0 = pl.multiple_of(c * RET_CHUNK, RET_CHUNK)
        rows = pl.ds(r0, RET_CHUNK)
        cs = cos_ref[rows, :]
        sn = sin_ref[rows, :]

        def rot(x):
            x1 = x[:, :half]
            x2 = x[:, half:]
            return jnp.concatenate([x1 * cs - x2 * sn, x2 * cs + x1 * sn], axis=-1)

        for h in range(RET_GROUP):
            cols = slice(h * hd, (h + 1) * hd)
            qr = rot(q_ref[rows, cols].astype(F32))
            kr = rot(k_ref[rows, cols].astype(F32))
            v = v_ref[rows, cols]
            scores = lax.dot_general(qr.astype(BF16), kr.astype(BF16), NT_DIMS,
                                     preferred_element_type=F32) * intra_ref[h]
            st = state_ref[h]
            o = (jnp.dot(scores.astype(BF16), v, preferred_element_type=F32)
                 + jnp.dot((qr * qd_ref[h]).astype(BF16), st.astype(BF16),
                           preferred_element_type=F32))
            kd_t = (kr * kd_ref[h]).T.astype(BF16)
            state_ref[h] = st * cd_ref[h] + jnp.dot(kd_t, v, preferred_element_type=F32)
            on = o * lax.rsqrt(jnp.mean(o * o, axis=-1, keepdims=True) + EPS)
            g = g_ref[rows, cols].astype(F32)
            out_ref[rows, cols] = (on * (g * jax.nn.sigmoid(g))).astype(BF16)
        return carry

    lax.fori_loop(0, RET_ROWS // RET_CHUNK, chunk, 0)


def _retention(proj, cos, sin, intra, qd, kd, cd, batch, seq):
    t = batch * seq
    hd = RET_HEAD_DIM
    width = RET_GROUP * hd
    groups = RET_HEADS // RET_GROUP
    steps = seq // RET_ROWS
    col = lambda off: (lambda b, p, s: (b * steps + s, off + p))
    tab = lambda b, p, s: (p, 0, 0)
    angle = pl.BlockSpec((RET_ROWS, hd // 2), lambda b, p, s: (s, 0))
    return pl.pallas_call(
        _retention_kernel,
        grid=(batch, groups, steps),
        in_specs=[
            pl.BlockSpec((RET_ROWS, width), col(0)),
            pl.BlockSpec((RET_ROWS, width), col(groups)),
            pl.BlockSpec((RET_ROWS, width), col(2 * groups)),
            pl.BlockSpec((RET_ROWS, width), col(3 * groups)),
            angle, angle,
            pl.BlockSpec((RET_GROUP, RET_CHUNK, RET_CHUNK), tab),
            pl.BlockSpec((RET_GROUP, RET_CHUNK, hd), tab),
            pl.BlockSpec((RET_GROUP, RET_CHUNK, hd), tab),
            pl.BlockSpec((RET_GROUP, 1, hd), tab),
        ],
        out_specs=pl.BlockSpec((RET_ROWS, width), lambda b, p, s: (b * steps + s, p)),
        out_shape=jax.ShapeDtypeStruct((t, RET_HEADS * hd), BF16),
        scratch_shapes=[pltpu.VMEM((RET_GROUP, hd, hd), F32)],
        compiler_params=_params("parallel", "parallel", "arbitrary"),
        name="retention",
    )(proj, proj, proj, proj, cos, sin, intra, qd, kd, cd)


def _retention_tables(seq):
    h = jnp.arange(RET_HEADS, dtype=F32)
    log_gamma = jnp.log1p(-(2.0 ** (-5.0 - h)))
    idx = jnp.arange(RET_CHUNK, dtype=F32)
    diff = idx[:, None] - idx[None, :]
    scale = RET_HEAD_DIM ** -0.5
    intra = jnp.where(diff >= 0, jnp.exp(log_gamma[:, None, None] * jnp.maximum(diff, 0.0)), 0.0) * scale
    q_decay = jnp.exp(log_gamma[:, None] * (idx + 1.0))
    k_decay = jnp.exp(log_gamma[:, None] * (RET_CHUNK - 1.0 - idx)) * scale
    chunk_decay = jnp.exp(log_gamma * RET_CHUNK)
    qd = jnp.broadcast_to(q_decay[:, :, None], (RET_HEADS, RET_CHUNK, RET_HEAD_DIM))
    kd = jnp.broadcast_to(k_decay[:, :, None], (RET_HEADS, RET_CHUNK, RET_HEAD_DIM))
    cd = jnp.broadcast_to(chunk_decay[:, None, None], (RET_HEADS, 1, RET_HEAD_DIM))
    inv_freq = ROPE_BASE ** (-jnp.arange(0, RET_HEAD_DIM, 2, dtype=F32) / RET_HEAD_DIM)
    ang = jnp.arange(seq, dtype=F32)[:, None] * inv_freq[None, :]
    return jnp.cos(ang), jnp.sin(ang), intra, qd, kd, cd


FOX_TQ = 512
FOX_TK = 512
assert FOX_TQ == FOX_TK
FOX_QSTEP = 2


def _pair_rms(x, lo_mask, gain_row):
    x2 = x * x
    s_all = jnp.sum(x2, axis=-1, keepdims=True)
    s_lo = jnp.sum(jnp.where(lo_mask, x2, 0.0), axis=-1, keepdims=True)
    ms = jnp.where(lo_mask, s_lo, s_all - s_lo) * (1.0 / FOX_HEAD_DIM)
    return x * lax.rsqrt(ms + EPS) * gain_row


AUG_CQ = 0
AUG_CK = 3
AUG_M = 6
AUG_SPLIT = 3
FOX_EXACT_ABOVE = 30.0
FOX_UNDERFLOW_LOG = -88.0


def _split3(x):
    hi = x.astype(BF16)
    r1 = x - hi.astype(F32)
    mid = r1.astype(BF16)
    lo = (r1 - mid.astype(F32)).astype(BF16)
    return hi, mid, lo


def _lanes_in(lane, start, count=AUG_SPLIT):
    return (lane >= start) & (lane < start + count)


def _fox_kernel(bound_ref, first_ref, q_ref, k_ref, v_ref, c_ref, qg_ref, kg_ref, out_ref,
                kaug_ref, caq_ref, vat_ref, qb_ref, qat_ref, acc_ref, m_ref):
    seq = k_ref.shape[0]
    hp = pl.program_id(1)
    step = pl.program_id(2)
    lane = lax.broadcasted_iota(jnp.int32, (1, LANES), 1)
    lo = lane < FOX_HEAD_DIM
    head_lanes = (lo, jnp.logical_not(lo))
    spare = (FOX_HEAD_DIM, 0)
    bound = bound_ref[0]

    @pl.when(step == 0)
    def _prepare_keys_values():
        r = lax.broadcasted_iota(jnp.int32, (LANES, LANES), 0)
        c = lax.broadcasted_iota(jnp.int32, (LANES, LANES), 1)
        place = []
        for k in range(AUG_SPLIT):
            pk = jnp.zeros((LANES, LANES), F32)
            for h in range(2):
                src = r == 2 * hp + h
                pk = pk + jnp.where(src & (c == spare[h] + AUG_CQ + k), 1.0, 0.0)
                pk = pk - jnp.where(src & (c == spare[h] + AUG_CK + k), 1.0, 0.0)
            place.append(pk.astype(BF16))
        bound_parts = [p.astype(F32) for p in _split3(jnp.full((1, LANES), -bound, F32))]

        def prep(i, carry):
            rows = pl.ds(pl.multiple_of(i * FOX_TK, FOX_TK), FOX_TK)
            e = sum(jnp.dot(c_ref[k, rows, :], place[k], preferred_element_type=F32)
                    for k in range(AUG_SPLIT))
            kn = _pair_rms(k_ref[rows, :].astype(F32), lo, kg_ref[...])
            v = v_ref[rows, :]
            one = jnp.ones_like(v)
            for h in range(2):
                ones_k = jnp.where(_lanes_in(lane, spare[h] + AUG_CQ)
                                   | _lanes_in(lane, spare[h] + AUG_M), 1.0, 0.0)
                kaug_ref[h, rows, :] = jnp.where(
                    head_lanes[h], kn,
                    jnp.where(_lanes_in(lane, spare[h] + AUG_CK), e, ones_k)).astype(BF16)
                const_q = jnp.where(_lanes_in(lane, spare[h] + AUG_CK), 1.0, 0.0)
                for k in range(AUG_SPLIT):
                    const_q = jnp.where(lane == spare[h] + AUG_M + k, bound_parts[k], const_q)
                caq_ref[h, rows, :] = jnp.where(
                    _lanes_in(lane, spare[h] + AUG_CQ), e, const_q).astype(BF16)
                va = jnp.where(head_lanes[h], v, one)
                vat_ref[h, i] = va.astype(F32).T.astype(BF16)
            return carry
        lax.fori_loop(0, seq // FOX_TK, prep, 0)

    def query_block(sub):
        qi = step * FOX_QSTEP + sub
        first = first_ref[(pl.program_id(0) * pl.num_programs(1) + hp)
                          * (pl.num_programs(2) * FOX_QSTEP) + qi]
        local_rows = pl.ds(sub * FOX_TQ, FOX_TQ)
        t0 = pl.multiple_of(qi * FOX_TQ, FOX_TQ)
        q_rows = pl.ds(t0, FOX_TQ)
        qn = _pair_rms(q_ref[local_rows, :].astype(F32), lo, qg_ref[...]) * (FOX_HEAD_DIM ** -0.5)
        kq = lax.broadcasted_iota(jnp.int32, (FOX_TK, FOX_TQ), 0)
        qq = lax.broadcasted_iota(jnp.int32, (FOX_TK, FOX_TQ), 1)

        q_aug = [jnp.where(head_lanes[h], qn, caq_ref[h, q_rows, :].astype(F32)) for h in range(2)]
        for h in range(2):
            qat_ref[h] = q_aug[h].T.astype(BF16)

        @pl.when(bound > FOX_EXACT_ABOVE)
        def _exact_row_max():
            for h in range(2):
                qb_ref[h] = jnp.where(_lanes_in(lane, spare[h] + AUG_M), 0.0, q_aug[h]).astype(BF16)
                m_ref[h] = jnp.full((FOX_TQ, 1), NEG_BIG, F32)

            def scan(j, masked):
                keys = pl.ds(pl.multiple_of(j * FOX_TK, FOX_TK), FOX_TK)
                for h in range(2):
                    s = lax.dot_general(qb_ref[h], kaug_ref[h, keys, :], NT_DIMS,
                                        preferred_element_type=F32)
                    if masked:
                        s = jnp.where(qq <= kq, s, NEG_BIG)
                    m_ref[h] = jnp.maximum(m_ref[h], jnp.max(s, axis=-1, keepdims=True))

            def scan_body(j, carry):
                scan(j, False)
                return carry
            lax.fori_loop(first, qi, scan_body, 0)
            scan(qi, True)
            for h in range(2):
                m_parts = _split3(-m_ref[h])
                qa = q_aug[h]
                for k in range(AUG_SPLIT):
                    qa = jnp.where(lane == spare[h] + AUG_M + k, m_parts[k].astype(F32), qa)
                qat_ref[h] = qa.T.astype(BF16)

        acc_ref[...] = jnp.zeros_like(acc_ref)

        def probs(h, keys, q_cols=slice(None)):
            st = jnp.dot(kaug_ref[h, keys, :], qat_ref[h, :, q_cols], preferred_element_type=F32)
            return jnp.exp(st)

        def kv_blocks(blocks):
            for h in range(2):
                total = None
                for j in blocks:
                    keys = pl.ds(pl.multiple_of(j * FOX_TK, FOX_TK), FOX_TK)
                    o = jnp.dot(vat_ref[h, j], probs(h, keys).astype(BF16),
                                preferred_element_type=F32)
                    total = o if total is None else total + o
                acc_ref[h] += total

        def pair(i, carry):
            kv_blocks((first + 2 * i, first + 2 * i + 1))
            return carry

        full_blocks = qi - first
        lax.fori_loop(0, lax.shift_right_logical(full_blocks, 1), pair, 0)

        @pl.when((full_blocks & 1) == 1)
        def _odd_block():
            kv_blocks((qi - 1,))

        half = FOX_TK // 2
        lower, upper = slice(0, half), slice(half, FOX_TK)
        causal_a = (lax.broadcasted_iota(jnp.int32, (half, FOX_TQ), 0)
                    <= lax.broadcasted_iota(jnp.int32, (half, FOX_TQ), 1))
        causal_b = (lax.broadcasted_iota(jnp.int32, (half, half), 0)
                    <= lax.broadcasted_iota(jnp.int32, (half, half), 1))
        for h in range(2):
            p_a = jnp.where(causal_a, probs(h, pl.ds(t0, half)), 0.0)
            p_b = jnp.where(causal_b, probs(h, pl.ds(t0 + half, half), upper), 0.0)
            acc_ref[h] += jnp.dot(vat_ref[h, qi, :, lower], p_a.astype(BF16),
                                  preferred_element_type=F32)
            acc_ref[h, :, upper] += jnp.dot(vat_ref[h, qi, :, upper], p_b.astype(BF16),
                                            preferred_element_type=F32)

        o0 = acc_ref[0]
        o1 = acc_ref[1]
        l0 = o0[FOX_HEAD_DIM:FOX_HEAD_DIM + 1, :]
        l1 = o1[0:1, :]
        row = lax.broadcasted_iota(jnp.int32, (LANES, 1), 0)
        out_ref[local_rows, :] = jnp.where(row < FOX_HEAD_DIM, o0 / l0, o1 / l1).T.astype(BF16)

    for sub in range(FOX_QSTEP):
        query_block(sub)


def _fox_first_blocks(cum, bound, batch, seq):
    blocks = seq // FOX_TK
    c = cum.reshape(batch, seq, LANES)[:, :, :FOX_HEADS]
    c_start = c[:, 0::FOX_TQ, :]
    c_end = c[:, FOX_TK - 1::FOX_TK, :]
    negligible = (c_start[:, :, None, :] - c_end[:, None, :, :] + 2.0 * bound) < FOX_UNDERFLOW_LOG
    idx = jnp.arange(blocks)
    earlier = (idx[None, :] < idx[:, None])[None, :, :, None]
    count = jnp.sum(negligible & earlier, axis=2)
    first = jnp.min(count.reshape(batch, blocks, FOX_HEADS // 2, 2), axis=-1)
    return jnp.transpose(first, (0, 2, 1)).reshape(-1).astype(jnp.int32)


def _fox_attention(proj, cum_split, bound, first, qg, kg, batch, seq):
    t = batch * seq
    nq = seq // (FOX_QSTEP * FOX_TQ)
    pairs = FOX_HEADS // 2
    base = 4 * (D_MODEL // LANES)
    return pl.pallas_call(
        _fox_kernel,
        grid=(batch, pairs, nq),
        in_specs=[
            pl.BlockSpec(memory_space=pltpu.SMEM),
            pl.BlockSpec(memory_space=pltpu.SMEM),
            pl.BlockSpec((FOX_QSTEP * FOX_TQ, LANES), lambda b, p, i: (b * nq + i, base + p)),
            pl.BlockSpec((seq, LANES), lambda b, p, i: (b, base + pairs + p)),
            pl.BlockSpec((seq, LANES), lambda b, p, i: (b, base + 2 * pairs + p)),
            pl.BlockSpec((AUG_SPLIT, seq, LANES), lambda b, p, i: (0, b, 0)),
            pl.BlockSpec((1, LANES), lambda b, p, i: (0, 0)),
            pl.BlockSpec((1, LANES), lambda b, p, i: (0, 0)),
        ],
        out_specs=pl.BlockSpec((FOX_QSTEP * FOX_TQ, LANES), lambda b, p, i: (b * nq + i, p)),
        out_shape=jax.ShapeDtypeStruct((t, FOX_HEADS * FOX_HEAD_DIM), BF16),
        scratch_shapes=[
            pltpu.VMEM((2, seq, LANES), BF16),
            pltpu.VMEM((2, seq, LANES), BF16),
            pltpu.VMEM((2, seq // FOX_TK, LANES, FOX_TK), BF16),
            pltpu.VMEM((2, FOX_TQ, LANES), BF16),
            pltpu.VMEM((2, LANES, FOX_TQ), BF16),
            pltpu.VMEM((2, LANES, FOX_TQ), F32),
            pltpu.VMEM((2, FOX_TQ, 1), F32),
        ],
        compiler_params=_params("parallel", "parallel", "arbitrary"),
        name="fox_attention",
    )(bound, first, proj, proj, proj, cum_split, qg, kg)


MEMKV_TM = 512
MEMATT_TQ = 1024


def _mem_kv_kernel(mem_ref, g_ref, w_ref, kg_ref, k_out, v_out):
    hb = _rms_rows(mem_ref[...], g_ref[...]).astype(BF16)
    kv = jnp.dot(hb, w_ref[...], preferred_element_type=F32)
    width = MEM_HEADS * MEM_HEAD_DIM
    for h in range(MEM_HEADS):
        cols = slice(h * MEM_HEAD_DIM, (h + 1) * MEM_HEAD_DIM)
        k_out[:, cols] = _rms_rows(kv[:, cols], kg_ref[...]).astype(BF16)
    v_out[...] = kv[:, width:].astype(BF16)


def _mem_kv(mem2, gain, w_kv, k_gain):
    rows = mem2.shape[0]
    width = MEM_HEADS * MEM_HEAD_DIM
    return pl.pallas_call(
        _mem_kv_kernel,
        grid=(rows // MEMKV_TM,),
        in_specs=[
            pl.BlockSpec((MEMKV_TM, D_MODEL), lambda i: (i, 0)),
            pl.BlockSpec((1, D_MODEL), lambda i: (0, 0)),
            pl.BlockSpec((D_MODEL, 2 * width), lambda i: (0, 0)),
            pl.BlockSpec((1, MEM_HEAD_DIM), lambda i: (0, 0)),
        ],
        out_specs=[pl.BlockSpec((MEMKV_TM, width), lambda i: (i, 0))] * 2,
        out_shape=[jax.ShapeDtypeStruct((rows, width), BF16)] * 2,
        compiler_params=_params("parallel"),
        name="mem_kv",
    )(mem2, gain, w_kv, k_gain)


def _mem_attn_kernel(q_ref, k_ref, v_ref, qg_ref, out_ref):
    for h in range(MEM_HEADS):
        cols = slice(h * MEM_HEAD_DIM, (h + 1) * MEM_HEAD_DIM)
        qn = _rms_rows(q_ref[:, cols].astype(F32), qg_ref[...]) * (MEM_HEAD_DIM ** -0.5)
        s = lax.dot_general(qn.astype(BF16), k_ref[:, cols], NT_DIMS, preferred_element_type=F32)
        p = jnp.exp(s - jnp.max(s, axis=-1, keepdims=True))
        denom = jnp.sum(p, axis=-1, keepdims=True)
        o = jnp.dot(p.astype(BF16), v_ref[:, cols], preferred_element_type=F32)
        out_ref[:, cols] = (o / denom).astype(BF16)


def _mem_attention(proj, mk, mv, q_gain, batch, seq, mem_len):
    t = batch * seq
    nq = seq // MEMATT_TQ
    width = MEM_HEADS * MEM_HEAD_DIM
    return pl.pallas_call(
        _mem_attn_kernel,
        grid=(batch, nq),
        in_specs=[
            pl.BlockSpec((MEMATT_TQ, width), lambda b, i: (b * nq + i, 7)),
            pl.BlockSpec((mem_len, width), lambda b, i: (b, 0)),
            pl.BlockSpec((mem_len, width), lambda b, i: (b, 0)),
            pl.BlockSpec((1, MEM_HEAD_DIM), lambda b, i: (0, 0)),
        ],
        out_specs=pl.BlockSpec((MEMATT_TQ, width), lambda b, i: (b * nq + i, 0)),
        out_shape=jax.ShapeDtypeStruct((t, width), BF16),
        compiler_params=_params("parallel", "parallel"),
        name="mem_attention",
    )(proj, mk, mv, q_gain)


MIX_TM = 512
ROUTER_ROWS = 32
PAIRS = ((0, 1), (0, 2), (0, 3), (1, 2), (1, 3), (2, 3))
PAIRS_PER_GROUP = len(PAIRS)
N_CLASSES = N_GROUPS * PAIRS_PER_GROUP
ROW_WIDTH = D_MODEL + LANES


def _top2_sum(b0, b1, b2, b3):
    p, q = jnp.maximum(b0, b1), jnp.minimum(b0, b1)
    r, s = jnp.maximum(b2, b3), jnp.minimum(b2, b3)
    return jnp.maximum(p, r) + jnp.maximum(jnp.minimum(p, r), jnp.maximum(q, s))


def _mix_kernel(ro_ref, fo_ref, mo_ref, gr_ref, gf_ref, gm_ref, x_ref,
                wr_ref, wf_ref, wm_ref, wo_ref, nf_ref, rw_ref, rb_ref,
                xa_ref, cls_ref):
    def branch(a_ref, w_ref, g_ref):
        y = jnp.dot(a_ref[...], w_ref[...], preferred_element_type=F32)
        return jax.nn.sigmoid(g_ref[...].astype(F32)) * y

    merged = (branch(ro_ref, wr_ref, gr_ref) + branch(fo_ref, wf_ref, gf_ref)
              + branch(mo_ref, wm_ref, gm_ref))
    xn = x_ref[...] + jnp.dot(merged.astype(BF16), wo_ref[...], preferred_element_type=F32)
    xa_ref[:, :D_MODEL] = xn
    h = _rms_rows(xn, nf_ref[...])
    h_hi = h.astype(BF16)

    h_lo = (h - h_hi.astype(F32)).astype(BF16)
    rw = rw_ref[...]
    rw_hi = rw.astype(BF16)
    rw_lo = (rw - rw_hi.astype(F32)).astype(BF16)
    dot_nt = lambda a, b: lax.dot_general(a, b, NT_DIMS, preferred_element_type=F32)
    logits = dot_nt(rw_hi, h_hi) + dot_nt(rw_hi, h_lo) + dot_nt(rw_lo, h_hi)
    scores = jax.nn.sigmoid(logits)
    biased = scores + rb_ref[...]
    n = EXPERTS_PER_GROUP
    sc = [scores[SUBLANES * j:SUBLANES * (j + 1)] for j in range(n)]
    bi = [biased[SUBLANES * j:SUBLANES * (j + 1)] for j in range(n)]
    group_score = _top2_sum(*bi)
    rows = lax.broadcasted_iota(jnp.int32, group_score.shape, 0)
    best = jnp.max(group_score, axis=0, keepdims=True)
    top_group = jnp.min(jnp.where(group_score == best, rows, SUBLANES), axis=0, keepdims=True)
    in_group = rows == top_group
    picked, chosen = [], []
    for j in range(n):
        rank = jnp.zeros(group_score.shape, jnp.int32)
        for i in range(n):
            if i == j:
                continue
            ahead = (bi[i] > bi[j]) | ((bi[i] == bi[j]) & (i < j))
            rank = rank + ahead.astype(jnp.int32)
        sel = in_group & (rank < 2)
        picked.append(jnp.where(sel, sc[j], 0.0))
        chosen.append(jnp.max(jnp.where(sel, 1.0, 0.0), axis=0, keepdims=True) > 0.5)
    denom = jnp.sum(picked[0] + picked[1] + picked[2] + picked[3], axis=0, keepdims=True)
    gate = [jnp.sum(picked[j], axis=0, keepdims=True) / denom for j in range(n)]

    first = jnp.full(top_group.shape, n, jnp.int32)
    second = jnp.full(top_group.shape, -1, jnp.int32)
    for j in range(n):
        first = jnp.minimum(first, jnp.where(chosen[j], j, n))
        second = jnp.maximum(second, jnp.where(chosen[j], j, -1))
    pair_base = jnp.where(first == 0, 0, jnp.where(first == 1, 3, 5))
    cls = top_group * PAIRS_PER_GROUP + pair_base + second - first - 1
    cls_ref[...] = jnp.clip(cls, 0, N_CLASSES - 1)
    g_first = sum(jnp.where(first == j, gate[j], 0.0) for j in range(n))
    g_second = sum(jnp.where(second == j, gate[j], 0.0) for j in range(n))
    srow = lax.broadcasted_iota(jnp.int32, (SUBLANES, 1), 0)
    gates8 = jnp.where(srow == 0, g_first, jnp.where(srow == 1, g_second, 0.0))
    pad = jnp.zeros((LANES - SUBLANES, gates8.shape[1]), F32)
    xa_ref[:, D_MODEL:] = jnp.concatenate([gates8, pad], axis=0).T


def _mix_and_route(ro, fo, mo, proj, x2, w_r, w_f, w_m, w_o, norm_ffn, rw_pad, rb_pad):
    t = x2.shape[0]
    tok = lambda i: (i, 0)
    const = lambda i: (0, 0)
    act = pl.BlockSpec((MIX_TM, D_MODEL), tok)
    wspec = pl.BlockSpec((D_MODEL, D_MODEL), const)
    return pl.pallas_call(
        _mix_kernel,
        grid=(t // MIX_TM,),
        in_specs=[
            act, act, act,
            pl.BlockSpec((MIX_TM, D_MODEL), lambda i: (i, 8)),
            pl.BlockSpec((MIX_TM, D_MODEL), lambda i: (i, 9)),
            pl.BlockSpec((MIX_TM, D_MODEL), lambda i: (i, 10)),
            act,
            wspec, wspec, wspec, wspec,
            pl.BlockSpec((1, D_MODEL), const),
            pl.BlockSpec((ROUTER_ROWS, D_MODEL), const),
            pl.BlockSpec((ROUTER_ROWS, 1), const),
        ],
        out_specs=[
            pl.BlockSpec((MIX_TM, ROW_WIDTH), tok),
            pl.BlockSpec((1, MIX_TM), lambda i: (0, i)),
        ],
        out_shape=[
            jax.ShapeDtypeStruct((t, ROW_WIDTH), F32),
            jax.ShapeDtypeStruct((1, t), jnp.int32),
        ],
        compiler_params=_params("parallel"),
        name="mix_and_route",
    )(ro, fo, mo, proj, proj, proj, x2, w_r, w_f, w_m, w_o, norm_ffn, rw_pad, rb_pad)


EXP_TR = 256
POS_TM = 2048
POS_SUB = 512
MOVE_TM = 1024
CLASS_ROWS = 32


def _position_kernel(cls_ref, pos_ref, offs_ref, size_ref, counts_ref, running_ref):
    phase = pl.program_id(0)
    i = pl.program_id(1)

    def class_onehot(cls):
        rows = lax.broadcasted_iota(jnp.int32, (CLASS_ROWS, cls.shape[1]), 0)
        return jnp.where(rows == cls, 1.0, 0.0)

    @pl.when(phase == 0)
    def _count():
        @pl.when(i == 0)
        def _():
            counts_ref[...] = jnp.zeros_like(counts_ref)
        counts_ref[...] += jnp.sum(class_onehot(cls_ref[...]), axis=1, keepdims=True)
        pos_ref[...] = jnp.zeros_like(pos_ref)

    @pl.when(phase == 1)
    def _place():
        @pl.when(i == 0)
        def _():
            padded = jnp.floor((counts_ref[...] + (EXP_TR - 1)) * (1.0 / EXP_TR)) * EXP_TR
            r = lax.broadcasted_iota(jnp.int32, (CLASS_ROWS, CLASS_ROWS), 0)
            c = lax.broadcasted_iota(jnp.int32, (CLASS_ROWS, CLASS_ROWS), 1)
            below = jnp.where(c < r, 1.0, 0.0)
            offs = jnp.dot(below, padded, preferred_element_type=F32,
                           precision=lax.Precision.HIGHEST)
            offs_ref[...] = offs
            size_ref[...] = padded
            running_ref[...] = offs

        r = lax.broadcasted_iota(jnp.int32, (POS_SUB, POS_SUB), 0)
        c = lax.broadcasted_iota(jnp.int32, (POS_SUB, POS_SUB), 1)
        earlier = jnp.where(r < c, 1.0, 0.0).astype(BF16)
        running = running_ref[:, 0:1]
        for u in range(POS_TM // POS_SUB):
            cols = slice(u * POS_SUB, (u + 1) * POS_SUB)
            onehot = class_onehot(cls_ref[:, cols])
            prefix = jnp.dot(onehot.astype(BF16), earlier, preferred_element_type=F32)
            pos_ref[:, cols] = jnp.sum(onehot * (prefix + running), axis=0,
                                       keepdims=True).astype(jnp.int32)
            running = running + jnp.sum(onehot, axis=1, keepdims=True)
        running_ref[...] = jnp.broadcast_to(running, running_ref.shape)


def _positions(cls):
    t = cls.shape[1]
    meta = jax.ShapeDtypeStruct((CLASS_ROWS, LANES), F32)
    return pl.pallas_call(
        _position_kernel,
        grid=(2, t // POS_TM),
        in_specs=[pl.BlockSpec((1, POS_TM), lambda p, i: (0, i))],
        out_specs=[
            pl.BlockSpec((1, POS_TM), lambda p, i: (0, i * p)),
            pl.BlockSpec((CLASS_ROWS, LANES), lambda p, i: (0, 0)),
            pl.BlockSpec((CLASS_ROWS, LANES), lambda p, i: (0, 0)),
        ],
        out_shape=[jax.ShapeDtypeStruct((1, t), jnp.int32), meta, meta],
        scratch_shapes=[pltpu.VMEM((CLASS_ROWS, LANES), F32), pltpu.VMEM((CLASS_ROWS, LANES), F32)],
        compiler_params=_params("arbitrary", "arbitrary"),
        name="moe_positions",
    )(cls)


def _move_rows(copy_for_row):
    def start(k, carry):
        copy_for_row(2 * k).start(priority=0)
        copy_for_row(2 * k + 1).start(priority=1)
        return carry

    def wait(r, carry):
        copy_for_row(r).wait()
        return carry

    lax.fori_loop(0, MOVE_TM // 2, start, 0, unroll=4)
    lax.fori_loop(0, MOVE_TM, wait, 0, unroll=8)


def _scatter_rows_kernel(pos_ref, src_ref, init_ref, dst_ref, sem):
    del init_ref
    _move_rows(lambda r: pltpu.make_async_copy(
        src_ref.at[pl.ds(r, 1)], dst_ref.at[pl.ds(pos_ref[r], 1)], sem))


def _scatter_rows(pos, src, init):
    t = src.shape[0]
    return pl.pallas_call(
        _scatter_rows_kernel,
        grid=(t // MOVE_TM,),
        in_specs=[
            pl.BlockSpec((MOVE_TM,), lambda i: (i,), memory_space=pltpu.SMEM),
            pl.BlockSpec((MOVE_TM, src.shape[1]), lambda i: (i, 0)),
            pl.BlockSpec(memory_space=pl.ANY),
        ],
        out_specs=pl.BlockSpec(memory_space=pl.ANY),
        out_shape=jax.ShapeDtypeStruct(init.shape, init.dtype),
        scratch_shapes=[pltpu.SemaphoreType.DMA(())],
        input_output_aliases={2: 0},
        compiler_params=_params("arbitrary"),
        name="moe_scatter_rows",
    )(pos, src, init)


def _gather_rows_kernel(pos_ref, src_ref, dst_ref, sem):
    _move_rows(lambda r: pltpu.make_async_copy(
        src_ref.at[pl.ds(pos_ref[r], 1)], dst_ref.at[pl.ds(r, 1)], sem))


def _gather_rows(pos, src):
    t = pos.shape[0]
    return pl.pallas_call(
        _gather_rows_kernel,
        grid=(t // MOVE_TM,),
        in_specs=[
            pl.BlockSpec((MOVE_TM,), lambda i: (i,), memory_space=pltpu.SMEM),
            pl.BlockSpec(memory_space=pl.ANY),
        ],
        out_specs=pl.BlockSpec((MOVE_TM, src.shape[1]), lambda i: (i, 0)),
        out_shape=jax.ShapeDtypeStruct((t, src.shape[1]), src.dtype),
        scratch_shapes=[pltpu.SemaphoreType.DMA(())],
        compiler_params=_params("arbitrary"),
        name="moe_gather_rows",
    )(pos, src)


def _experts_kernel(ea_ref, eb_ref, used_ref, xs_ref, nf_ref,
                    wga_ref, wua_ref, wda_ref, wgb_ref, wub_ref, wdb_ref, ys_ref):
    del ea_ref, eb_ref
    k = pl.program_id(0)

    @pl.when(k < used_ref[0])
    def _():
        x = xs_ref[:, :D_MODEL]
        gates = xs_ref[:, D_MODEL:]
        h = _rms_rows(x, nf_ref[...]).astype(BF16)

        def ffn(wg_ref, wu_ref, wd_ref):
            a = jnp.dot(h, wg_ref[...], preferred_element_type=F32)
            u = jnp.dot(h, wu_ref[...], preferred_element_type=F32)
            return jnp.dot((a * jax.nn.sigmoid(a) * u).astype(BF16), wd_ref[...],
                           preferred_element_type=F32)

        moe = (gates[:, 0:1] * ffn(wga_ref, wua_ref, wda_ref)
               + gates[:, 1:2] * ffn(wgb_ref, wub_ref, wdb_ref))
        ys_ref[...] = x + moe

    @pl.when(k >= used_ref[0])
    def _():
        ys_ref[...] = jnp.zeros_like(ys_ref)


def _experts(xs, norm_ffn, w_gate, w_up, w_down, expert_a, expert_b, n_used):
    n_tiles = xs.shape[0] // EXP_TR
    up = lambda sel: pl.BlockSpec((None, D_MODEL, D_FF), lambda k, ea, eb, nu: (sel(ea, eb)[k], 0, 0))
    down = lambda sel: pl.BlockSpec((None, D_FF, D_MODEL), lambda k, ea, eb, nu: (sel(ea, eb)[k], 0, 0))
    first = lambda ea, eb: ea
    second = lambda ea, eb: eb
    grid_spec = pltpu.PrefetchScalarGridSpec(
        num_scalar_prefetch=3,
        grid=(n_tiles,),
        in_specs=[
            pl.BlockSpec((EXP_TR, ROW_WIDTH), lambda k, ea, eb, nu: (k, 0)),
            pl.BlockSpec((1, D_MODEL), lambda k, ea, eb, nu: (0, 0)),
            up(first), up(first), down(first), up(second), up(second), down(second),
        ],
        out_specs=pl.BlockSpec((EXP_TR, D_MODEL), lambda k, ea, eb, nu: (k, 0)),
    )
    return pl.pallas_call(
        _experts_kernel,
        grid_spec=grid_spec,
        out_shape=jax.ShapeDtypeStruct((xs.shape[0], D_MODEL), F32),
        compiler_params=_params("arbitrary"),
        name="experts",
    )(expert_a, expert_b, n_used, xs, norm_ffn, w_gate, w_up, w_down, w_gate, w_up, w_down)


def _tile_experts(offs, sizes, n_tiles):
    ends = (offs[:N_CLASSES, 0] + sizes[:N_CLASSES, 0]).astype(jnp.int32)
    starts = jnp.arange(n_tiles, dtype=jnp.int32) * EXP_TR
    tile_cls = jnp.minimum(jnp.sum(ends[None, :] <= starts[:, None], axis=1), N_CLASSES - 1)
    group = tile_cls // PAIRS_PER_GROUP
    pair = tile_cls % PAIRS_PER_GROUP
    slot_a = jnp.array([a for a, _ in PAIRS], jnp.int32)[pair]
    slot_b = jnp.array([b for _, b in PAIRS], jnp.int32)[pair]
    expert_a = (group * EXPERTS_PER_GROUP + slot_a).astype(jnp.int32)
    expert_b = (group * EXPERTS_PER_GROUP + slot_b).astype(jnp.int32)
    return expert_a, expert_b, (ends[-1] // EXP_TR).reshape(1)


def _router_layout(router_w, router_bias):
    w = router_w.T.reshape(N_GROUPS, EXPERTS_PER_GROUP, D_MODEL)
    w = jnp.transpose(w, (1, 0, 2))
    w = jnp.pad(w, ((0, 0), (0, SUBLANES - N_GROUPS), (0, 0))).reshape(ROUTER_ROWS, D_MODEL)
    b = jnp.transpose(router_bias.reshape(N_GROUPS, EXPERTS_PER_GROUP), (1, 0))
    b = jnp.pad(b, ((0, 0), (0, SUBLANES - N_GROUPS)), constant_values=NEG_BIG)
    return w.astype(F32), b.reshape(ROUTER_ROWS, 1).astype(F32)


def kernel(x, mem, norm_mix, norm_mem, w_in, b_forget, fox_q_norm, fox_k_norm, mem_q_norm,
           mem_k_norm, w_mem_kv, w_o_ret, w_o_fox, w_o_mem, w_out, norm_ffn, router_w,
           router_bias, w_gate, w_up, w_down):
    batch, seq, d = x.shape
    mem_len = mem.shape[1]
    depth = w_in.shape[0]
    t = batch * seq
    ff_lo = 7 * D_MODEL
    ff_hi = ff_lo + FOX_HEADS

    cos, sin, intra, qd, kd, cd = _retention_tables(seq)
    rw_pad, rb_pad = _router_layout(router_w, router_bias)
    row = lambda v: v.reshape(1, -1).astype(F32)

    x2 = x.reshape(t, d)
    mem2 = mem.reshape(batch * mem_len, d)
    n_tiles = t // EXP_TR + N_CLASSES
    sorted_rows = jnp.zeros((n_tiles * EXP_TR, ROW_WIDTH), F32)
    for l in range(depth):
        w_main = jnp.concatenate([w_in[l, :, :ff_lo], w_in[l, :, ff_hi:]], axis=1).astype(BF16)
        w_ff = jnp.pad(w_in[l, :, ff_lo:ff_hi], ((0, 0), (0, LANES - FOX_HEADS))).astype(BF16)
        b_pad = jnp.pad(b_forget[l], (0, LANES - FOX_HEADS)).reshape(1, LANES)

        proj, ff = _in_projection(x2, row(norm_mix[l]), w_main, w_ff)
        cum, cum_split = _forget_cumsum(ff, b_pad, batch, seq)
        ro = _retention(proj, cos, sin, intra, qd, kd, cd, batch, seq)
        bound = (8.16 * jnp.max(jnp.abs(fox_q_norm[l])) * jnp.max(jnp.abs(fox_k_norm[l])))
        first = _fox_first_blocks(cum, bound, batch, seq)
        fo = _fox_attention(proj, cum_split, bound.reshape(1).astype(F32), first,
                            row(jnp.tile(fox_q_norm[l], 2)), row(jnp.tile(fox_k_norm[l], 2)),
                            batch, seq)
        mk, mv = _mem_kv(mem2, row(norm_mem[l]), w_mem_kv[l].astype(BF16), row(mem_k_norm[l]))
        mo = _mem_attention(proj, mk, mv, row(mem_q_norm[l]), batch, seq, mem_len)
        xa, cls = _mix_and_route(
            ro, fo, mo, proj, x2, w_o_ret[l].astype(BF16), w_o_fox[l].astype(BF16),
            w_o_mem[l].astype(BF16), w_out[l].astype(BF16), row(norm_ffn[l]), rw_pad, rb_pad)
        pos, offs, sizes = _positions(cls)
        pos = pos.reshape(t)
        expert_a, expert_b, n_used = _tile_experts(offs, sizes, n_tiles)
        sorted_rows = _scatter_rows(pos, xa, sorted_rows)
        ys = _experts(sorted_rows, row(norm_ffn[l]), w_gate[l].astype(BF16), w_up[l].astype(BF16),
                      w_down[l].astype(BF16), expert_a, expert_b, n_used)
        x2 = _gather_rows(pos, ys)
    return x2.reshape(batch, seq, d)
```

```python
import functools

import jax
import jax.numpy as jnp
from jax import lax
from jax.experimental import pallas as pl
from jax.experimental.pallas import tpu as pltpu

F32 = jnp.float32
BF16 = jnp.bfloat16

D_MODEL = 1024
EPS = 1e-6
RET_HEADS = 4
RET_HEAD_DIM = 256
RET_CHUNK = 128
ROPE_BASE = 10000.0
FOX_HEADS = 16
FOX_HEAD_DIM = 64
MEM_HEADS = 4
MEM_HEAD_DIM = 256
N_EXPERTS = 16
N_GROUPS = 4
EXPERTS_PER_GROUP = 4
D_FF = 512

LANES = 128
SUBLANES = 8
VMEM_LIMIT = 56 * 1024 * 1024
NEG_BIG = -1e30

NT_DIMS = (((1,), (1,)), ((), ()))


def _params(*sem):
    return pltpu.CompilerParams(dimension_semantics=sem, vmem_limit_bytes=VMEM_LIMIT)


def _rms_rows(x, gain_row):
    ms = jnp.mean(x * x, axis=-1, keepdims=True)
    return x * lax.rsqrt(ms + EPS) * gain_row


IN_TM = 2048
IN_BLOCKS = 11


def _inproj_kernel(x_ref, g_ref, w_ref, wff_ref, out_ref, ff_ref, hn_ref):
    @pl.when(pl.program_id(1) == 0)
    def _():
        hb = _rms_rows(x_ref[...], g_ref[...]).astype(BF16)
        hn_ref[...] = hb
        ff_ref[...] = jnp.dot(hb, wff_ref[...], preferred_element_type=F32)

    out_ref[...] = jnp.dot(hn_ref[...], w_ref[...], preferred_element_type=F32).astype(BF16)


def _in_projection(x2, gain, w_main, w_ff):
    t = x2.shape[0]
    return pl.pallas_call(
        _inproj_kernel,
        grid=(t // IN_TM, IN_BLOCKS),
        in_specs=[
            pl.BlockSpec((IN_TM, D_MODEL), lambda i, j: (i, 0)),
            pl.BlockSpec((1, D_MODEL), lambda i, j: (0, 0)),
            pl.BlockSpec((D_MODEL, D_MODEL), lambda i, j: (0, j)),
            pl.BlockSpec((D_MODEL, LANES), lambda i, j: (0, 0)),
        ],
        out_specs=[
            pl.BlockSpec((IN_TM, D_MODEL), lambda i, j: (i, j)),
            pl.BlockSpec((IN_TM, LANES), lambda i, j: (i, 0)),
        ],
        out_shape=[
            jax.ShapeDtypeStruct((t, IN_BLOCKS * D_MODEL), BF16),
            jax.ShapeDtypeStruct((t, LANES), F32),
        ],
        scratch_shapes=[pltpu.VMEM((IN_TM, D_MODEL), BF16)],
        compiler_params=_params("parallel", "arbitrary"),
        name="in_projection",
    )(x2, gain, w_main, w_ff)


CUM_BLOCK = 512


def _cumsum_kernel(ff_ref, b_ref, out_ref, split_ref):
    s = ff_ref.shape[0]
    r = lax.broadcasted_iota(jnp.int32, (CUM_BLOCK, CUM_BLOCK), 0)
    c = lax.broadcasted_iota(jnp.int32, (CUM_BLOCK, CUM_BLOCK), 1)
    lower = (c <= r).astype(F32)
    carry = jnp.zeros((1, LANES), F32)
    for blk in range(s // CUM_BLOCK):
        rows = slice(blk * CUM_BLOCK, (blk + 1) * CUM_BLOCK)
        z = ff_ref[rows, :] + b_ref[...]
        log_f = jnp.minimum(z, 0.0) - jnp.log1p(jnp.exp(-jnp.abs(z)))
        cs = jnp.dot(lower, log_f, preferred_element_type=F32,
                     precision=lax.Precision.HIGHEST) + carry
        out_ref[rows, :] = cs
        for k, part in enumerate(_split3(cs)):
            split_ref[k, rows, :] = part
        carry = cs[CUM_BLOCK - 1:CUM_BLOCK, :]


def _forget_cumsum(ff, b_pad, batch, seq):
    return pl.pallas_call(
        _cumsum_kernel,
        grid=(batch,),
        in_specs=[
            pl.BlockSpec((seq, LANES), lambda b: (b, 0)),
            pl.BlockSpec((1, LANES), lambda b: (0, 0)),
        ],
        out_specs=[
            pl.BlockSpec((seq, LANES), lambda b: (b, 0)),
            pl.BlockSpec((AUG_SPLIT, seq, LANES), lambda b: (0, b, 0)),
        ],
        out_shape=[
            jax.ShapeDtypeStruct((batch * seq, LANES), F32),
            jax.ShapeDtypeStruct((AUG_SPLIT, batch * seq, LANES), BF16),
        ],
        compiler_params=_params("parallel"),
        name="forget_cumsum",
    )(ff, b_pad)


RET_ROWS = 1024
RET_GROUP = 4


def _retention_kernel(q_ref, k_ref, v_ref, g_ref, cos_ref, sin_ref, intra_ref, qd_ref, kd_ref,
                      cd_ref, out_ref, state_ref):
    hd = RET_HEAD_DIM
    half = hd // 2

    @pl.when(pl.program_id(2) == 0)
    def _():
        state_ref[...] = jnp.zeros_like(state_ref)

    def chunk(c, carry):
        r0 = pl.multiple_of(c * RET_CHUNK, RET_CHUNK)
        rows = pl.ds(r0, RET_CHUNK)
        cs = cos_ref[rows, :]
        sn = sin_ref[rows, :]

        def rot(x):
            x1 = x[:, :half]
            x2 = x[:, half:]
            return jnp.concatenate([x1 * cs - x2 * sn, x2 * cs + x1 * sn], axis=-1)

        for h in range(RET_GROUP):
            cols = slice(h * hd, (h + 1) * hd)
            qr = rot(q_ref[rows, cols].astype(F32))
            kr = rot(k_ref[rows, cols].astype(F32))
            v = v_ref[rows, cols]
            scores = lax.dot_general(qr.astype(BF16), kr.astype(BF16), NT_DIMS,
                                     preferred_element_type=F32) * intra_ref[h]
            st = state_ref[h]
            o = (jnp.dot(scores.astype(BF16), v, preferred_element_type=F32)
                 + jnp.dot((qr * qd_ref[h]).astype(BF16), st.astype(BF16),
                           preferred_element_type=F32))
            kd_t = (kr * kd_ref[h]).T.astype(BF16)
            state_ref[h] = st * cd_ref[h] + jnp.dot(kd_t, v, preferred_element_type=F32)
            on = o * lax.rsqrt(jnp.mean(o * o, axis=-1, keepdims=True) + EPS)
            g = g_ref[rows, cols].astype(F32)
            out_ref[rows, cols] = (on * (g * jax.nn.sigmoid(g))).astype(BF16)
        return carry

    lax.fori_loop(0, RET_ROWS // RET_CHUNK, chunk, 0)


def _retention(proj, cos, sin, intra, qd, kd, cd, batch, seq):
    t = batch * seq
    hd = RET_HEAD_DIM
    width = RET_GROUP * hd
    groups = RET_HEADS // RET_GROUP
    steps = seq // RET_ROWS
    col = lambda off: (lambda b, p, s: (b * steps + s, off + p))
    tab = lambda b, p, s: (p, 0, 0)
    angle = pl.BlockSpec((RET_ROWS, hd // 2), lambda b, p, s: (s, 0))
    return pl.pallas_call(
        _retention_kernel,
        grid=(batch, groups, steps),
        in_specs=[
            pl.BlockSpec((RET_ROWS, width), col(0)),
            pl.BlockSpec((RET_ROWS, width), col(groups)),
            pl.BlockSpec((RET_ROWS, width), col(2 * groups)),
            pl.BlockSpec((RET_ROWS, width), col(3 * groups)),
            angle, angle,
            pl.BlockSpec((RET_GROUP, RET_CHUNK, RET_CHUNK), tab),
            pl.BlockSpec((RET_GROUP, RET_CHUNK, hd), tab),
            pl.BlockSpec((RET_GROUP, RET_CHUNK, hd), tab),
            pl.BlockSpec((RET_GROUP, 1, hd), tab),
        ],
        out_specs=pl.BlockSpec((RET_ROWS, width), lambda b, p, s: (b * steps + s, p)),
        out_shape=jax.ShapeDtypeStruct((t, RET_HEADS * hd), BF16),
        scratch_shapes=[pltpu.VMEM((RET_GROUP, hd, hd), F32)],
        compiler_params=_params("parallel", "parallel", "arbitrary"),
        name="retention",
    )(proj, proj, proj, proj, cos, sin, intra, qd, kd, cd)


def _retention_tables(seq):
    h = jnp.arange(RET_HEADS, dtype=F32)
    log_gamma = jnp.log1p(-(2.0 ** (-5.0 - h)))
    idx = jnp.arange(RET_CHUNK, dtype=F32)
    diff = idx[:, None] - idx[None, :]
    scale = RET_HEAD_DIM ** -0.5
    intra = jnp.where(diff >= 0, jnp.exp(log_gamma[:, None, None] * jnp.maximum(diff, 0.0)), 0.0) * scale
    q_decay = jnp.exp(log_gamma[:, None] * (idx + 1.0))
    k_decay = jnp.exp(log_gamma[:, None] * (RET_CHUNK - 1.0 - idx)) * scale
    chunk_decay = jnp.exp(log_gamma * RET_CHUNK)
    qd = jnp.broadcast_to(q_decay[:, :, None], (RET_HEADS, RET_CHUNK, RET_HEAD_DIM))
    kd = jnp.broadcast_to(k_decay[:, :, None], (RET_HEADS, RET_CHUNK, RET_HEAD_DIM))
    cd = jnp.broadcast_to(chunk_decay[:, None, None], (RET_HEADS, 1, RET_HEAD_DIM))
    inv_freq = ROPE_BASE ** (-jnp.arange(0, RET_HEAD_DIM, 2, dtype=F32) / RET_HEAD_DIM)
    ang = jnp.arange(seq, dtype=F32)[:, None] * inv_freq[None, :]
    return jnp.cos(ang), jnp.sin(ang), intra, qd, kd, cd


FOX_TQ = 512
FOX_TK = 512
assert FOX_TQ == FOX_TK
FOX_QSTEP = 4


def _pair_rms(x, lo_mask, gain_row):
    x2 = x * x
    s_all = jnp.sum(x2, axis=-1, keepdims=True)
    s_lo = jnp.sum(jnp.where(lo_mask, x2, 0.0), axis=-1, keepdims=True)
    ms = jnp.where(lo_mask, s_lo, s_all - s_lo) * (1.0 / FOX_HEAD_DIM)
    return x * lax.rsqrt(ms + EPS) * gain_row


AUG_CQ = 0
AUG_CK = 3
AUG_M = 6
AUG_SPLIT = 3
FOX_EXACT_ABOVE = 30.0
FOX_UNDERFLOW_LOG = -88.0


def _split3(x):
    hi = x.astype(BF16)
    r1 = x - hi.astype(F32)
    mid = r1.astype(BF16)
    lo = (r1 - mid.astype(F32)).astype(BF16)
    return hi, mid, lo


def _lanes_in(lane, start, count=AUG_SPLIT):
    return (lane >= start) & (lane < start + count)


def _fox_kernel(bound_ref, first_ref, q_ref, k_ref, v_ref, c_ref, qg_ref, kg_ref, out_ref,
                kaug_ref, caq_ref, vat_ref, qb_ref, qat_ref, acc_ref, m_ref):
    seq = k_ref.shape[0]
    hp = pl.program_id(1)
    step = pl.program_id(2)
    lane = lax.broadcasted_iota(jnp.int32, (1, LANES), 1)
    lo = lane < FOX_HEAD_DIM
    head_lanes = (lo, jnp.logical_not(lo))
    spare = (FOX_HEAD_DIM, 0)
    bound = bound_ref[0]

    @pl.when(step == 0)
    def _prepare_keys_values():
        r = lax.broadcasted_iota(jnp.int32, (LANES, LANES), 0)
        c = lax.broadcasted_iota(jnp.int32, (LANES, LANES), 1)
        place = []
        for k in range(AUG_SPLIT):
            pk = jnp.zeros((LANES, LANES), F32)
            for h in range(2):
                src = r == 2 * hp + h
                pk = pk + jnp.where(src & (c == spare[h] + AUG_CQ + k), 1.0, 0.0)
                pk = pk - jnp.where(src & (c == spare[h] + AUG_CK + k), 1.0, 0.0)
            place.append(pk.astype(BF16))
        bound_parts = [p.astype(F32) for p in _split3(jnp.full((1, LANES), -bound, F32))]

        def prep(i, carry):
            rows = pl.ds(pl.multiple_of(i * FOX_TK, FOX_TK), FOX_TK)
            e = sum(jnp.dot(c_ref[k, rows, :], place[k], preferred_element_type=F32)
                    for k in range(AUG_SPLIT))
            kn = _pair_rms(k_ref[rows, :].astype(F32), lo, kg_ref[...])
            v = v_ref[rows, :]
            one = jnp.ones_like(v)
            for h in range(2):
                ones_k = jnp.where(_lanes_in(lane, spare[h] + AUG_CQ)
                                   | _lanes_in(lane, spare[h] + AUG_M), 1.0, 0.0)
                kaug_ref[h, rows, :] = jnp.where(
                    head_lanes[h], kn,
                    jnp.where(_lanes_in(lane, spare[h] + AUG_CK), e, ones_k)).astype(BF16)
                const_q = jnp.where(_lanes_in(lane, spare[h] + AUG_CK), 1.0, 0.0)
                for k in range(AUG_SPLIT):
                    const_q = jnp.where(lane == spare[h] + AUG_M + k, bound_parts[k], const_q)
                caq_ref[h, rows, :] = jnp.where(
                    _lanes_in(lane, spare[h] + AUG_CQ), e, const_q).astype(BF16)
                va = jnp.where(head_lanes[h], v, one)
                vat_ref[h, i] = va.astype(F32).T.astype(BF16)
            return carry
        lax.fori_loop(0, seq // FOX_TK, prep, 0)

    def query_block(sub):
        qi = step * FOX_QSTEP + sub
        first = first_ref[(pl.program_id(0) * pl.num_programs(1) + hp)
                          * (pl.num_programs(2) * FOX_QSTEP) + qi]
        local_rows = pl.ds(sub * FOX_TQ, FOX_TQ)
        t0 = pl.multiple_of(qi * FOX_TQ, FOX_TQ)
        q_rows = pl.ds(t0, FOX_TQ)
        qn = _pair_rms(q_ref[local_rows, :].astype(F32), lo, qg_ref[...]) * (FOX_HEAD_DIM ** -0.5)
        kq = lax.broadcasted_iota(jnp.int32, (FOX_TK, FOX_TQ), 0)
        qq = lax.broadcasted_iota(jnp.int32, (FOX_TK, FOX_TQ), 1)

        q_aug = [jnp.where(head_lanes[h], qn, caq_ref[h, q_rows, :].astype(F32)) for h in range(2)]
        for h in range(2):
            qat_ref[h] = q_aug[h].T.astype(BF16)

        @pl.when(bound > FOX_EXACT_ABOVE)
        def _exact_row_max():
            for h in range(2):
                qb_ref[h] = jnp.where(_lanes_in(lane, spare[h] + AUG_M), 0.0, q_aug[h]).astype(BF16)
                m_ref[h] = jnp.full((FOX_TQ, 1), NEG_BIG, F32)

            def scan(j, masked):
                keys = pl.ds(pl.multiple_of(j * FOX_TK, FOX_TK), FOX_TK)
                for h in range(2):
                    s = lax.dot_general(qb_ref[h], kaug_ref[h, keys, :], NT_DIMS,
                                        preferred_element_type=F32)
                    if masked:
                        s = jnp.where(qq <= kq, s, NEG_BIG)
                    m_ref[h] = jnp.maximum(m_ref[h], jnp.max(s, axis=-1, keepdims=True))

            def scan_body(j, carry):
                scan(j, False)
                return carry
            lax.fori_loop(first, qi, scan_body, 0)
            scan(qi, True)
            for h in range(2):
                m_parts = _split3(-m_ref[h])
                qa = q_aug[h]
                for k in range(AUG_SPLIT):
                    qa = jnp.where(lane == spare[h] + AUG_M + k, m_parts[k].astype(F32), qa)
                qat_ref[h] = qa.T.astype(BF16)

        acc_ref[...] = jnp.zeros_like(acc_ref)

        def probs(h, keys, q_cols=slice(None)):
            st = jnp.dot(kaug_ref[h, keys, :], qat_ref[h, :, q_cols], preferred_element_type=F32)
            return jnp.exp(st)

        def kv_blocks(blocks):
            for h in range(2):
                total = None
                for j in blocks:
                    keys = pl.ds(pl.multiple_of(j * FOX_TK, FOX_TK), FOX_TK)
                    o = jnp.dot(vat_ref[h, j], probs(h, keys).astype(BF16),
                                preferred_element_type=F32)
                    total = o if total is None else total + o
                acc_ref[h] += total

        def pair(i, carry):
            kv_blocks((first + 2 * i, first + 2 * i + 1))
            return carry

        full_blocks = qi - first
        lax.fori_loop(0, lax.shift_right_logical(full_blocks, 1), pair, 0)

        @pl.when((full_blocks & 1) == 1)
        def _odd_block():
            kv_blocks((qi - 1,))

        half = FOX_TK // 2
        lower, upper = slice(0, half), slice(half, FOX_TK)
        causal_a = (lax.broadcasted_iota(jnp.int32, (half, FOX_TQ), 0)
                    <= lax.broadcasted_iota(jnp.int32, (half, FOX_TQ), 1))
        causal_b = (lax.broadcasted_iota(jnp.int32, (half, half), 0)
                    <= lax.broadcasted_iota(jnp.int32, (half, half), 1))
        for h in range(2):
            p_a = jnp.where(causal_a, probs(h, pl.ds(t0, half)), 0.0)
            p_b = jnp.where(causal_b, probs(h, pl.ds(t0 + half, half), upper), 0.0)
            acc_ref[h] += jnp.dot(vat_ref[h, qi, :, lower], p_a.astype(BF16),
                                  preferred_element_type=F32)
            acc_ref[h, :, upper] += jnp.dot(vat_ref[h, qi, :, upper], p_b.astype(BF16),
                                            preferred_element_type=F32)

        o0 = acc_ref[0]
        o1 = acc_ref[1]
        l0 = o0[FOX_HEAD_DIM:FOX_HEAD_DIM + 1, :]
        l1 = o1[0:1, :]
        row = lax.broadcasted_iota(jnp.int32, (LANES, 1), 0)
        out_ref[local_rows, :] = jnp.where(row < FOX_HEAD_DIM, o0 / l0, o1 / l1).T.astype(BF16)

    for sub in range(FOX_QSTEP):
        query_block(sub)


def _fox_first_blocks(cum, bound, batch, seq):
    blocks = seq // FOX_TK
    c = cum.reshape(batch, seq, LANES)[:, :, :FOX_HEADS]
    c_start = c[:, 0::FOX_TQ, :]
    c_end = c[:, FOX_TK - 1::FOX_TK, :]
    negligible = (c_start[:, :, None, :] - c_end[:, None, :, :] + 2.0 * bound) < FOX_UNDERFLOW_LOG
    idx = jnp.arange(blocks)
    earlier = (idx[None, :] < idx[:, None])[None, :, :, None]
    count = jnp.sum(negligible & earlier, axis=2)
    first = jnp.min(count.reshape(batch, blocks, FOX_HEADS // 2, 2), axis=-1)
    return jnp.transpose(first, (0, 2, 1)).reshape(-1).astype(jnp.int32)


def _fox_attention(proj, cum_split, bound, first, qg, kg, batch, seq):
    t = batch * seq
    nq = seq // (FOX_QSTEP * FOX_TQ)
    pairs = FOX_HEADS // 2
    base = 4 * (D_MODEL // LANES)
    return pl.pallas_call(
        _fox_kernel,
        grid=(batch, pairs, nq),
        in_specs=[
            pl.BlockSpec(memory_space=pltpu.SMEM),
            pl.BlockSpec(memory_space=pltpu.SMEM),
            pl.BlockSpec((FOX_QSTEP * FOX_TQ, LANES), lambda b, p, i: (b * nq + i, base + p)),
            pl.BlockSpec((seq, LANES), lambda b, p, i: (b, base + pairs + p)),
            pl.BlockSpec((seq, LANES), lambda b, p, i: (b, base + 2 * pairs + p)),
            pl.BlockSpec((AUG_SPLIT, seq, LANES), lambda b, p, i: (0, b, 0)),
            pl.BlockSpec((1, LANES), lambda b, p, i: (0, 0)),
            pl.BlockSpec((1, LANES), lambda b, p, i: (0, 0)),
        ],
        out_specs=pl.BlockSpec((FOX_QSTEP * FOX_TQ, LANES), lambda b, p, i: (b * nq + i, p)),
        out_shape=jax.ShapeDtypeStruct((t, FOX_HEADS * FOX_HEAD_DIM), BF16),
        scratch_shapes=[
            pltpu.VMEM((2, seq, LANES), BF16),
            pltpu.VMEM((2, seq, LANES), BF16),
            pltpu.VMEM((2, seq // FOX_TK, LANES, FOX_TK), BF16),
            pltpu.VMEM((2, FOX_TQ, LANES), BF16),
            pltpu.VMEM((2, LANES, FOX_TQ), BF16),
            pltpu.VMEM((2, LANES, FOX_TQ), F32),
            pltpu.VMEM((2, FOX_TQ, 1), F32),
        ],
        compiler_params=_params("parallel", "parallel", "arbitrary"),
        name="fox_attention",
    )(bound, first, proj, proj, proj, cum_split, qg, kg)


MEMKV_TM = 512
MEMATT_TQ = 1024


def _mem_kv_kernel(mem_ref, g_ref, w_ref, kg_ref, k_out, v_out):
    hb = _rms_rows(mem_ref[...], g_ref[...]).astype(BF16)
    kv = jnp.dot(hb, w_ref[...], preferred_element_type=F32)
    width = MEM_HEADS * MEM_HEAD_DIM
    for h in range(MEM_HEADS):
        cols = slice(h * MEM_HEAD_DIM, (h + 1) * MEM_HEAD_DIM)
        k_out[:, cols] = _rms_rows(kv[:, cols], kg_ref[...]).astype(BF16)
    v_out[...] = kv[:, width:].astype(BF16)


def _mem_kv(mem2, gain, w_kv, k_gain):
    rows = mem2.shape[0]
    width = MEM_HEADS * MEM_HEAD_DIM
    return pl.pallas_call(
        _mem_kv_kernel,
        grid=(rows // MEMKV_TM,),
        in_specs=[
            pl.BlockSpec((MEMKV_TM, D_MODEL), lambda i: (i, 0)),
            pl.BlockSpec((1, D_MODEL), lambda i: (0, 0)),
            pl.BlockSpec((D_MODEL, 2 * width), lambda i: (0, 0)),
            pl.BlockSpec((1, MEM_HEAD_DIM), lambda i: (0, 0)),
        ],
        out_specs=[pl.BlockSpec((MEMKV_TM, width), lambda i: (i, 0))] * 2,
        out_shape=[jax.ShapeDtypeStruct((rows, width), BF16)] * 2,
        compiler_params=_params("parallel"),
        name="mem_kv",
    )(mem2, gain, w_kv, k_gain)


def _mem_attn_kernel(q_ref, k_ref, v_ref, qg_ref, out_ref):
    for h in range(MEM_HEADS):
        cols = slice(h * MEM_HEAD_DIM, (h + 1) * MEM_HEAD_DIM)
        qn = _rms_rows(q_ref[:, cols].astype(F32), qg_ref[...]) * (MEM_HEAD_DIM ** -0.5)
        s = lax.dot_general(qn.astype(BF16), k_ref[:, cols], NT_DIMS, preferred_element_type=F32)
        p = jnp.exp(s - jnp.max(s, axis=-1, keepdims=True))
        denom = jnp.sum(p, axis=-1, keepdims=True)
        o = jnp.dot(p.astype(BF16), v_ref[:, cols], preferred_element_type=F32)
        out_ref[:, cols] = (o / denom).astype(BF16)


def _mem_attention(proj, mk, mv, q_gain, batch, seq, mem_len):
    t = batch * seq
    nq = seq // MEMATT_TQ
    width = MEM_HEADS * MEM_HEAD_DIM
    return pl.pallas_call(
        _mem_attn_kernel,
        grid=(batch, nq),
        in_specs=[
            pl.BlockSpec((MEMATT_TQ, width), lambda b, i: (b * nq + i, 7)),
            pl.BlockSpec((mem_len, width), lambda b, i: (b, 0)),
            pl.BlockSpec((mem_len, width), lambda b, i: (b, 0)),
            pl.BlockSpec((1, MEM_HEAD_DIM), lambda b, i: (0, 0)),
        ],
        out_specs=pl.BlockSpec((MEMATT_TQ, width), lambda b, i: (b * nq + i, 0)),
        out_shape=jax.ShapeDtypeStruct((t, width), BF16),
        compiler_params=_params("parallel", "parallel"),
        name="mem_attention",
    )(proj, mk, mv, q_gain)


MIX_TM = 512
ROUTER_ROWS = 32
PAIRS = ((0, 1), (0, 2), (0, 3), (1, 2), (1, 3), (2, 3))
PAIRS_PER_GROUP = len(PAIRS)
N_CLASSES = N_GROUPS * PAIRS_PER_GROUP
ROW_WIDTH = D_MODEL + LANES


def _top2_sum(b0, b1, b2, b3):
    p, q = jnp.maximum(b0, b1), jnp.minimum(b0, b1)
    r, s = jnp.maximum(b2, b3), jnp.minimum(b2, b3)
    return jnp.maximum(p, r) + jnp.maximum(jnp.minimum(p, r), jnp.maximum(q, s))


def _mix_kernel(ro_ref, fo_ref, mo_ref, gr_ref, gf_ref, gm_ref, x_ref,
                wr_ref, wf_ref, wm_ref, wo_ref, nf_ref, rw_ref, rb_ref,
                xa_ref, cls_ref):
    def branch(a_ref, w_ref, g_ref):
        y = jnp.dot(a_ref[...], w_ref[...], preferred_element_type=F32)
        return jax.nn.sigmoid(g_ref[...].astype(F32)) * y

    merged = (branch(ro_ref, wr_ref, gr_ref) + branch(fo_ref, wf_ref, gf_ref)
              + branch(mo_ref, wm_ref, gm_ref))
    xn = x_ref[...] + jnp.dot(merged.astype(BF16), wo_ref[...], preferred_element_type=F32)
    xa_ref[:, :D_MODEL] = xn
    h = _rms_rows(xn, nf_ref[...])
    h_hi = h.astype(BF16)

    h_lo = (h - h_hi.astype(F32)).astype(BF16)
    rw = rw_ref[...]
    rw_hi = rw.astype(BF16)
    rw_lo = (rw - rw_hi.astype(F32)).astype(BF16)
    dot_nt = lambda a, b: lax.dot_general(a, b, NT_DIMS, preferred_element_type=F32)
    logits = dot_nt(rw_hi, h_hi) + dot_nt(rw_hi, h_lo) + dot_nt(rw_lo, h_hi)
    scores = jax.nn.sigmoid(logits)
    biased = scores + rb_ref[...]
    n = EXPERTS_PER_GROUP
    sc = [scores[SUBLANES * j:SUBLANES * (j + 1)] for j in range(n)]
    bi = [biased[SUBLANES * j:SUBLANES * (j + 1)] for j in range(n)]
    group_score = _top2_sum(*bi)
    rows = lax.broadcasted_iota(jnp.int32, group_score.shape, 0)
    best = jnp.max(group_score, axis=0, keepdims=True)
    top_group = jnp.min(jnp.where(group_score == best, rows, SUBLANES), axis=0, keepdims=True)
    in_group = rows == top_group
    picked, chosen = [], []
    for j in range(n):
        rank = jnp.zeros(group_score.shape, jnp.int32)
        for i in range(n):
            if i == j:
                continue
            ahead = (bi[i] > bi[j]) | ((bi[i] == bi[j]) & (i < j))
            rank = rank + ahead.astype(jnp.int32)
        sel = in_group & (rank < 2)
        picked.append(jnp.where(sel, sc[j], 0.0))
        chosen.append(jnp.max(jnp.where(sel, 1.0, 0.0), axis=0, keepdims=True) > 0.5)
    denom = jnp.sum(picked[0] + picked[1] + picked[2] + picked[3], axis=0, keepdims=True)
    gate = [jnp.sum(picked[j], axis=0, keepdims=True) / denom for j in range(n)]

    first = jnp.full(top_group.shape, n, jnp.int32)
    second = jnp.full(top_group.shape, -1, jnp.int32)
    for j in range(n):
        first = jnp.minimum(first, jnp.where(chosen[j], j, n))
        second = jnp.maximum(second, jnp.where(chosen[j], j, -1))
    pair_base = jnp.where(first == 0, 0, jnp.where(first == 1, 3, 5))
    cls = top_group * PAIRS_PER_GROUP + pair_base + second - first - 1
    cls_ref[...] = jnp.clip(cls, 0, N_CLASSES - 1)
    g_first = sum(jnp.where(first == j, gate[j], 0.0) for j in range(n))
    g_second = sum(jnp.where(second == j, gate[j], 0.0) for j in range(n))
    srow = lax.broadcasted_iota(jnp.int32, (SUBLANES, 1), 0)
    gates8 = jnp.where(srow == 0, g_first, jnp.where(srow == 1, g_second, 0.0))
    pad = jnp.zeros((LANES - SUBLANES, gates8.shape[1]), F32)
    xa_ref[:, D_MODEL:] = jnp.concatenate([gates8, pad], axis=0).T


def _mix_and_route(ro, fo, mo, proj, x2, w_r, w_f, w_m, w_o, norm_ffn, rw_pad, rb_pad):
    t = x2.shape[0]
    tok = lambda i: (i, 0)
    const = lambda i: (0, 0)
    act = pl.BlockSpec((MIX_TM, D_MODEL), tok)
    wspec = pl.BlockSpec((D_MODEL, D_MODEL), const)
    return pl.pallas_call(
        _mix_kernel,
        grid=(t // MIX_TM,),
        in_specs=[
            act, act, act,
            pl.BlockSpec((MIX_TM, D_MODEL), lambda i: (i, 8)),
            pl.BlockSpec((MIX_TM, D_MODEL), lambda i: (i, 9)),
            pl.BlockSpec((MIX_TM, D_MODEL), lambda i: (i, 10)),
            act,
            wspec, wspec, wspec, wspec,
            pl.BlockSpec((1, D_MODEL), const),
            pl.BlockSpec((ROUTER_ROWS, D_MODEL), const),
            pl.BlockSpec((ROUTER_ROWS, 1), const),
        ],
        out_specs=[
            pl.BlockSpec((MIX_TM, ROW_WIDTH), tok),
            pl.BlockSpec((1, MIX_TM), lambda i: (0, i)),
        ],
        out_shape=[
            jax.ShapeDtypeStruct((t, ROW_WIDTH), F32),
            jax.ShapeDtypeStruct((1, t), jnp.int32),
        ],
        compiler_params=_params("parallel"),
        name="mix_and_route",
    )(ro, fo, mo, proj, proj, proj, x2, w_r, w_f, w_m, w_o, norm_ffn, rw_pad, rb_pad)


EXP_TR = 256
POS_TM = 2048
POS_SUB = 512
MOVE_TM = 1024
CLASS_ROWS = 32


def _position_kernel(cls_ref, pos_ref, offs_ref, size_ref, counts_ref, running_ref):
    phase = pl.program_id(0)
    i = pl.program_id(1)

    def class_onehot(cls):
        rows = lax.broadcasted_iota(jnp.int32, (CLASS_ROWS, cls.shape[1]), 0)
        return jnp.where(rows == cls, 1.0, 0.0)

    @pl.when(phase == 0)
    def _count():
        @pl.when(i == 0)
        def _():
            counts_ref[...] = jnp.zeros_like(counts_ref)
        counts_ref[...] += jnp.sum(class_onehot(cls_ref[...]), axis=1, keepdims=True)
        pos_ref[...] = jnp.zeros_like(pos_ref)

    @pl.when(phase == 1)
    def _place():
        @pl.when(i == 0)
        def _():
            padded = jnp.floor((counts_ref[...] + (EXP_TR - 1)) * (1.0 / EXP_TR)) * EXP_TR
            r = lax.broadcasted_iota(jnp.int32, (CLASS_ROWS, CLASS_ROWS), 0)
            c = lax.broadcasted_iota(jnp.int32, (CLASS_ROWS, CLASS_ROWS), 1)
            below = jnp.where(c < r, 1.0, 0.0)
            offs = jnp.dot(below, padded, preferred_element_type=F32,
                           precision=lax.Precision.HIGHEST)
            offs_ref[...] = offs
            size_ref[...] = padded
            running_ref[...] = offs

        r = lax.broadcasted_iota(jnp.int32, (POS_SUB, POS_SUB), 0)
        c = lax.broadcasted_iota(jnp.int32, (POS_SUB, POS_SUB), 1)
        earlier = jnp.where(r < c, 1.0, 0.0).astype(BF16)
        running = running_ref[:, 0:1]
        for u in range(POS_TM // POS_SUB):
            cols = slice(u * POS_SUB, (u + 1) * POS_SUB)
            onehot = class_onehot(cls_ref[:, cols])
            prefix = jnp.dot(onehot.astype(BF16), earlier, preferred_element_type=F32)
            pos_ref[:, cols] = jnp.sum(onehot * (prefix + running), axis=0,
                                       keepdims=True).astype(jnp.int32)
            running = running + jnp.sum(onehot, axis=1, keepdims=True)
        running_ref[...] = jnp.broadcast_to(running, running_ref.shape)


def _positions(cls):
    t = cls.shape[1]
    meta = jax.ShapeDtypeStruct((CLASS_ROWS, LANES), F32)
    return pl.pallas_call(
        _position_kernel,
        grid=(2, t // POS_TM),
        in_specs=[pl.BlockSpec((1, POS_TM), lambda p, i: (0, i))],
        out_specs=[
            pl.BlockSpec((1, POS_TM), lambda p, i: (0, i * p)),
            pl.BlockSpec((CLASS_ROWS, LANES), lambda p, i: (0, 0)),
            pl.BlockSpec((CLASS_ROWS, LANES), lambda p, i: (0, 0)),
        ],
        out_shape=[jax.ShapeDtypeStruct((1, t), jnp.int32), meta, meta],
        scratch_shapes=[pltpu.VMEM((CLASS_ROWS, LANES), F32), pltpu.VMEM((CLASS_ROWS, LANES), F32)],
        compiler_params=_params("arbitrary", "arbitrary"),
        name="moe_positions",
    )(cls)


def _move_rows(copy_for_row):
    def start(group, carry):
        for sub in range(SUBLANES):
            copy_for_row(group, sub).start(priority=sub % 2)
        return carry

    def wait(group, carry):
        for sub in range(SUBLANES):
            copy_for_row(group, sub).wait()
        return carry

    lax.fori_loop(0, MOVE_TM // SUBLANES, start, 0)
    lax.fori_loop(0, MOVE_TM // SUBLANES, wait, 0)


def _scatter_rows_kernel(pos_ref, src_ref, init_ref, dst_ref, sem):
    del init_ref
    _move_rows(lambda group, sub: pltpu.make_async_copy(
        src_ref.at[group, pl.ds(sub, 1)],
        dst_ref.at[pl.ds(pos_ref[group * SUBLANES + sub], 1)], sem))


def _scatter_rows(pos, src, init):
    t, width = src.shape
    return pl.pallas_call(
        _scatter_rows_kernel,
        grid=(t // MOVE_TM,),
        in_specs=[
            pl.BlockSpec((MOVE_TM,), lambda i: (i,), memory_space=pltpu.SMEM),
            pl.BlockSpec((MOVE_TM // SUBLANES, SUBLANES, width), lambda i: (i, 0, 0)),
            pl.BlockSpec(memory_space=pl.ANY),
        ],
        out_specs=pl.BlockSpec(memory_space=pl.ANY),
        out_shape=jax.ShapeDtypeStruct(init.shape, init.dtype),
        scratch_shapes=[pltpu.SemaphoreType.DMA(())],
        input_output_aliases={2: 0},
        compiler_params=_params("arbitrary"),
        name="moe_scatter_rows",
    )(pos, src.reshape(t // SUBLANES, SUBLANES, width), init)


def _gather_rows_kernel(pos_ref, src_ref, dst_ref, sem):
    _move_rows(lambda group, sub: pltpu.make_async_copy(
        src_ref.at[pl.ds(pos_ref[group * SUBLANES + sub], 1)],
        dst_ref.at[group, pl.ds(sub, 1)], sem))


def _gather_rows(pos, src):
    t = pos.shape[0]
    width = src.shape[1]
    out = pl.pallas_call(
        _gather_rows_kernel,
        grid=(t // MOVE_TM,),
        in_specs=[
            pl.BlockSpec((MOVE_TM,), lambda i: (i,), memory_space=pltpu.SMEM),
            pl.BlockSpec(memory_space=pl.ANY),
        ],
        out_specs=pl.BlockSpec((MOVE_TM // SUBLANES, SUBLANES, width), lambda i: (i, 0, 0)),
        out_shape=jax.ShapeDtypeStruct((t // SUBLANES, SUBLANES, width), src.dtype),
        scratch_shapes=[pltpu.SemaphoreType.DMA(())],
        compiler_params=_params("arbitrary"),
        name="moe_gather_rows",
    )(pos, src)
    return out.reshape(t, width)


def _experts_kernel(ea_ref, eb_ref, used_ref, xs_ref, nf_ref,
                    wga_ref, wua_ref, wda_ref, wgb_ref, wub_ref, wdb_ref, ys_ref):
    del ea_ref, eb_ref
    k = pl.program_id(0)

    @pl.when(k < used_ref[0])
    def _():
        x = xs_ref[:, :D_MODEL]
        gates = xs_ref[:, D_MODEL:]
        h = _rms_rows(x, nf_ref[...]).astype(BF16)

        def ffn(wg_ref, wu_ref, wd_ref):
            a = jnp.dot(h, wg_ref[...], preferred_element_type=F32)
            u = jnp.dot(h, wu_ref[...], preferred_element_type=F32)
            return jnp.dot((a * jax.nn.sigmoid(a) * u).astype(BF16), wd_ref[...],
                           preferred_element_type=F32)

        moe = (gates[:, 0:1] * ffn(wga_ref, wua_ref, wda_ref)
               + gates[:, 1:2] * ffn(wgb_ref, wub_ref, wdb_ref))
        ys_ref[...] = x + moe

    @pl.when(k >= used_ref[0])
    def _():
        ys_ref[...] = jnp.zeros_like(ys_ref)


def _experts(xs, norm_ffn, w_gate, w_up, w_down, expert_a, expert_b, n_used):
    n_tiles = xs.shape[0] // EXP_TR
    up = lambda sel: pl.BlockSpec((None, D_MODEL, D_FF), lambda k, ea, eb, nu: (sel(ea, eb)[k], 0, 0))
    down = lambda sel: pl.BlockSpec((None, D_FF, D_MODEL), lambda k, ea, eb, nu: (sel(ea, eb)[k], 0, 0))
    first = lambda ea, eb: ea
    second = lambda ea, eb: eb
    grid_spec = pltpu.PrefetchScalarGridSpec(
        num_scalar_prefetch=3,
        grid=(n_tiles,),
        in_specs=[
            pl.BlockSpec((EXP_TR, ROW_WIDTH), lambda k, ea, eb, nu: (k, 0)),
            pl.BlockSpec((1, D_MODEL), lambda k, ea, eb, nu: (0, 0)),
            up(first), up(first), down(first), up(second), up(second), down(second),
        ],
        out_specs=pl.BlockSpec((EXP_TR, D_MODEL), lambda k, ea, eb, nu: (k, 0)),
    )
    return pl.pallas_call(
        _experts_kernel,
        grid_spec=grid_spec,
        out_shape=jax.ShapeDtypeStruct((xs.shape[0], D_MODEL), F32),
        compiler_params=_params("arbitrary"),
        name="experts",
    )(expert_a, expert_b, n_used, xs, norm_ffn, w_gate, w_up, w_down, w_gate, w_up, w_down)


def _tile_experts(offs, sizes, n_tiles):
    ends = (offs[:N_CLASSES, 0] + sizes[:N_CLASSES, 0]).astype(jnp.int32)
    starts = jnp.arange(n_tiles, dtype=jnp.int32) * EXP_TR
    tile_cls = jnp.minimum(jnp.sum(ends[None, :] <= starts[:, None], axis=1), N_CLASSES - 1)
    group = tile_cls // PAIRS_PER_GROUP
    pair = tile_cls % PAIRS_PER_GROUP
    slot_a = jnp.array([a for a, _ in PAIRS], jnp.int32)[pair]
    slot_b = jnp.array([b for _, b in PAIRS], jnp.int32)[pair]
    expert_a = (group * EXPERTS_PER_GROUP + slot_a).astype(jnp.int32)
    expert_b = (group * EXPERTS_PER_GROUP + slot_b).astype(jnp.int32)
    return expert_a, expert_b, (ends[-1] // EXP_TR).reshape(1)


def _router_layout(router_w, router_bias):
    w = router_w.T.reshape(N_GROUPS, EXPERTS_PER_GROUP, D_MODEL)
    w = jnp.transpose(w, (1, 0, 2))
    w = jnp.pad(w, ((0, 0), (0, SUBLANES - N_GROUPS), (0, 0))).reshape(ROUTER_ROWS, D_MODEL)
    b = jnp.transpose(router_bias.reshape(N_GROUPS, EXPERTS_PER_GROUP), (1, 0))
    b = jnp.pad(b, ((0, 0), (0, SUBLANES - N_GROUPS)), constant_values=NEG_BIG)
    return w.astype(F32), b.reshape(ROUTER_ROWS, 1).astype(F32)


def kernel(x, mem, norm_mix, norm_mem, w_in, b_forget, fox_q_norm, fox_k_norm, mem_q_norm,
           mem_k_norm, w_mem_kv, w_o_ret, w_o_fox, w_o_mem, w_out, norm_ffn, router_w,
           router_bias, w_gate, w_up, w_down):
    batch, seq, d = x.shape
    mem_len = mem.shape[1]
    depth = w_in.shape[0]
    t = batch * seq
    ff_lo = 7 * D_MODEL
    ff_hi = ff_lo + FOX_HEADS

    cos, sin, intra, qd, kd, cd = _retention_tables(seq)
    rw_pad, rb_pad = _router_layout(router_w, router_bias)
    row = lambda v: v.reshape(1, -1).astype(F32)

    x2 = x.reshape(t, d)
    mem2 = mem.reshape(batch * mem_len, d)
    n_tiles = t // EXP_TR + N_CLASSES
    sorted_rows = jnp.zeros((n_tiles * EXP_TR, ROW_WIDTH), F32)
    for l in range(depth):
        w_main = jnp.concatenate([w_in[l, :, :ff_lo], w_in[l, :, ff_hi:]], axis=1).astype(BF16)
        w_ff = jnp.pad(w_in[l, :, ff_lo:ff_hi], ((0, 0), (0, LANES - FOX_HEADS))).astype(BF16)
        b_pad = jnp.pad(b_forget[l], (0, LANES - FOX_HEADS)).reshape(1, LANES)

        proj, ff = _in_projection(x2, row(norm_mix[l]), w_main, w_ff)
        cum, cum_split = _forget_cumsum(ff, b_pad, batch, seq)
        ro = _retention(proj, cos, sin, intra, qd, kd, cd, batch, seq)
        bound = (8.16 * jnp.max(jnp.abs(fox_q_norm[l])) * jnp.max(jnp.abs(fox_k_norm[l])))
        first = _fox_first_blocks(cum, bound, batch, seq)
        fo = _fox_attention(proj, cum_split, bound.reshape(1).astype(F32), first,
                            row(jnp.tile(fox_q_norm[l], 2)), row(jnp.tile(fox_k_norm[l], 2)),
                            batch, seq)
        mk, mv = _mem_kv(mem2, row(norm_mem[l]), w_mem_kv[l].astype(BF16), row(mem_k_norm[l]))
        mo = _mem_attention(proj, mk, mv, row(mem_q_norm[l]), batch, seq, mem_len)
        xa, cls = _mix_and_route(
            ro, fo, mo, proj, x2, w_o_ret[l].astype(BF16), w_o_fox[l].astype(BF16),
            w_o_mem[l].astype(BF16), w_out[l].astype(BF16), row(norm_ffn[l]), rw_pad, rb_pad)
        pos, offs, sizes = _positions(cls)
        pos = pos.reshape(t)
        expert_a, expert_b, n_used = _tile_experts(offs, sizes, n_tiles)
        sorted_rows = _scatter_rows(pos, xa, sorted_rows)
        ys = _experts(sorted_rows, row(norm_ffn[l]), w_gate[l].astype(BF16), w_up[l].astype(BF16),
                      w_down[l].astype(BF16), expert_a, expert_b, n_used)
        x2 = _gather_rows(pos, ys)
    return x2.reshape(batch, seq, d)
```

```python
import functools

import jax
import jax.numpy as jnp
from jax import lax
from jax.experimental import pallas as pl
from jax.experimental.pallas import tpu as pltpu

F32 = jnp.float32
BF16 = jnp.bfloat16

D_MODEL = 1024
EPS = 1e-6
RET_HEADS = 4
RET_HEAD_DIM = 256
RET_CHUNK = 128
ROPE_BASE = 10000.0
FOX_HEADS = 16
FOX_HEAD_DIM = 64
MEM_HEADS = 4
MEM_HEAD_DIM = 256
N_EXPERTS = 16
N_GROUPS = 4
EXPERTS_PER_GROUP = 4
D_FF = 512

LANES = 128
SUBLANES = 8
VMEM_LIMIT = 56 * 1024 * 1024
NEG_BIG = -1e30

NT_DIMS = (((1,), (1,)), ((), ()))


def _params(*sem):
    return pltpu.CompilerParams(dimension_semantics=sem, vmem_limit_bytes=VMEM_LIMIT)


def _rms_rows(x, gain_row):
    ms = jnp.mean(x * x, axis=-1, keepdims=True)
    return x * lax.rsqrt(ms + EPS) * gain_row


IN_TM = 2048
IN_BLOCKS = 11


def _inproj_kernel(x_ref, g_ref, w_ref, wff_ref, out_ref, ff_ref, hn_ref):
    @pl.when(pl.program_id(1) == 0)
    def _():
        hb = _rms_rows(x_ref[...], g_ref[...]).astype(BF16)
        hn_ref[...] = hb
        ff_ref[...] = jnp.dot(hb, wff_ref[...], preferred_element_type=F32)

    out_ref[...] = jnp.dot(hn_ref[...], w_ref[...], preferred_element_type=F32).astype(BF16)


def _in_projection(x2, gain, w_main, w_ff):
    t = x2.shape[0]
    return pl.pallas_call(
        _inproj_kernel,
        grid=(t // IN_TM, IN_BLOCKS),
        in_specs=[
            pl.BlockSpec((IN_TM, D_MODEL), lambda i, j: (i, 0)),
            pl.BlockSpec((1, D_MODEL), lambda i, j: (0, 0)),
            pl.BlockSpec((D_MODEL, D_MODEL), lambda i, j: (0, j)),
            pl.BlockSpec((D_MODEL, LANES), lambda i, j: (0, 0)),
        ],
        out_specs=[
            pl.BlockSpec((IN_TM, D_MODEL), lambda i, j: (i, j)),
            pl.BlockSpec((IN_TM, LANES), lambda i, j: (i, 0)),
        ],
        out_shape=[
            jax.ShapeDtypeStruct((t, IN_BLOCKS * D_MODEL), BF16),
            jax.ShapeDtypeStruct((t, LANES), F32),
        ],
        scratch_shapes=[pltpu.VMEM((IN_TM, D_MODEL), BF16)],
        compiler_params=_params("parallel", "arbitrary"),
        name="in_projection",
    )(x2, gain, w_main, w_ff)


CUM_BLOCK = 512


def _cumsum_kernel(ff_ref, b_ref, out_ref, split_ref):
    s = ff_ref.shape[0]
    r = lax.broadcasted_iota(jnp.int32, (CUM_BLOCK, CUM_BLOCK), 0)
    c = lax.broadcasted_iota(jnp.int32, (CUM_BLOCK, CUM_BLOCK), 1)
    lower = (c <= r).astype(F32)
    carry = jnp.zeros((1, LANES), F32)
    for blk in range(s // CUM_BLOCK):
        rows = slice(blk * CUM_BLOCK, (blk + 1) * CUM_BLOCK)
        z = ff_ref[rows, :] + b_ref[...]
        log_f = jnp.minimum(z, 0.0) - jnp.log1p(jnp.exp(-jnp.abs(z)))
        cs = jnp.dot(lower, log_f, preferred_element_type=F32,
                     precision=lax.Precision.HIGHEST) + carry
        out_ref[rows, :] = cs
        for k, part in enumerate(_split3(cs)):
            split_ref[k, rows, :] = part
        carry = cs[CUM_BLOCK - 1:CUM_BLOCK, :]


def _forget_cumsum(ff, b_pad, batch, seq):
    return pl.pallas_call(
        _cumsum_kernel,
        grid=(batch,),
        in_specs=[
            pl.BlockSpec((seq, LANES), lambda b: (b, 0)),
            pl.BlockSpec((1, LANES), lambda b: (0, 0)),
        ],
        out_specs=[
            pl.BlockSpec((seq, LANES), lambda b: (b, 0)),
            pl.BlockSpec((AUG_SPLIT, seq, LANES), lambda b: (0, b, 0)),
        ],
        out_shape=[
            jax.ShapeDtypeStruct((batch * seq, LANES), F32),
            jax.ShapeDtypeStruct((AUG_SPLIT, batch * seq, LANES), BF16),
        ],
        compiler_params=_params("parallel"),
        name="forget_cumsum",
    )(ff, b_pad)


RET_ROWS = 1024
RET_GROUP = 4


def _retention_kernel(q_ref, k_ref, v_ref, g_ref, cos_ref, sin_ref, intra_ref, qd_ref, kd_ref,
                      cd_ref, out_ref, state_ref):
    hd = RET_HEAD_DIM
    half = hd // 2

    @pl.when(pl.program_id(2) == 0)
    def _():
        state_ref[...] = jnp.zeros_like(state_ref)

    def chunk(c, carry):
        r0 = pl.multiple_of(c * RET_CHUNK, RET_CHUNK)
        rows = pl.ds(r0, RET_CHUNK)
        cs = cos_ref[rows, :]
        sn = sin_ref[rows, :]

        def rot(x):
            x1 = x[:, :half]
            x2 = x[:, half:]
            return jnp.concatenate([x1 * cs - x2 * sn, x2 * cs + x1 * sn], axis=-1)

        for h in range(RET_GROUP):
            cols = slice(h * hd, (h + 1) * hd)
            qr = rot(q_ref[rows, cols].astype(F32))
            kr = rot(k_ref[rows, cols].astype(F32))
            v = v_ref[rows, cols]
            scores = lax.dot_general(qr.astype(BF16), kr.astype(BF16), NT_DIMS,
                                     preferred_element_type=F32) * intra_ref[h]
            st = state_ref[h]
            o = (jnp.dot(scores.astype(BF16), v, preferred_element_type=F32)
                 + jnp.dot((qr * qd_ref[h]).astype(BF16), st.astype(BF16),
                           preferred_element_type=F32))
            kd_t = (kr * kd_ref[h]).T.astype(BF16)
            state_ref[h] = st * cd_ref[h] + jnp.dot(kd_t, v, preferred_element_type=F32)
            on = o * lax.rsqrt(jnp.mean(o * o, axis=-1, keepdims=True) + EPS)
            g = g_ref[rows, cols].astype(F32)
            out_ref[rows, cols] = (on * (g * jax.nn.sigmoid(g))).astype(BF16)
        return carry

    lax.fori_loop(0, RET_ROWS // RET_CHUNK, chunk, 0)


def _retention(proj, cos, sin, intra, qd, kd, cd, batch, seq):
    t = batch * seq
    hd = RET_HEAD_DIM
    width = RET_GROUP * hd
    groups = RET_HEADS // RET_GROUP
    steps = seq // RET_ROWS
    col = lambda off: (lambda b, p, s: (b * steps + s, off + p))
    tab = lambda b, p, s: (p, 0, 0)
    angle = pl.BlockSpec((RET_ROWS, hd // 2), lambda b, p, s: (s, 0))
    return pl.pallas_call(
        _retention_kernel,
        grid=(batch, groups, steps),
        in_specs=[
            pl.BlockSpec((RET_ROWS, width), col(0)),
            pl.BlockSpec((RET_ROWS, width), col(groups)),
            pl.BlockSpec((RET_ROWS, width), col(2 * groups)),
            pl.BlockSpec((RET_ROWS, width), col(3 * groups)),
            angle, angle,
            pl.BlockSpec((RET_GROUP, RET_CHUNK, RET_CHUNK), tab),
            pl.BlockSpec((RET_GROUP, RET_CHUNK, hd), tab),
            pl.BlockSpec((RET_GROUP, RET_CHUNK, hd), tab),
            pl.BlockSpec((RET_GROUP, 1, hd), tab),
        ],
        out_specs=pl.BlockSpec((RET_ROWS, width), lambda b, p, s: (b * steps + s, p)),
        out_shape=jax.ShapeDtypeStruct((t, RET_HEADS * hd), BF16),
        scratch_shapes=[pltpu.VMEM((RET_GROUP, hd, hd), F32)],
        compiler_params=_params("parallel", "parallel", "arbitrary"),
        name="retention",
    )(proj, proj, proj, proj, cos, sin, intra, qd, kd, cd)


def _retention_tables(seq):
    h = jnp.arange(RET_HEADS, dtype=F32)
    log_gamma = jnp.log1p(-(2.0 ** (-5.0 - h)))
    idx = jnp.arange(RET_CHUNK, dtype=F32)
    diff = idx[:, None] - idx[None, :]
    scale = RET_HEAD_DIM ** -0.5
    intra = jnp.where(diff >= 0, jnp.exp(log_gamma[:, None, None] * jnp.maximum(diff, 0.0)), 0.0) * scale
    q_decay = jnp.exp(log_gamma[:, None] * (idx + 1.0))
    k_decay = jnp.exp(log_gamma[:, None] * (RET_CHUNK - 1.0 - idx)) * scale
    chunk_decay = jnp.exp(log_gamma * RET_CHUNK)
    qd = jnp.broadcast_to(q_decay[:, :, None], (RET_HEADS, RET_CHUNK, RET_HEAD_DIM))
    kd = jnp.broadcast_to(k_decay[:, :, None], (RET_HEADS, RET_CHUNK, RET_HEAD_DIM))
    cd = jnp.broadcast_to(chunk_decay[:, None, None], (RET_HEADS, 1, RET_HEAD_DIM))
    inv_freq = ROPE_BASE ** (-jnp.arange(0, RET_HEAD_DIM, 2, dtype=F32) / RET_HEAD_DIM)
    ang = jnp.arange(seq, dtype=F32)[:, None] * inv_freq[None, :]
    return jnp.cos(ang), jnp.sin(ang), intra, qd, kd, cd


FOX_TQ = 512
FOX_TK = 512
assert FOX_TQ == FOX_TK
FOX_QSTEP = 4


def _pair_rms(x, lo_mask, gain_row):
    x2 = x * x
    s_all = jnp.sum(x2, axis=-1, keepdims=True)
    s_lo = jnp.sum(jnp.where(lo_mask, x2, 0.0), axis=-1, keepdims=True)
    ms = jnp.where(lo_mask, s_lo, s_all - s_lo) * (1.0 / FOX_HEAD_DIM)
    return x * lax.rsqrt(ms + EPS) * gain_row


AUG_CQ = 0
AUG_CK = 3
AUG_M = 6
AUG_SPLIT = 3
FOX_EXACT_ABOVE = 30.0
FOX_UNDERFLOW_LOG = -88.0


def _split3(x):
    hi = x.astype(BF16)
    r1 = x - hi.astype(F32)
    mid = r1.astype(BF16)
    lo = (r1 - mid.astype(F32)).astype(BF16)
    return hi, mid, lo


def _lanes_in(lane, start, count=AUG_SPLIT):
    return (lane >= start) & (lane < start + count)


def _fox_kernel(bound_ref, first_ref, q_ref, k_ref, v_ref, c_ref, qg_ref, kg_ref, out_ref,
                kaug_ref, caq_ref, vat_ref, qb_ref, qat_ref, acc_ref, m_ref):
    seq = k_ref.shape[0]
    hp = pl.program_id(1)
    step = pl.program_id(2)
    lane = lax.broadcasted_iota(jnp.int32, (1, LANES), 1)
    lo = lane < FOX_HEAD_DIM
    head_lanes = (lo, jnp.logical_not(lo))
    spare = (FOX_HEAD_DIM, 0)
    bound = bound_ref[0]

    @pl.when(step == 0)
    def _prepare_keys_values():
        r = lax.broadcasted_iota(jnp.int32, (LANES, LANES), 0)
        c = lax.broadcasted_iota(jnp.int32, (LANES, LANES), 1)
        place = []
        for k in range(AUG_SPLIT):
            pk = jnp.zeros((LANES, LANES), F32)
            for h in range(2):
                src = r == 2 * hp + h
                pk = pk + jnp.where(src & (c == spare[h] + AUG_CQ + k), 1.0, 0.0)
                pk = pk - jnp.where(src & (c == spare[h] + AUG_CK + k), 1.0, 0.0)
            place.append(pk.astype(BF16))
        bound_parts = [p.astype(F32) for p in _split3(jnp.full((1, LANES), -bound, F32))]

        def prep(i, carry):
            rows = pl.ds(pl.multiple_of(i * FOX_TK, FOX_TK), FOX_TK)
            e = sum(jnp.dot(c_ref[k, rows, :], place[k], preferred_element_type=F32)
                    for k in range(AUG_SPLIT))
            kn = _pair_rms(k_ref[rows, :].astype(F32), lo, kg_ref[...])
            v = v_ref[rows, :]
            one = jnp.ones_like(v)
            for h in range(2):
                ones_k = jnp.where(_lanes_in(lane, spare[h] + AUG_CQ)
                                   | _lanes_in(lane, spare[h] + AUG_M), 1.0, 0.0)
                kaug_ref[h, rows, :] = jnp.where(
                    head_lanes[h], kn,
                    jnp.where(_lanes_in(lane, spare[h] + AUG_CK), e, ones_k)).astype(BF16)
                const_q = jnp.where(_lanes_in(lane, spare[h] + AUG_CK), 1.0, 0.0)
                for k in range(AUG_SPLIT):
                    const_q = jnp.where(lane == spare[h] + AUG_M + k, bound_parts[k], const_q)
                caq_ref[h, rows, :] = jnp.where(
                    _lanes_in(lane, spare[h] + AUG_CQ), e, const_q).astype(BF16)
                va = jnp.where(head_lanes[h], v, one)
                vat_ref[h, i] = va.T
            return carry
        lax.fori_loop(0, seq // FOX_TK, prep, 0)

    def query_block(sub):
        qi = step * FOX_QSTEP + sub
        first = first_ref[(pl.program_id(0) * pl.num_programs(1) + hp)
                          * (pl.num_programs(2) * FOX_QSTEP) + qi]
        local_rows = pl.ds(sub * FOX_TQ, FOX_TQ)
        t0 = pl.multiple_of(qi * FOX_TQ, FOX_TQ)
        q_rows = pl.ds(t0, FOX_TQ)
        qn = _pair_rms(q_ref[local_rows, :].astype(F32), lo, qg_ref[...]) * (FOX_HEAD_DIM ** -0.5)
        kq = lax.broadcasted_iota(jnp.int32, (FOX_TK, FOX_TQ), 0)
        qq = lax.broadcasted_iota(jnp.int32, (FOX_TK, FOX_TQ), 1)

        q_aug = [jnp.where(head_lanes[h], qn, caq_ref[h, q_rows, :].astype(F32)) for h in range(2)]
        for h in range(2):
            qat_ref[h] = q_aug[h].astype(BF16).T

        @pl.when(bound > FOX_EXACT_ABOVE)
        def _exact_row_max():
            for h in range(2):
                qb_ref[h] = jnp.where(_lanes_in(lane, spare[h] + AUG_M), 0.0, q_aug[h]).astype(BF16)
                m_ref[h] = jnp.full((FOX_TQ, 1), NEG_BIG, F32)

            def scan(j, masked):
                keys = pl.ds(pl.multiple_of(j * FOX_TK, FOX_TK), FOX_TK)
                for h in range(2):
                    s = lax.dot_general(qb_ref[h], kaug_ref[h, keys, :], NT_DIMS,
                                        preferred_element_type=F32)
                    if masked:
                        s = jnp.where(qq <= kq, s, NEG_BIG)
                    m_ref[h] = jnp.maximum(m_ref[h], jnp.max(s, axis=-1, keepdims=True))

            def scan_body(j, carry):
                scan(j, False)
                return carry
            lax.fori_loop(first, qi, scan_body, 0)
            scan(qi, True)
            for h in range(2):
                m_parts = _split3(-m_ref[h])
                qa = q_aug[h]
                for k in range(AUG_SPLIT):
                    qa = jnp.where(lane == spare[h] + AUG_M + k, m_parts[k].astype(F32), qa)
                qat_ref[h] = qa.T.astype(BF16)

        acc_ref[...] = jnp.zeros_like(acc_ref)

        def probs(h, keys, q_cols=slice(None)):
            st = jnp.dot(kaug_ref[h, keys, :], qat_ref[h, :, q_cols], preferred_element_type=F32)
            return jnp.exp(st)

        def kv_blocks(blocks):
            for h in range(2):
                total = None
                for j in blocks:
                    keys = pl.ds(pl.multiple_of(j * FOX_TK, FOX_TK), FOX_TK)
                    o = jnp.dot(vat_ref[h, j], probs(h, keys).astype(BF16),
                                preferred_element_type=F32)
                    total = o if total is None else total + o
                acc_ref[h] += total

        def pair(i, carry):
            kv_blocks((first + 2 * i, first + 2 * i + 1))
            return carry

        full_blocks = qi - first
        lax.fori_loop(0, lax.shift_right_logical(full_blocks, 1), pair, 0)

        @pl.when((full_blocks & 1) == 1)
        def _odd_block():
            kv_blocks((qi - 1,))

        half = FOX_TK // 2
        lower, upper = slice(0, half), slice(half, FOX_TK)
        causal_a = (lax.broadcasted_iota(jnp.int32, (half, FOX_TQ), 0)
                    <= lax.broadcasted_iota(jnp.int32, (half, FOX_TQ), 1))
        causal_b = (lax.broadcasted_iota(jnp.int32, (half, half), 0)
                    <= lax.broadcasted_iota(jnp.int32, (half, half), 1))
        for h in range(2):
            p_a = jnp.where(causal_a, probs(h, pl.ds(t0, half)), 0.0)
            p_b = jnp.where(causal_b, probs(h, pl.ds(t0 + half, half), upper), 0.0)
            acc_ref[h] += jnp.dot(vat_ref[h, qi, :, lower], p_a.astype(BF16),
                                  preferred_element_type=F32)
            acc_ref[h, :, upper] += jnp.dot(vat_ref[h, qi, :, upper], p_b.astype(BF16),
                                            preferred_element_type=F32)

        o0 = acc_ref[0]
        o1 = acc_ref[1]
        l0 = o0[FOX_HEAD_DIM:FOX_HEAD_DIM + 1, :]
        l1 = o1[0:1, :]
        row = lax.broadcasted_iota(jnp.int32, (LANES, 1), 0)
        out_ref[local_rows, :] = jnp.where(row < FOX_HEAD_DIM, o0 / l0, o1 / l1).T.astype(BF16)

    for sub in range(FOX_QSTEP):
        query_block(sub)


def _fox_first_blocks(cum, bound, batch, seq):
    blocks = seq // FOX_TK
    c = cum.reshape(batch, seq, LANES)[:, :, :FOX_HEADS]
    c_start = c[:, 0::FOX_TQ, :]
    c_end = c[:, FOX_TK - 1::FOX_TK, :]
    negligible = (c_start[:, :, None, :] - c_end[:, None, :, :] + 2.0 * bound) < FOX_UNDERFLOW_LOG
    idx = jnp.arange(blocks)
    earlier = (idx[None, :] < idx[:, None])[None, :, :, None]
    count = jnp.sum(negligible & earlier, axis=2)
    first = jnp.min(count.reshape(batch, blocks, FOX_HEADS // 2, 2), axis=-1)
    return jnp.transpose(first, (0, 2, 1)).reshape(-1).astype(jnp.int32)


def _fox_attention(proj, cum_split, bound, first, qg, kg, batch, seq):
    t = batch * seq
    nq = seq // (FOX_QSTEP * FOX_TQ)
    pairs = FOX_HEADS // 2
    base = 4 * (D_MODEL // LANES)
    return pl.pallas_call(
        _fox_kernel,
        grid=(batch, pairs, nq),
        in_specs=[
            pl.BlockSpec(memory_space=pltpu.SMEM),
            pl.BlockSpec(memory_space=pltpu.SMEM),
            pl.BlockSpec((FOX_QSTEP * FOX_TQ, LANES), lambda b, p, i: (b * nq + i, base + p)),
            pl.BlockSpec((seq, LANES), lambda b, p, i: (b, base + pairs + p)),
            pl.BlockSpec((seq, LANES), lambda b, p, i: (b, base + 2 * pairs + p)),
            pl.BlockSpec((AUG_SPLIT, seq, LANES), lambda b, p, i: (0, b, 0)),
            pl.BlockSpec((1, LANES), lambda b, p, i: (0, 0)),
            pl.BlockSpec((1, LANES), lambda b, p, i: (0, 0)),
        ],
        out_specs=pl.BlockSpec((FOX_QSTEP * FOX_TQ, LANES), lambda b, p, i: (b * nq + i, p)),
        out_shape=jax.ShapeDtypeStruct((t, FOX_HEADS * FOX_HEAD_DIM), BF16),
        scratch_shapes=[
            pltpu.VMEM((2, seq, LANES), BF16),
            pltpu.VMEM((2, seq, LANES), BF16),
            pltpu.VMEM((2, seq // FOX_TK, LANES, FOX_TK), BF16),
            pltpu.VMEM((2, FOX_TQ, LANES), BF16),
            pltpu.VMEM((2, LANES, FOX_TQ), BF16),
            pltpu.VMEM((2, LANES, FOX_TQ), F32),
            pltpu.VMEM((2, FOX_TQ, 1), F32),
        ],
        compiler_params=_params("parallel", "parallel", "arbitrary"),
        name="fox_attention",
    )(bound, first, proj, proj, proj, cum_split, qg, kg)


MEMKV_TM = 512
MEMATT_TQ = 1024


def _mem_kv_kernel(mem_ref, g_ref, w_ref, kg_ref, k_out, v_out):
    hb = _rms_rows(mem_ref[...], g_ref[...]).astype(BF16)
    kv = jnp.dot(hb, w_ref[...], preferred_element_type=F32)
    width = MEM_HEADS * MEM_HEAD_DIM
    for h in range(MEM_HEADS):
        cols = slice(h * MEM_HEAD_DIM, (h + 1) * MEM_HEAD_DIM)
        k_out[:, cols] = _rms_rows(kv[:, cols], kg_ref[...]).astype(BF16)
    v_out[...] = kv[:, width:].astype(BF16)


def _mem_kv(mem2, gain, w_kv, k_gain):
    rows = mem2.shape[0]
    width = MEM_HEADS * MEM_HEAD_DIM
    return pl.pallas_call(
        _mem_kv_kernel,
        grid=(rows // MEMKV_TM,),
        in_specs=[
            pl.BlockSpec((MEMKV_TM, D_MODEL), lambda i: (i, 0)),
            pl.BlockSpec((1, D_MODEL), lambda i: (0, 0)),
            pl.BlockSpec((D_MODEL, 2 * width), lambda i: (0, 0)),
            pl.BlockSpec((1, MEM_HEAD_DIM), lambda i: (0, 0)),
        ],
        out_specs=[pl.BlockSpec((MEMKV_TM, width), lambda i: (i, 0))] * 2,
        out_shape=[jax.ShapeDtypeStruct((rows, width), BF16)] * 2,
        compiler_params=_params("parallel"),
        name="mem_kv",
    )(mem2, gain, w_kv, k_gain)


def _mem_attn_kernel(q_ref, k_ref, v_ref, qg_ref, out_ref):
    for h in range(MEM_HEADS):
        cols = slice(h * MEM_HEAD_DIM, (h + 1) * MEM_HEAD_DIM)
        qn = _rms_rows(q_ref[:, cols].astype(F32), qg_ref[...]) * (MEM_HEAD_DIM ** -0.5)
        s = lax.dot_general(qn.astype(BF16), k_ref[:, cols], NT_DIMS, preferred_element_type=F32)
        p = jnp.exp(s - jnp.max(s, axis=-1, keepdims=True))
        denom = jnp.sum(p, axis=-1, keepdims=True)
        o = jnp.dot(p.astype(BF16), v_ref[:, cols], preferred_element_type=F32)
        out_ref[:, cols] = (o / denom).astype(BF16)


def _mem_attention(proj, mk, mv, q_gain, batch, seq, mem_len):
    t = batch * seq
    nq = seq // MEMATT_TQ
    width = MEM_HEADS * MEM_HEAD_DIM
    return pl.pallas_call(
        _mem_attn_kernel,
        grid=(batch, nq),
        in_specs=[
            pl.BlockSpec((MEMATT_TQ, width), lambda b, i: (b * nq + i, 7)),
            pl.BlockSpec((mem_len, width), lambda b, i: (b, 0)),
            pl.BlockSpec((mem_len, width), lambda b, i: (b, 0)),
            pl.BlockSpec((1, MEM_HEAD_DIM), lambda b, i: (0, 0)),
        ],
        out_specs=pl.BlockSpec((MEMATT_TQ, width), lambda b, i: (b * nq + i, 0)),
        out_shape=jax.ShapeDtypeStruct((t, width), BF16),
        compiler_params=_params("parallel", "parallel"),
        name="mem_attention",
    )(proj, mk, mv, q_gain)


MIX_TM = 512
ROUTER_ROWS = 32
PAIRS = ((0, 1), (0, 2), (0, 3), (1, 2), (1, 3), (2, 3))
PAIRS_PER_GROUP = len(PAIRS)
N_CLASSES = N_GROUPS * PAIRS_PER_GROUP
ROW_WIDTH = D_MODEL + LANES


def _top2_sum(b0, b1, b2, b3):
    p, q = jnp.maximum(b0, b1), jnp.minimum(b0, b1)
    r, s = jnp.maximum(b2, b3), jnp.minimum(b2, b3)
    return jnp.maximum(p, r) + jnp.maximum(jnp.minimum(p, r), jnp.maximum(q, s))


def _mix_kernel(ro_ref, fo_ref, mo_ref, gr_ref, gf_ref, gm_ref, x_ref,
                wr_ref, wf_ref, wm_ref, wo_ref, nf_ref, rw_ref, rb_ref,
                xa_ref, cls_ref):
    def branch(a_ref, w_ref, g_ref):
        y = jnp.dot(a_ref[...], w_ref[...], preferred_element_type=F32)
        return jax.nn.sigmoid(g_ref[...].astype(F32)) * y

    merged = (branch(ro_ref, wr_ref, gr_ref) + branch(fo_ref, wf_ref, gf_ref)
              + branch(mo_ref, wm_ref, gm_ref))
    xn = x_ref[...] + jnp.dot(merged.astype(BF16), wo_ref[...], preferred_element_type=F32)
    xa_ref[:, :D_MODEL] = xn
    h = _rms_rows(xn, nf_ref[...])
    h_hi = h.astype(BF16)

    h_lo = (h - h_hi.astype(F32)).astype(BF16)
    rw = rw_ref[...]
    rw_hi = rw.astype(BF16)
    rw_lo = (rw - rw_hi.astype(F32)).astype(BF16)
    dot_nt = lambda a, b: lax.dot_general(a, b, NT_DIMS, preferred_element_type=F32)
    logits = dot_nt(rw_hi, h_hi) + dot_nt(rw_hi, h_lo) + dot_nt(rw_lo, h_hi)
    scores = jax.nn.sigmoid(logits)
    biased = scores + rb_ref[...]
    n = EXPERTS_PER_GROUP
    sc = [scores[SUBLANES * j:SUBLANES * (j + 1)] for j in range(n)]
    bi = [biased[SUBLANES * j:SUBLANES * (j + 1)] for j in range(n)]
    group_score = _top2_sum(*bi)
    rows = lax.broadcasted_iota(jnp.int32, group_score.shape, 0)
    best = jnp.max(group_score, axis=0, keepdims=True)
    top_group = jnp.min(jnp.where(group_score == best, rows, SUBLANES), axis=0, keepdims=True)
    in_group = rows == top_group
    picked, chosen = [], []
    for j in range(n):
        rank = jnp.zeros(group_score.shape, jnp.int32)
        for i in range(n):
            if i == j:
                continue
            ahead = (bi[i] > bi[j]) | ((bi[i] == bi[j]) & (i < j))
            rank = rank + ahead.astype(jnp.int32)
        sel = in_group & (rank < 2)
        picked.append(jnp.where(sel, sc[j], 0.0))
        chosen.append(jnp.max(jnp.where(sel, 1.0, 0.0), axis=0, keepdims=True) > 0.5)
    denom = jnp.sum(picked[0] + picked[1] + picked[2] + picked[3], axis=0, keepdims=True)
    gate = [jnp.sum(picked[j], axis=0, keepdims=True) / denom for j in range(n)]

    first = jnp.full(top_group.shape, n, jnp.int32)
    second = jnp.full(top_group.shape, -1, jnp.int32)
    for j in range(n):
        first = jnp.minimum(first, jnp.where(chosen[j], j, n))
        second = jnp.maximum(second, jnp.where(chosen[j], j, -1))
    pair_base = jnp.where(first == 0, 0, jnp.where(first == 1, 3, 5))
    cls = top_group * PAIRS_PER_GROUP + pair_base + second - first - 1
    cls_ref[...] = jnp.clip(cls, 0, N_CLASSES - 1)
    g_first = sum(jnp.where(first == j, gate[j], 0.0) for j in range(n))
    g_second = sum(jnp.where(second == j, gate[j], 0.0) for j in range(n))
    srow = lax.broadcasted_iota(jnp.int32, (SUBLANES, 1), 0)
    gates8 = jnp.where(srow == 0, g_first, jnp.where(srow == 1, g_second, 0.0))
    pad = jnp.zeros((LANES - SUBLANES, gates8.shape[1]), F32)
    xa_ref[:, D_MODEL:] = jnp.concatenate([gates8, pad], axis=0).T


def _mix_and_route(ro, fo, mo, proj, x2, w_r, w_f, w_m, w_o, norm_ffn, rw_pad, rb_pad):
    t = x2.shape[0]
    tok = lambda i: (i, 0)
    const = lambda i: (0, 0)
    act = pl.BlockSpec((MIX_TM, D_MODEL), tok)
    wspec = pl.BlockSpec((D_MODEL, D_MODEL), const)
    return pl.pallas_call(
        _mix_kernel,
        grid=(t // MIX_TM,),
        in_specs=[
            act, act, act,
            pl.BlockSpec((MIX_TM, D_MODEL), lambda i: (i, 8)),
            pl.BlockSpec((MIX_TM, D_MODEL), lambda i: (i, 9)),
            pl.BlockSpec((MIX_TM, D_MODEL), lambda i: (i, 10)),
            act,
            wspec, wspec, wspec, wspec,
            pl.BlockSpec((1, D_MODEL), const),
            pl.BlockSpec((ROUTER_ROWS, D_MODEL), const),
            pl.BlockSpec((ROUTER_ROWS, 1), const),
        ],
        out_specs=[
            pl.BlockSpec((MIX_TM, ROW_WIDTH), tok),
            pl.BlockSpec((1, MIX_TM), lambda i: (0, i)),
        ],
        out_shape=[
            jax.ShapeDtypeStruct((t, ROW_WIDTH), F32),
            jax.ShapeDtypeStruct((1, t), jnp.int32),
        ],
        compiler_params=_params("parallel"),
        name="mix_and_route",
    )(ro, fo, mo, proj, proj, proj, x2, w_r, w_f, w_m, w_o, norm_ffn, rw_pad, rb_pad)


EXP_TR = 512
POS_TM = 2048
POS_SUB = 512
MOVE_TM = 1024
CLASS_ROWS = 32


def _position_kernel(cls_ref, pos_ref, offs_ref, size_ref, counts_ref, running_ref):
    phase = pl.program_id(0)
    i = pl.program_id(1)

    def class_onehot(cls):
        rows = lax.broadcasted_iota(jnp.int32, (CLASS_ROWS, cls.shape[1]), 0)
        return jnp.where(rows == cls, 1.0, 0.0)

    @pl.when(phase == 0)
    def _count():
        @pl.when(i == 0)
        def _():
            counts_ref[...] = jnp.zeros_like(counts_ref)
        counts_ref[...] += jnp.sum(class_onehot(cls_ref[...]), axis=1, keepdims=True)
        pos_ref[...] = jnp.zeros_like(pos_ref)

    @pl.when(phase == 1)
    def _place():
        @pl.when(i == 0)
        def _():
            padded = jnp.floor((counts_ref[...] + (EXP_TR - 1)) * (1.0 / EXP_TR)) * EXP_TR
            r = lax.broadcasted_iota(jnp.int32, (CLASS_ROWS, CLASS_ROWS), 0)
            c = lax.broadcasted_iota(jnp.int32, (CLASS_ROWS, CLASS_ROWS), 1)
            below = jnp.where(c < r, 1.0, 0.0)
            offs = jnp.dot(below, padded, preferred_element_type=F32,
                           precision=lax.Precision.HIGHEST)
            offs_ref[...] = offs
            size_ref[...] = padded
            running_ref[...] = offs

        r = lax.broadcasted_iota(jnp.int32, (POS_SUB, POS_SUB), 0)
        c = lax.broadcasted_iota(jnp.int32, (POS_SUB, POS_SUB), 1)
        earlier = jnp.where(r < c, 1.0, 0.0).astype(BF16)
        running = running_ref[:, 0:1]
        for u in range(POS_TM // POS_SUB):
            cols = slice(u * POS_SUB, (u + 1) * POS_SUB)
            onehot = class_onehot(cls_ref[:, cols])
            prefix = jnp.dot(onehot.astype(BF16), earlier, preferred_element_type=F32)
            pos_ref[:, cols] = jnp.sum(onehot * (prefix + running), axis=0,
                                       keepdims=True).astype(jnp.int32)
            running = running + jnp.sum(onehot, axis=1, keepdims=True)
        running_ref[...] = jnp.broadcast_to(running, running_ref.shape)


def _positions(cls):
    t = cls.shape[1]
    meta = jax.ShapeDtypeStruct((CLASS_ROWS, LANES), F32)
    return pl.pallas_call(
        _position_kernel,
        grid=(2, t // POS_TM),
        in_specs=[pl.BlockSpec((1, POS_TM), lambda p, i: (0, i))],
        out_specs=[
            pl.BlockSpec((1, POS_TM), lambda p, i: (0, i * p)),
            pl.BlockSpec((CLASS_ROWS, LANES), lambda p, i: (0, 0)),
            pl.BlockSpec((CLASS_ROWS, LANES), lambda p, i: (0, 0)),
        ],
        out_shape=[jax.ShapeDtypeStruct((1, t), jnp.int32), meta, meta],
        scratch_shapes=[pltpu.VMEM((CLASS_ROWS, LANES), F32), pltpu.VMEM((CLASS_ROWS, LANES), F32)],
        compiler_params=_params("arbitrary", "arbitrary"),
        name="moe_positions",
    )(cls)


def _move_rows(copy_for_row):
    def start(group, carry):
        for sub in range(SUBLANES):
            copy_for_row(group, sub).start(priority=sub % 2)
        return carry

    def wait(group, carry):
        for sub in range(SUBLANES):
            copy_for_row(group, sub).wait()
        return carry

    lax.fori_loop(0, MOVE_TM // SUBLANES, start, 0)
    lax.fori_loop(0, MOVE_TM // SUBLANES, wait, 0)


def _scatter_rows_kernel(pos_ref, src_ref, init_ref, dst_ref, sem):
    del init_ref
    _move_rows(lambda group, sub: pltpu.make_async_copy(
        src_ref.at[group, pl.ds(sub, 1)],
        dst_ref.at[pl.ds(pos_ref[group * SUBLANES + sub], 1)], sem))


def _scatter_rows(pos, src, init):
    t, width = src.shape
    return pl.pallas_call(
        _scatter_rows_kernel,
        grid=(t // MOVE_TM,),
        in_specs=[
            pl.BlockSpec((MOVE_TM,), lambda i: (i,), memory_space=pltpu.SMEM),
            pl.BlockSpec((MOVE_TM // SUBLANES, SUBLANES, width), lambda i: (i, 0, 0)),
            pl.BlockSpec(memory_space=pl.ANY),
        ],
        out_specs=pl.BlockSpec(memory_space=pl.ANY),
        out_shape=jax.ShapeDtypeStruct(init.shape, init.dtype),
        scratch_shapes=[pltpu.SemaphoreType.DMA(())],
        input_output_aliases={2: 0},
        compiler_params=_params("arbitrary"),
        name="moe_scatter_rows",
    )(pos, src.reshape(t // SUBLANES, SUBLANES, width), init)


def _gather_rows_kernel(pos_ref, src_ref, dst_ref, sem):
    _move_rows(lambda group, sub: pltpu.make_async_copy(
        src_ref.at[pl.ds(pos_ref[group * SUBLANES + sub], 1)],
        dst_ref.at[group, pl.ds(sub, 1)], sem))


def _gather_rows(pos, src):
    t = pos.shape[0]
    width = src.shape[1]
    out = pl.pallas_call(
        _gather_rows_kernel,
        grid=(t // MOVE_TM,),
        in_specs=[
            pl.BlockSpec((MOVE_TM,), lambda i: (i,), memory_space=pltpu.SMEM),
            pl.BlockSpec(memory_space=pl.ANY),
        ],
        out_specs=pl.BlockSpec((MOVE_TM // SUBLANES, SUBLANES, width), lambda i: (i, 0, 0)),
        out_shape=jax.ShapeDtypeStruct((t // SUBLANES, SUBLANES, width), src.dtype),
        scratch_shapes=[pltpu.SemaphoreType.DMA(())],
        compiler_params=_params("arbitrary"),
        name="moe_gather_rows",
    )(pos, src)
    return out.reshape(t, width)


def _experts_kernel(ea_ref, eb_ref, used_ref, xs_ref, nf_ref,
                    wga_ref, wua_ref, wda_ref, wgb_ref, wub_ref, wdb_ref, ys_ref):
    del ea_ref, eb_ref
    k = pl.program_id(0)

    @pl.when(k < used_ref[0])
    def _():
        x = xs_ref[:, :D_MODEL]
        gates = xs_ref[:, D_MODEL:]
        h = _rms_rows(x, nf_ref[...]).astype(BF16)

        def ffn(wg_ref, wu_ref, wd_ref):
            a = jnp.dot(h, wg_ref[...], preferred_element_type=F32)
            u = jnp.dot(h, wu_ref[...], preferred_element_type=F32)
            return jnp.dot((a * jax.nn.sigmoid(a) * u).astype(BF16), wd_ref[...],
                           preferred_element_type=F32)

        moe = (gates[:, 0:1] * ffn(wga_ref, wua_ref, wda_ref)
               + gates[:, 1:2] * ffn(wgb_ref, wub_ref, wdb_ref))
        ys_ref[...] = x + moe

    @pl.when(k >= used_ref[0])
    def _():
        ys_ref[...] = jnp.zeros_like(ys_ref)


def _experts(xs, norm_ffn, w_gate, w_up, w_down, expert_a, expert_b, n_used):
    n_tiles = xs.shape[0] // EXP_TR
    up = lambda sel: pl.BlockSpec((None, D_MODEL, D_FF), lambda k, ea, eb, nu: (sel(ea, eb)[k], 0, 0))
    down = lambda sel: pl.BlockSpec((None, D_FF, D_MODEL), lambda k, ea, eb, nu: (sel(ea, eb)[k], 0, 0))
    first = lambda ea, eb: ea
    second = lambda ea, eb: eb
    grid_spec = pltpu.PrefetchScalarGridSpec(
        num_scalar_prefetch=3,
        grid=(n_tiles,),
        in_specs=[
            pl.BlockSpec((EXP_TR, ROW_WIDTH), lambda k, ea, eb, nu: (k, 0)),
            pl.BlockSpec((1, D_MODEL), lambda k, ea, eb, nu: (0, 0)),
            up(first), up(first), down(first), up(second), up(second), down(second),
        ],
        out_specs=pl.BlockSpec((EXP_TR, D_MODEL), lambda k, ea, eb, nu: (k, 0)),
    )
    return pl.pallas_call(
        _experts_kernel,
        grid_spec=grid_spec,
        out_shape=jax.ShapeDtypeStruct((xs.shape[0], D_MODEL), F32),
        compiler_params=_params("arbitrary"),
        name="experts",
    )(expert_a, expert_b, n_used, xs, norm_ffn, w_gate, w_up, w_down, w_gate, w_up, w_down)


def _tile_experts(offs, sizes, n_tiles):
    ends = (offs[:N_CLASSES, 0] + sizes[:N_CLASSES, 0]).astype(jnp.int32)
    starts = jnp.arange(n_tiles, dtype=jnp.int32) * EXP_TR
    tile_cls = jnp.minimum(jnp.sum(ends[None, :] <= starts[:, None], axis=1), N_CLASSES - 1)
    group = tile_cls // PAIRS_PER_GROUP
    pair = tile_cls % PAIRS_PER_GROUP
    slot_a = jnp.array([a for a, _ in PAIRS], jnp.int32)[pair]
    slot_b = jnp.array([b for _, b in PAIRS], jnp.int32)[pair]
    expert_a = (group * EXPERTS_PER_GROUP + slot_a).astype(jnp.int32)
    expert_b = (group * EXPERTS_PER_GROUP + slot_b).astype(jnp.int32)
    return expert_a, expert_b, (ends[-1] // EXP_TR).reshape(1)


def _router_layout(router_w, router_bias):
    w = router_w.T.reshape(N_GROUPS, EXPERTS_PER_GROUP, D_MODEL)
    w = jnp.transpose(w, (1, 0, 2))
    w = jnp.pad(w, ((0, 0), (0, SUBLANES - N_GROUPS), (0, 0))).reshape(ROUTER_ROWS, D_MODEL)
    b = jnp.transpose(router_bias.reshape(N_GROUPS, EXPERTS_PER_GROUP), (1, 0))
    b = jnp.pad(b, ((0, 0), (0, SUBLANES - N_GROUPS)), constant_values=NEG_BIG)
    return w.astype(F32), b.reshape(ROUTER_ROWS, 1).astype(F32)


def kernel(x, mem, norm_mix, norm_mem, w_in, b_forget, fox_q_norm, fox_k_norm, mem_q_norm,
           mem_k_norm, w_mem_kv, w_o_ret, w_o_fox, w_o_mem, w_out, norm_ffn, router_w,
           router_bias, w_gate, w_up, w_down):
    batch, seq, d = x.shape
    mem_len = mem.shape[1]
    depth = w_in.shape[0]
    t = batch * seq
    ff_lo = 7 * D_MODEL
    ff_hi = ff_lo + FOX_HEADS

    cos, sin, intra, qd, kd, cd = _retention_tables(seq)
    rw_pad, rb_pad = _router_layout(router_w, router_bias)
    row = lambda v: v.reshape(1, -1).astype(F32)

    x2 = x.reshape(t, d)
    mem2 = mem.reshape(batch * mem_len, d)
    n_tiles = t // EXP_TR + N_CLASSES
    sorted_rows = jnp.zeros((n_tiles * EXP_TR, ROW_WIDTH), F32)
    for l in range(depth):
        w_main = jnp.concatenate([w_in[l, :, :ff_lo], w_in[l, :, ff_hi:]], axis=1).astype(BF16)
        w_ff = jnp.pad(w_in[l, :, ff_lo:ff_hi], ((0, 0), (0, LANES - FOX_HEADS))).astype(BF16)
        b_pad = jnp.pad(b_forget[l], (0, LANES - FOX_HEADS)).reshape(1, LANES)

        proj, ff = _in_projection(x2, row(norm_mix[l]), w_main, w_ff)
        cum, cum_split = _forget_cumsum(ff, b_pad, batch, seq)
        ro = _retention(proj, cos, sin, intra, qd, kd, cd, batch, seq)
        bound = (8.16 * jnp.max(jnp.abs(fox_q_norm[l])) * jnp.max(jnp.abs(fox_k_norm[l])))
        first = _fox_first_blocks(cum, bound, batch, seq)
        fo = _fox_attention(proj, cum_split, bound.reshape(1).astype(F32), first,
                            row(jnp.tile(fox_q_norm[l], 2)), row(jnp.tile(fox_k_norm[l], 2)),
                            batch, seq)
        mk, mv = _mem_kv(mem2, row(norm_mem[l]), w_mem_kv[l].astype(BF16), row(mem_k_norm[l]))
        mo = _mem_attention(proj, mk, mv, row(mem_q_norm[l]), batch, seq, mem_len)
        xa, cls = _mix_and_route(
            ro, fo, mo, proj, x2, w_o_ret[l].astype(BF16), w_o_fox[l].astype(BF16),
            w_o_mem[l].astype(BF16), w_out[l].astype(BF16), row(norm_ffn[l]), rw_pad, rb_pad)
        pos, offs, sizes = _positions(cls)
        pos = pos.reshape(t)
        expert_a, expert_b, n_used = _tile_experts(offs, sizes, n_tiles)
        sorted_rows = _scatter_rows(pos, xa, sorted_rows)
        ys = _experts(sorted_rows, row(norm_ffn[l]), w_gate[l].astype(BF16), w_up[l].astype(BF16),
                      w_down[l].astype(BF16), expert_a, expert_b, n_used)
        x2 = _gather_rows(pos, ys)
    return x2.reshape(batch, seq, d)
```

```python
import functools

import jax
import jax.numpy as jnp
from jax import lax
from jax.experimental import pallas as pl
from jax.experimental.pallas import tpu as pltpu

F32 = jnp.float32
BF16 = jnp.bfloat16

D_MODEL = 1024
EPS = 1e-6
RET_HEADS = 4
RET_HEAD_DIM = 256
RET_CHUNK = 128
ROPE_BASE = 10000.0
FOX_HEADS = 16
FOX_HEAD_DIM = 64
MEM_HEADS = 4
MEM_HEAD_DIM = 256
N_EXPERTS = 16
N_GROUPS = 4
EXPERTS_PER_GROUP = 4
D_FF = 512

LANES = 128
SUBLANES = 8
VMEM_LIMIT = 56 * 1024 * 1024
NEG_BIG = -1e30

NT_DIMS = (((1,), (1,)), ((), ()))


def _params(*sem):
    return pltpu.CompilerParams(dimension_semantics=sem, vmem_limit_bytes=VMEM_LIMIT)


def _rms_rows(x, gain_row):
    ms = jnp.mean(x * x, axis=-1, keepdims=True)
    return x * lax.rsqrt(ms + EPS) * gain_row


IN_TM = 2048
IN_BLOCKS = 11


def _inproj_kernel(x_ref, g_ref, w_ref, wff_ref, out_ref, ff_ref, hn_ref):
    @pl.when(pl.program_id(1) == 0)
    def _():
        hb = _rms_rows(x_ref[...], g_ref[...]).astype(BF16)
        hn_ref[...] = hb
        ff_ref[...] = jnp.dot(hb, wff_ref[...], preferred_element_type=F32)

    out_ref[...] = jnp.dot(hn_ref[...], w_ref[...], preferred_element_type=F32).astype(BF16)


def _in_projection(x2, gain, w_main, w_ff):
    t = x2.shape[0]
    return pl.pallas_call(
        _inproj_kernel,
        grid=(t // IN_TM, IN_BLOCKS),
        in_specs=[
            pl.BlockSpec((IN_TM, D_MODEL), lambda i, j: (i, 0)),
            pl.BlockSpec((1, D_MODEL), lambda i, j: (0, 0)),
            pl.BlockSpec((D_MODEL, D_MODEL), lambda i, j: (0, j)),
            pl.BlockSpec((D_MODEL, LANES), lambda i, j: (0, 0)),
        ],
        out_specs=[
            pl.BlockSpec((IN_TM, D_MODEL), lambda i, j: (i, j)),
            pl.BlockSpec((IN_TM, LANES), lambda i, j: (i, 0)),
        ],
        out_shape=[
            jax.ShapeDtypeStruct((t, IN_BLOCKS * D_MODEL), BF16),
            jax.ShapeDtypeStruct((t, LANES), F32),
        ],
        scratch_shapes=[pltpu.VMEM((IN_TM, D_MODEL), BF16)],
        compiler_params=_params("parallel", "arbitrary"),
        name="in_projection",
    )(x2, gain, w_main, w_ff)


CUM_BLOCK = 512


def _cumsum_kernel(ff_ref, b_ref, out_ref, split_ref):
    s = ff_ref.shape[0]
    r = lax.broadcasted_iota(jnp.int32, (CUM_BLOCK, CUM_BLOCK), 0)
    c = lax.broadcasted_iota(jnp.int32, (CUM_BLOCK, CUM_BLOCK), 1)
    lower = (c <= r).astype(BF16)
    carry = jnp.zeros((1, LANES), F32)
    for blk in range(s // CUM_BLOCK):
        rows = slice(blk * CUM_BLOCK, (blk + 1) * CUM_BLOCK)
        z = ff_ref[rows, :] + b_ref[...]
        log_f = jnp.minimum(z, 0.0) - jnp.log1p(jnp.exp(-jnp.abs(z)))
        cs = sum(jnp.dot(lower, part, preferred_element_type=F32) for part in _split3(log_f)) + carry
        out_ref[rows, :] = cs
        for k, part in enumerate(_split3(cs)):
            split_ref[k, rows, :] = part
        carry = cs[CUM_BLOCK - 1:CUM_BLOCK, :]


def _forget_cumsum(ff, b_pad, batch, seq):
    return pl.pallas_call(
        _cumsum_kernel,
        grid=(batch,),
        in_specs=[
            pl.BlockSpec((seq, LANES), lambda b: (b, 0)),
            pl.BlockSpec((1, LANES), lambda b: (0, 0)),
        ],
        out_specs=[
            pl.BlockSpec((seq, LANES), lambda b: (b, 0)),
            pl.BlockSpec((AUG_SPLIT, seq, LANES), lambda b: (0, b, 0)),
        ],
        out_shape=[
            jax.ShapeDtypeStruct((batch * seq, LANES), F32),
            jax.ShapeDtypeStruct((AUG_SPLIT, batch * seq, LANES), BF16),
        ],
        compiler_params=_params("parallel"),
        name="forget_cumsum",
    )(ff, b_pad)


RET_ROWS = 1024
RET_GROUP = 4


def _retention_kernel(q_ref, k_ref, v_ref, g_ref, cos_ref, sin_ref, intra_ref, qd_ref, kd_ref,
                      cd_ref, out_ref, state_ref):
    hd = RET_HEAD_DIM
    half = hd // 2

    @pl.when(pl.program_id(2) == 0)
    def _():
        state_ref[...] = jnp.zeros_like(state_ref)

    def chunk(c, carry):
        r0 = pl.multiple_of(c * RET_CHUNK, RET_CHUNK)
        rows = pl.ds(r0, RET_CHUNK)
        cs = cos_ref[rows, :]
        sn = sin_ref[rows, :]

        def rot(x):
            x1 = x[:, :half]
            x2 = x[:, half:]
            return jnp.concatenate([x1 * cs - x2 * sn, x2 * cs + x1 * sn], axis=-1)

        for h in range(RET_GROUP):
            cols = slice(h * hd, (h + 1) * hd)
            qr = rot(q_ref[rows, cols].astype(F32))
            kr = rot(k_ref[rows, cols].astype(F32))
            v = v_ref[rows, cols]
            scores = lax.dot_general(qr.astype(BF16), kr.astype(BF16), NT_DIMS,
                                     preferred_element_type=F32) * intra_ref[h]
            st = state_ref[h]
            o = (jnp.dot(scores.astype(BF16), v, preferred_element_type=F32)
                 + jnp.dot((qr * qd_ref[h]).astype(BF16), st.astype(BF16),
                           preferred_element_type=F32))
            kd_t = (kr * kd_ref[h]).T.astype(BF16)
            state_ref[h] = st * cd_ref[h] + jnp.dot(kd_t, v, preferred_element_type=F32)
            on = o * lax.rsqrt(jnp.mean(o * o, axis=-1, keepdims=True) + EPS)
            g = g_ref[rows, cols].astype(F32)
            out_ref[rows, cols] = (on * (g * jax.nn.sigmoid(g))).astype(BF16)
        return carry

    lax.fori_loop(0, RET_ROWS // RET_CHUNK, chunk, 0)


def _retention(proj, cos, sin, intra, qd, kd, cd, batch, seq):
    t = batch * seq
    hd = RET_HEAD_DIM
    width = RET_GROUP * hd
    groups = RET_HEADS // RET_GROUP
    steps = seq // RET_ROWS
    col = lambda off: (lambda b, p, s: (b * steps + s, off + p))
    tab = lambda b, p, s: (p, 0, 0)
    angle = pl.BlockSpec((RET_ROWS, hd // 2), lambda b, p, s: (s, 0))
    return pl.pallas_call(
        _retention_kernel,
        grid=(batch, groups, steps),
        in_specs=[
            pl.BlockSpec((RET_ROWS, width), col(0)),
            pl.BlockSpec((RET_ROWS, width), col(groups)),
            pl.BlockSpec((RET_ROWS, width), col(2 * groups)),
            pl.BlockSpec((RET_ROWS, width), col(3 * groups)),
            angle, angle,
            pl.BlockSpec((RET_GROUP, RET_CHUNK, RET_CHUNK), tab),
            pl.BlockSpec((RET_GROUP, RET_CHUNK, hd), tab),
            pl.BlockSpec((RET_GROUP, RET_CHUNK, hd), tab),
            pl.BlockSpec((RET_GROUP, 1, hd), tab),
        ],
        out_specs=pl.BlockSpec((RET_ROWS, width), lambda b, p, s: (b * steps + s, p)),
        out_shape=jax.ShapeDtypeStruct((t, RET_HEADS * hd), BF16),
        scratch_shapes=[pltpu.VMEM((RET_GROUP, hd, hd), F32)],
        compiler_params=_params("parallel", "parallel", "arbitrary"),
        name="retention",
    )(proj, proj, proj, proj, cos, sin, intra, qd, kd, cd)


def _retention_tables(seq):
    h = jnp.arange(RET_HEADS, dtype=F32)
    log_gamma = jnp.log1p(-(2.0 ** (-5.0 - h)))
    idx = jnp.arange(RET_CHUNK, dtype=F32)
    diff = idx[:, None] - idx[None, :]
    scale = RET_HEAD_DIM ** -0.5
    intra = jnp.where(diff >= 0, jnp.exp(log_gamma[:, None, None] * jnp.maximum(diff, 0.0)), 0.0) * scale
    q_decay = jnp.exp(log_gamma[:, None] * (idx + 1.0))
    k_decay = jnp.exp(log_gamma[:, None] * (RET_CHUNK - 1.0 - idx)) * scale
    chunk_decay = jnp.exp(log_gamma * RET_CHUNK)
    qd = jnp.broadcast_to(q_decay[:, :, None], (RET_HEADS, RET_CHUNK, RET_HEAD_DIM))
    kd = jnp.broadcast_to(k_decay[:, :, None], (RET_HEADS, RET_CHUNK, RET_HEAD_DIM))
    cd = jnp.broadcast_to(chunk_decay[:, None, None], (RET_HEADS, 1, RET_HEAD_DIM))
    inv_freq = ROPE_BASE ** (-jnp.arange(0, RET_HEAD_DIM, 2, dtype=F32) / RET_HEAD_DIM)
    ang = jnp.arange(seq, dtype=F32)[:, None] * inv_freq[None, :]
    return jnp.cos(ang), jnp.sin(ang), intra, qd, kd, cd


FOX_TQ = 512
FOX_TK = 512
assert FOX_TQ == FOX_TK
FOX_QSTEP = 8


def _pair_rms(x, lo_mask, gain_row):
    x2 = x * x
    s_all = jnp.sum(x2, axis=-1, keepdims=True)
    s_lo = jnp.sum(jnp.where(lo_mask, x2, 0.0), axis=-1, keepdims=True)
    ms = jnp.where(lo_mask, s_lo, s_all - s_lo) * (1.0 / FOX_HEAD_DIM)
    return x * lax.rsqrt(ms + EPS) * gain_row


AUG_CQ = 0
AUG_CK = 3
AUG_M = 6
AUG_SPLIT = 3
FOX_EXACT_ABOVE = 30.0
FOX_UNDERFLOW_LOG = -88.0


def _split3(x):
    hi = x.astype(BF16)
    r1 = x - hi.astype(F32)
    mid = r1.astype(BF16)
    lo = (r1 - mid.astype(F32)).astype(BF16)
    return hi, mid, lo


def _lanes_in(lane, start, count=AUG_SPLIT):
    return (lane >= start) & (lane < start + count)


def _fox_kernel(bound_ref, first_ref, q_ref, k_ref, v_ref, c_ref, qg_ref, kg_ref, out_ref,
                kaug_ref, caq_ref, vat_ref, qb_ref, qat_ref, acc_ref, m_ref):
    seq = k_ref.shape[0]
    hp = pl.program_id(1)
    step = pl.program_id(2)
    lane = lax.broadcasted_iota(jnp.int32, (1, LANES), 1)
    lo = lane < FOX_HEAD_DIM
    head_lanes = (lo, jnp.logical_not(lo))
    spare = (FOX_HEAD_DIM, 0)
    bound = bound_ref[0]

    @pl.when(step == 0)
    def _prepare_keys_values():
        r = lax.broadcasted_iota(jnp.int32, (LANES, LANES), 0)
        c = lax.broadcasted_iota(jnp.int32, (LANES, LANES), 1)
        place = []
        for k in range(AUG_SPLIT):
            pk = jnp.zeros((LANES, LANES), F32)
            for h in range(2):
                src = r == 2 * hp + h
                pk = pk + jnp.where(src & (c == spare[h] + AUG_CQ + k), 1.0, 0.0)
                pk = pk - jnp.where(src & (c == spare[h] + AUG_CK + k), 1.0, 0.0)
            place.append(pk.astype(BF16))
        bound_parts = [p.astype(F32) for p in _split3(jnp.full((1, LANES), -bound, F32))]

        def prep(i, carry):
            rows = pl.ds(pl.multiple_of(i * FOX_TK, FOX_TK), FOX_TK)
            e = sum(jnp.dot(c_ref[k, rows, :], place[k], preferred_element_type=F32)
                    for k in range(AUG_SPLIT))
            kn = _pair_rms(k_ref[rows, :].astype(F32), lo, kg_ref[...])
            v = v_ref[rows, :]
            one = jnp.ones_like(v)
            for h in range(2):
                ones_k = jnp.where(_lanes_in(lane, spare[h] + AUG_CQ)
                                   | _lanes_in(lane, spare[h] + AUG_M), 1.0, 0.0)
                kaug_ref[h, rows, :] = jnp.where(
                    head_lanes[h], kn,
                    jnp.where(_lanes_in(lane, spare[h] + AUG_CK), e, ones_k)).astype(BF16)
                const_q = jnp.where(_lanes_in(lane, spare[h] + AUG_CK), 1.0, 0.0)
                for k in range(AUG_SPLIT):
                    const_q = jnp.where(lane == spare[h] + AUG_M + k, bound_parts[k], const_q)
                caq_ref[h, rows, :] = jnp.where(
                    _lanes_in(lane, spare[h] + AUG_CQ), e, const_q).astype(BF16)
                va = jnp.where(head_lanes[h], v, one)
                vat_ref[h, i] = va.T
            return carry
        lax.fori_loop(0, seq // FOX_TK, prep, 0)

    def query_block(sub):
        qi = step * FOX_QSTEP + sub
        first = first_ref[(pl.program_id(0) * pl.num_programs(1) + hp)
                          * (pl.num_programs(2) * FOX_QSTEP) + qi]
        local_rows = pl.ds(sub * FOX_TQ, FOX_TQ)
        t0 = pl.multiple_of(qi * FOX_TQ, FOX_TQ)
        q_rows = pl.ds(t0, FOX_TQ)
        qn = _pair_rms(q_ref[local_rows, :].astype(F32), lo, qg_ref[...]) * (FOX_HEAD_DIM ** -0.5)
        kq = lax.broadcasted_iota(jnp.int32, (FOX_TK, FOX_TQ), 0)
        qq = lax.broadcasted_iota(jnp.int32, (FOX_TK, FOX_TQ), 1)

        q_aug = [jnp.where(head_lanes[h], qn, caq_ref[h, q_rows, :].astype(F32)) for h in range(2)]
        for h in range(2):
            qat_ref[h] = q_aug[h].astype(BF16).T

        @pl.when(bound > FOX_EXACT_ABOVE)
        def _exact_row_max():
            for h in range(2):
                qb_ref[h] = jnp.where(_lanes_in(lane, spare[h] + AUG_M), 0.0, q_aug[h]).astype(BF16)
                m_ref[h] = jnp.full((FOX_TQ, 1), NEG_BIG, F32)

            def scan(j, masked):
                keys = pl.ds(pl.multiple_of(j * FOX_TK, FOX_TK), FOX_TK)
                for h in range(2):
                    s = lax.dot_general(qb_ref[h], kaug_ref[h, keys, :], NT_DIMS,
                                        preferred_element_type=F32)
                    if masked:
                        s = jnp.where(qq <= kq, s, NEG_BIG)
                    m_ref[h] = jnp.maximum(m_ref[h], jnp.max(s, axis=-1, keepdims=True))

            def scan_body(j, carry):
                scan(j, False)
                return carry
            lax.fori_loop(first, qi, scan_body, 0)
            scan(qi, True)
            for h in range(2):
                m_parts = _split3(-m_ref[h])
                qa = q_aug[h]
                for k in range(AUG_SPLIT):
                    qa = jnp.where(lane == spare[h] + AUG_M + k, m_parts[k].astype(F32), qa)
                qat_ref[h] = qa.T.astype(BF16)

        acc_ref[...] = jnp.zeros_like(acc_ref)

        def probs(h, keys, q_cols=slice(None)):
            st = jnp.dot(kaug_ref[h, keys, :], qat_ref[h, :, q_cols], preferred_element_type=F32)
            return jnp.exp(st)

        def kv_blocks(blocks):
            for h in range(2):
                total = None
                for j in blocks:
                    keys = pl.ds(pl.multiple_of(j * FOX_TK, FOX_TK), FOX_TK)
                    o = jnp.dot(vat_ref[h, j], probs(h, keys).astype(BF16),
                                preferred_element_type=F32)
                    total = o if total is None else total + o
                acc_ref[h] += total

        def pair(i, carry):
            kv_blocks((first + 2 * i, first + 2 * i + 1))
            return carry

        full_blocks = qi - first
        lax.fori_loop(0, lax.shift_right_logical(full_blocks, 1), pair, 0)

        @pl.when((full_blocks & 1) == 1)
        def _odd_block():
            kv_blocks((qi - 1,))

        half = FOX_TK // 2
        lower, upper = slice(0, half), slice(half, FOX_TK)
        causal_a = (lax.broadcasted_iota(jnp.int32, (half, FOX_TQ), 0)
                    <= lax.broadcasted_iota(jnp.int32, (half, FOX_TQ), 1))
        causal_b = (lax.broadcasted_iota(jnp.int32, (half, half), 0)
                    <= lax.broadcasted_iota(jnp.int32, (half, half), 1))
        for h in range(2):
            p_a = jnp.where(causal_a, probs(h, pl.ds(t0, half)), 0.0)
            p_b = jnp.where(causal_b, probs(h, pl.ds(t0 + half, half), upper), 0.0)
            acc_ref[h] += jnp.dot(vat_ref[h, qi, :, lower], p_a.astype(BF16),
                                  preferred_element_type=F32)
            acc_ref[h, :, upper] += jnp.dot(vat_ref[h, qi, :, upper], p_b.astype(BF16),
                                            preferred_element_type=F32)

        o0 = acc_ref[0]
        o1 = acc_ref[1]
        l0 = o0[FOX_HEAD_DIM:FOX_HEAD_DIM + 1, :]
        l1 = o1[0:1, :]
        row = lax.broadcasted_iota(jnp.int32, (LANES, 1), 0)
        out_ref[local_rows, :] = jnp.where(row < FOX_HEAD_DIM, o0 / l0, o1 / l1).T.astype(BF16)

    for sub in range(FOX_QSTEP):
        query_block(sub)


def _fox_first_blocks(cum, bound, batch, seq):
    blocks = seq // FOX_TK
    c = cum.reshape(batch, seq, LANES)[:, :, :FOX_HEADS]
    c_start = c[:, 0::FOX_TQ, :]
    c_end = c[:, FOX_TK - 1::FOX_TK, :]
    negligible = (c_start[:, :, None, :] - c_end[:, None, :, :] + 2.0 * bound) < FOX_UNDERFLOW_LOG
    idx = jnp.arange(blocks)
    earlier = (idx[None, :] < idx[:, None])[None, :, :, None]
    count = jnp.sum(negligible & earlier, axis=2)
    first = jnp.min(count.reshape(batch, blocks, FOX_HEADS // 2, 2), axis=-1)
    return jnp.transpose(first, (0, 2, 1)).reshape(-1).astype(jnp.int32)


def _fox_attention(proj, cum_split, bound, first, qg, kg, batch, seq):
    t = batch * seq
    nq = seq // (FOX_QSTEP * FOX_TQ)
    pairs = FOX_HEADS // 2
    base = 4 * (D_MODEL // LANES)
    return pl.pallas_call(
        _fox_kernel,
        grid=(batch, pairs, nq),
        in_specs=[
            pl.BlockSpec(memory_space=pltpu.SMEM),
            pl.BlockSpec(memory_space=pltpu.SMEM),
            pl.BlockSpec((FOX_QSTEP * FOX_TQ, LANES), lambda b, p, i: (b * nq + i, base + p)),
            pl.BlockSpec((seq, LANES), lambda b, p, i: (b, base + pairs + p)),
            pl.BlockSpec((seq, LANES), lambda b, p, i: (b, base + 2 * pairs + p)),
            pl.BlockSpec((AUG_SPLIT, seq, LANES), lambda b, p, i: (0, b, 0)),
            pl.BlockSpec((1, LANES), lambda b, p, i: (0, 0)),
            pl.BlockSpec((1, LANES), lambda b, p, i: (0, 0)),
        ],
        out_specs=pl.BlockSpec((FOX_QSTEP * FOX_TQ, LANES), lambda b, p, i: (b * nq + i, p)),
        out_shape=jax.ShapeDtypeStruct((t, FOX_HEADS * FOX_HEAD_DIM), BF16),
        scratch_shapes=[
            pltpu.VMEM((2, seq, LANES), BF16),
            pltpu.VMEM((2, seq, LANES), BF16),
            pltpu.VMEM((2, seq // FOX_TK, LANES, FOX_TK), BF16),
            pltpu.VMEM((2, FOX_TQ, LANES), BF16),
            pltpu.VMEM((2, LANES, FOX_TQ), BF16),
            pltpu.VMEM((2, LANES, FOX_TQ), F32),
            pltpu.VMEM((2, FOX_TQ, 1), F32),
        ],
        compiler_params=_params("parallel", "parallel", "arbitrary"),
        name="fox_attention",
    )(bound, first, proj, proj, proj, cum_split, qg, kg)


MEMKV_TM = 512
MEMATT_TQ = 1024


def _mem_kv_kernel(mem_ref, g_ref, w_ref, kg_ref, k_out, v_out):
    hb = _rms_rows(mem_ref[...], g_ref[...]).astype(BF16)
    kv = jnp.dot(hb, w_ref[...], preferred_element_type=F32)
    width = MEM_HEADS * MEM_HEAD_DIM
    for h in range(MEM_HEADS):
        cols = slice(h * MEM_HEAD_DIM, (h + 1) * MEM_HEAD_DIM)
        k_out[:, cols] = _rms_rows(kv[:, cols], kg_ref[...]).astype(BF16)
    v_out[...] = kv[:, width:].astype(BF16)


def _mem_kv(mem2, gain, w_kv, k_gain):
    rows = mem2.shape[0]
    width = MEM_HEADS * MEM_HEAD_DIM
    return pl.pallas_call(
        _mem_kv_kernel,
        grid=(rows // MEMKV_TM,),
        in_specs=[
            pl.BlockSpec((MEMKV_TM, D_MODEL), lambda i: (i, 0)),
            pl.BlockSpec((1, D_MODEL), lambda i: (0, 0)),
            pl.BlockSpec((D_MODEL, 2 * width), lambda i: (0, 0)),
            pl.BlockSpec((1, MEM_HEAD_DIM), lambda i: (0, 0)),
        ],
        out_specs=[pl.BlockSpec((MEMKV_TM, width), lambda i: (i, 0))] * 2,
        out_shape=[jax.ShapeDtypeStruct((rows, width), BF16)] * 2,
        compiler_params=_params("parallel"),
        name="mem_kv",
    )(mem2, gain, w_kv, k_gain)


def _mem_attn_kernel(q_ref, k_ref, v_ref, qg_ref, out_ref):
    for h in range(MEM_HEADS):
        cols = slice(h * MEM_HEAD_DIM, (h + 1) * MEM_HEAD_DIM)
        qn = _rms_rows(q_ref[:, cols].astype(F32), qg_ref[...]) * (MEM_HEAD_DIM ** -0.5)
        s = lax.dot_general(qn.astype(BF16), k_ref[:, cols], NT_DIMS, preferred_element_type=F32)
        p = jnp.exp(s - jnp.max(s, axis=-1, keepdims=True))
        denom = jnp.sum(p, axis=-1, keepdims=True)
        o = jnp.dot(p.astype(BF16), v_ref[:, cols], preferred_element_type=F32)
        out_ref[:, cols] = (o / denom).astype(BF16)


def _mem_attention(proj, mk, mv, q_gain, batch, seq, mem_len):
    t = batch * seq
    nq = seq // MEMATT_TQ
    width = MEM_HEADS * MEM_HEAD_DIM
    return pl.pallas_call(
        _mem_attn_kernel,
        grid=(batch, nq),
        in_specs=[
            pl.BlockSpec((MEMATT_TQ, width), lambda b, i: (b * nq + i, 7)),
            pl.BlockSpec((mem_len, width), lambda b, i: (b, 0)),
            pl.BlockSpec((mem_len, width), lambda b, i: (b, 0)),
            pl.BlockSpec((1, MEM_HEAD_DIM), lambda b, i: (0, 0)),
        ],
        out_specs=pl.BlockSpec((MEMATT_TQ, width), lambda b, i: (b * nq + i, 0)),
        out_shape=jax.ShapeDtypeStruct((t, width), BF16),
        compiler_params=_params("parallel", "parallel"),
        name="mem_attention",
    )(proj, mk, mv, q_gain)


MIX_TM = 512
ROUTER_ROWS = 32
PAIRS = ((0, 1), (0, 2), (0, 3), (1, 2), (1, 3), (2, 3))
PAIRS_PER_GROUP = len(PAIRS)
N_CLASSES = N_GROUPS * PAIRS_PER_GROUP
ROW_WIDTH = D_MODEL + LANES


def _top2_sum(b0, b1, b2, b3):
    p, q = jnp.maximum(b0, b1), jnp.minimum(b0, b1)
    r, s = jnp.maximum(b2, b3), jnp.minimum(b2, b3)
    return jnp.maximum(p, r) + jnp.maximum(jnp.minimum(p, r), jnp.maximum(q, s))


def _mix_kernel(ro_ref, fo_ref, mo_ref, gr_ref, gf_ref, gm_ref, x_ref,
                wr_ref, wf_ref, wm_ref, wo_ref, nf_ref, rw_ref, rb_ref,
                xa_ref, cls_ref):
    def branch(a_ref, w_ref, g_ref):
        y = jnp.dot(a_ref[...], w_ref[...], preferred_element_type=F32)
        return jax.nn.sigmoid(g_ref[...].astype(F32)) * y

    merged = (branch(ro_ref, wr_ref, gr_ref) + branch(fo_ref, wf_ref, gf_ref)
              + branch(mo_ref, wm_ref, gm_ref))
    xn = x_ref[...] + jnp.dot(merged.astype(BF16), wo_ref[...], preferred_element_type=F32)
    xa_ref[:, :D_MODEL] = xn
    h = _rms_rows(xn, nf_ref[...])
    h_hi = h.astype(BF16)

    h_lo = (h - h_hi.astype(F32)).astype(BF16)
    rw = rw_ref[...]
    rw_hi = rw.astype(BF16)
    rw_lo = (rw - rw_hi.astype(F32)).astype(BF16)
    dot_nt = lambda a, b: lax.dot_general(a, b, NT_DIMS, preferred_element_type=F32)
    logits = dot_nt(rw_hi, h_hi) + dot_nt(rw_hi, h_lo) + dot_nt(rw_lo, h_hi)
    scores = jax.nn.sigmoid(logits)
    biased = scores + rb_ref[...]
    n = EXPERTS_PER_GROUP
    sc = [scores[SUBLANES * j:SUBLANES * (j + 1)] for j in range(n)]
    bi = [biased[SUBLANES * j:SUBLANES * (j + 1)] for j in range(n)]
    group_score = _top2_sum(*bi)
    rows = lax.broadcasted_iota(jnp.int32, group_score.shape, 0)
    best = jnp.max(group_score, axis=0, keepdims=True)
    top_group = jnp.min(jnp.where(group_score == best, rows, SUBLANES), axis=0, keepdims=True)
    in_group = rows == top_group
    picked, chosen = [], []
    for j in range(n):
        rank = jnp.zeros(group_score.shape, jnp.int32)
        for i in range(n):
            if i == j:
                continue
            ahead = (bi[i] > bi[j]) | ((bi[i] == bi[j]) & (i < j))
            rank = rank + ahead.astype(jnp.int32)
        sel = in_group & (rank < 2)
        picked.append(jnp.where(sel, sc[j], 0.0))
        chosen.append(jnp.max(jnp.where(sel, 1.0, 0.0), axis=0, keepdims=True) > 0.5)
    denom = jnp.sum(picked[0] + picked[1] + picked[2] + picked[3], axis=0, keepdims=True)
    gate = [jnp.sum(picked[j], axis=0, keepdims=True) / denom for j in range(n)]

    first = jnp.full(top_group.shape, n, jnp.int32)
    second = jnp.full(top_group.shape, -1, jnp.int32)
    for j in range(n):
        first = jnp.minimum(first, jnp.where(chosen[j], j, n))
        second = jnp.maximum(second, jnp.where(chosen[j], j, -1))
    pair_base = jnp.where(first == 0, 0, jnp.where(first == 1, 3, 5))
    cls = top_group * PAIRS_PER_GROUP + pair_base + second - first - 1
    cls_ref[...] = jnp.clip(cls, 0, N_CLASSES - 1)
    g_first = sum(jnp.where(first == j, gate[j], 0.0) for j in range(n))
    g_second = sum(jnp.where(second == j, gate[j], 0.0) for j in range(n))
    srow = lax.broadcasted_iota(jnp.int32, (SUBLANES, 1), 0)
    gates8 = jnp.where(srow == 0, g_first, jnp.where(srow == 1, g_second, 0.0))
    pad = jnp.zeros((LANES - SUBLANES, gates8.shape[1]), F32)
    xa_ref[:, D_MODEL:] = jnp.concatenate([gates8, pad], axis=0).T


def _mix_and_route(ro, fo, mo, proj, x2, w_r, w_f, w_m, w_o, norm_ffn, rw_pad, rb_pad):
    t = x2.shape[0]
    tok = lambda i: (i, 0)
    const = lambda i: (0, 0)
    act = pl.BlockSpec((MIX_TM, D_MODEL), tok)
    wspec = pl.BlockSpec((D_MODEL, D_MODEL), const)
    return pl.pallas_call(
        _mix_kernel,
        grid=(t // MIX_TM,),
        in_specs=[
            act, act, act,
            pl.BlockSpec((MIX_TM, D_MODEL), lambda i: (i, 8)),
            pl.BlockSpec((MIX_TM, D_MODEL), lambda i: (i, 9)),
            pl.BlockSpec((MIX_TM, D_MODEL), lambda i: (i, 10)),
            act,
            wspec, wspec, wspec, wspec,
            pl.BlockSpec((1, D_MODEL), const),
            pl.BlockSpec((ROUTER_ROWS, D_MODEL), const),
            pl.BlockSpec((ROUTER_ROWS, 1), const),
        ],
        out_specs=[
            pl.BlockSpec((MIX_TM, ROW_WIDTH), tok),
            pl.BlockSpec((1, MIX_TM), lambda i: (0, i)),
        ],
        out_shape=[
            jax.ShapeDtypeStruct((t, ROW_WIDTH), F32),
            jax.ShapeDtypeStruct((1, t), jnp.int32),
        ],
        compiler_params=_params("parallel"),
        name="mix_and_route",
    )(ro, fo, mo, proj, proj, proj, x2, w_r, w_f, w_m, w_o, norm_ffn, rw_pad, rb_pad)


EXP_TR = 512
POS_TM = 2048
POS_SUB = 512
MOVE_TM = 1024
CLASS_ROWS = 32


def _position_kernel(cls_ref, pos_ref, offs_ref, size_ref, counts_ref, running_ref):
    phase = pl.program_id(0)
    i = pl.program_id(1)

    def class_onehot(cls):
        rows = lax.broadcasted_iota(jnp.int32, (CLASS_ROWS, cls.shape[1]), 0)
        return jnp.where(rows == cls, 1.0, 0.0)

    @pl.when(phase == 0)
    def _count():
        @pl.when(i == 0)
        def _():
            counts_ref[...] = jnp.zeros_like(counts_ref)
        counts_ref[...] += jnp.sum(class_onehot(cls_ref[...]), axis=1, keepdims=True)
        pos_ref[...] = jnp.zeros_like(pos_ref)

    @pl.when(phase == 1)
    def _place():
        @pl.when(i == 0)
        def _():
            padded = jnp.floor((counts_ref[...] + (EXP_TR - 1)) * (1.0 / EXP_TR)) * EXP_TR
            r = lax.broadcasted_iota(jnp.int32, (CLASS_ROWS, CLASS_ROWS), 0)
            c = lax.broadcasted_iota(jnp.int32, (CLASS_ROWS, CLASS_ROWS), 1)
            below = jnp.where(c < r, 1.0, 0.0)
            offs = jnp.dot(below, padded, preferred_element_type=F32,
                           precision=lax.Precision.HIGHEST)
            offs_ref[...] = offs
            size_ref[...] = padded
            running_ref[...] = offs

        r = lax.broadcasted_iota(jnp.int32, (POS_SUB, POS_SUB), 0)
        c = lax.broadcasted_iota(jnp.int32, (POS_SUB, POS_SUB), 1)
        earlier = jnp.where(r < c, 1.0, 0.0).astype(BF16)
        running = running_ref[:, 0:1]
        for u in range(POS_TM // POS_SUB):
            cols = slice(u * POS_SUB, (u + 1) * POS_SUB)
            onehot = class_onehot(cls_ref[:, cols])
            prefix = jnp.dot(onehot.astype(BF16), earlier, preferred_element_type=F32)
            pos_ref[:, cols] = jnp.sum(onehot * (prefix + running), axis=0,
                                       keepdims=True).astype(jnp.int32)
            running = running + jnp.sum(onehot, axis=1, keepdims=True)
        running_ref[...] = jnp.broadcast_to(running, running_ref.shape)


def _positions(cls):
    t = cls.shape[1]
    meta = jax.ShapeDtypeStruct((CLASS_ROWS, LANES), F32)
    return pl.pallas_call(
        _position_kernel,
        grid=(2, t // POS_TM),
        in_specs=[pl.BlockSpec((1, POS_TM), lambda p, i: (0, i))],
        out_specs=[
            pl.BlockSpec((1, POS_TM), lambda p, i: (0, i * p)),
            pl.BlockSpec((CLASS_ROWS, LANES), lambda p, i: (0, 0)),
            pl.BlockSpec((CLASS_ROWS, LANES), lambda p, i: (0, 0)),
        ],
        out_shape=[jax.ShapeDtypeStruct((1, t), jnp.int32), meta, meta],
        scratch_shapes=[pltpu.VMEM((CLASS_ROWS, LANES), F32), pltpu.VMEM((CLASS_ROWS, LANES), F32)],
        compiler_params=_params("arbitrary", "arbitrary"),
        name="moe_positions",
    )(cls)


def _move_rows(copy_for_row):
    def start(group, carry):
        for sub in range(SUBLANES):
            copy_for_row(group, sub).start(priority=sub % 2)
        return carry

    def wait(group, carry):
        for sub in range(SUBLANES):
            copy_for_row(group, sub).wait()
        return carry

    lax.fori_loop(0, MOVE_TM // SUBLANES, start, 0)
    lax.fori_loop(0, MOVE_TM // SUBLANES, wait, 0)


def _scatter_rows_kernel(pos_ref, src_ref, init_ref, dst_ref, sem):
    del init_ref
    _move_rows(lambda group, sub: pltpu.make_async_copy(
        src_ref.at[group, pl.ds(sub, 1)],
        dst_ref.at[pl.ds(pos_ref[group * SUBLANES + sub], 1)], sem))


def _scatter_rows(pos, src, init):
    t, width = src.shape
    return pl.pallas_call(
        _scatter_rows_kernel,
        grid=(t // MOVE_TM,),
        in_specs=[
            pl.BlockSpec((MOVE_TM,), lambda i: (i,), memory_space=pltpu.SMEM),
            pl.BlockSpec((MOVE_TM // SUBLANES, SUBLANES, width), lambda i: (i, 0, 0)),
            pl.BlockSpec(memory_space=pl.ANY),
        ],
        out_specs=pl.BlockSpec(memory_space=pl.ANY),
        out_shape=jax.ShapeDtypeStruct(init.shape, init.dtype),
        scratch_shapes=[pltpu.SemaphoreType.DMA(())],
        input_output_aliases={2: 0},
        compiler_params=_params("arbitrary"),
        name="moe_scatter_rows",
    )(pos, src.reshape(t // SUBLANES, SUBLANES, width), init)


def _gather_rows_kernel(pos_ref, src_ref, dst_ref, sem):
    _move_rows(lambda group, sub: pltpu.make_async_copy(
        src_ref.at[pl.ds(pos_ref[group * SUBLANES + sub], 1)],
        dst_ref.at[group, pl.ds(sub, 1)], sem))


def _gather_rows(pos, src):
    t = pos.shape[0]
    width = src.shape[1]
    out = pl.pallas_call(
        _gather_rows_kernel,
        grid=(t // MOVE_TM,),
        in_specs=[
            pl.BlockSpec((MOVE_TM,), lambda i: (i,), memory_space=pltpu.SMEM),
            pl.BlockSpec(memory_space=pl.ANY),
        ],
        out_specs=pl.BlockSpec((MOVE_TM // SUBLANES, SUBLANES, width), lambda i: (i, 0, 0)),
        out_shape=jax.ShapeDtypeStruct((t // SUBLANES, SUBLANES, width), src.dtype),
        scratch_shapes=[pltpu.SemaphoreType.DMA(())],
        compiler_params=_params("arbitrary"),
        name="moe_gather_rows",
    )(pos, src)
    return out.reshape(t, width)


def _experts_kernel(ea_ref, eb_ref, used_ref, xs_ref, nf_ref,
                    wga_ref, wua_ref, wda_ref, wgb_ref, wub_ref, wdb_ref, ys_ref):
    del ea_ref, eb_ref
    k = pl.program_id(0)

    @pl.when(k < used_ref[0])
    def _():
        x = xs_ref[:, :D_MODEL]
        gates = xs_ref[:, D_MODEL:]
        h = _rms_rows(x, nf_ref[...]).astype(BF16)

        def ffn(wg_ref, wu_ref, wd_ref):
            a = jnp.dot(h, wg_ref[...], preferred_element_type=F32)
            u = jnp.dot(h, wu_ref[...], preferred_element_type=F32)
            return jnp.dot((a * jax.nn.sigmoid(a) * u).astype(BF16), wd_ref[...],
                           preferred_element_type=F32)

        moe = (gates[:, 0:1] * ffn(wga_ref, wua_ref, wda_ref)
               + gates[:, 1:2] * ffn(wgb_ref, wub_ref, wdb_ref))
        ys_ref[...] = x + moe

    @pl.when(k >= used_ref[0])
    def _():
        ys_ref[...] = jnp.zeros_like(ys_ref)


def _experts(xs, norm_ffn, w_gate, w_up, w_down, expert_a, expert_b, n_used):
    n_tiles = xs.shape[0] // EXP_TR
    up = lambda sel: pl.BlockSpec((None, D_MODEL, D_FF), lambda k, ea, eb, nu: (sel(ea, eb)[k], 0, 0))
    down = lambda sel: pl.BlockSpec((None, D_FF, D_MODEL), lambda k, ea, eb, nu: (sel(ea, eb)[k], 0, 0))
    first = lambda ea, eb: ea
    second = lambda ea, eb: eb
    grid_spec = pltpu.PrefetchScalarGridSpec(
        num_scalar_prefetch=3,
        grid=(n_tiles,),
        in_specs=[
            pl.BlockSpec((EXP_TR, ROW_WIDTH), lambda k, ea, eb, nu: (k, 0)),
            pl.BlockSpec((1, D_MODEL), lambda k, ea, eb, nu: (0, 0)),
            up(first), up(first), down(first), up(second), up(second), down(second),
        ],
        out_specs=pl.BlockSpec((EXP_TR, D_MODEL), lambda k, ea, eb, nu: (k, 0)),
    )
    return pl.pallas_call(
        _experts_kernel,
        grid_spec=grid_spec,
        out_shape=jax.ShapeDtypeStruct((xs.shape[0], D_MODEL), F32),
        compiler_params=_params("arbitrary"),
        name="experts",
    )(expert_a, expert_b, n_used, xs, norm_ffn, w_gate, w_up, w_down, w_gate, w_up, w_down)


def _tile_experts(offs, sizes, n_tiles):
    ends = (offs[:N_CLASSES, 0] + sizes[:N_CLASSES, 0]).astype(jnp.int32)
    starts = jnp.arange(n_tiles, dtype=jnp.int32) * EXP_TR
    tile_cls = jnp.minimum(jnp.sum(ends[None, :] <= starts[:, None], axis=1), N_CLASSES - 1)
    group = tile_cls // PAIRS_PER_GROUP
    pair = tile_cls % PAIRS_PER_GROUP
    slot_a = jnp.array([a for a, _ in PAIRS], jnp.int32)[pair]
    slot_b = jnp.array([b for _, b in PAIRS], jnp.int32)[pair]
    expert_a = (group * EXPERTS_PER_GROUP + slot_a).astype(jnp.int32)
    expert_b = (group * EXPERTS_PER_GROUP + slot_b).astype(jnp.int32)
    return expert_a, expert_b, (ends[-1] // EXP_TR).reshape(1)


def _router_layout(router_w, router_bias):
    w = router_w.T.reshape(N_GROUPS, EXPERTS_PER_GROUP, D_MODEL)
    w = jnp.transpose(w, (1, 0, 2))
    w = jnp.pad(w, ((0, 0), (0, SUBLANES - N_GROUPS), (0, 0))).reshape(ROUTER_ROWS, D_MODEL)
    b = jnp.transpose(router_bias.reshape(N_GROUPS, EXPERTS_PER_GROUP), (1, 0))
    b = jnp.pad(b, ((0, 0), (0, SUBLANES - N_GROUPS)), constant_values=NEG_BIG)
    return w.astype(F32), b.reshape(ROUTER_ROWS, 1).astype(F32)


def kernel(x, mem, norm_mix, norm_mem, w_in, b_forget, fox_q_norm, fox_k_norm, mem_q_norm,
           mem_k_norm, w_mem_kv, w_o_ret, w_o_fox, w_o_mem, w_out, norm_ffn, router_w,
           router_bias, w_gate, w_up, w_down):
    batch, seq, d = x.shape
    mem_len = mem.shape[1]
    depth = w_in.shape[0]
    t = batch * seq
    ff_lo = 7 * D_MODEL
    ff_hi = ff_lo + FOX_HEADS

    cos, sin, intra, qd, kd, cd = _retention_tables(seq)
    rw_pad, rb_pad = _router_layout(router_w, router_bias)
    row = lambda v: v.reshape(1, -1).astype(F32)

    x2 = x.reshape(t, d)
    mem2 = mem.reshape(batch * mem_len, d)
    n_tiles = t // EXP_TR + N_CLASSES
    sorted_rows = jnp.zeros((n_tiles * EXP_TR, ROW_WIDTH), F32)
    for l in range(depth):
        w_main = jnp.concatenate([w_in[l, :, :ff_lo], w_in[l, :, ff_hi:]], axis=1).astype(BF16)
        w_ff = jnp.pad(w_in[l, :, ff_lo:ff_hi], ((0, 0), (0, LANES - FOX_HEADS))).astype(BF16)
        b_pad = jnp.pad(b_forget[l], (0, LANES - FOX_HEADS)).reshape(1, LANES)

        proj, ff = _in_projection(x2, row(norm_mix[l]), w_main, w_ff)
        cum, cum_split = _forget_cumsum(ff, b_pad, batch, seq)
        ro = _retention(proj, cos, sin, intra, qd, kd, cd, batch, seq)
        bound = (8.16 * jnp.max(jnp.abs(fox_q_norm[l])) * jnp.max(jnp.abs(fox_k_norm[l])))
        first = _fox_first_blocks(cum, bound, batch, seq)
        fo = _fox_attention(proj, cum_split, bound.reshape(1).astype(F32), first,
                            row(jnp.tile(fox_q_norm[l], 2)), row(jnp.tile(fox_k_norm[l], 2)),
                            batch, seq)
        mk, mv = _mem_kv(mem2, row(norm_mem[l]), w_mem_kv[l].astype(BF16), row(mem_k_norm[l]))
        mo = _mem_attention(proj, mk, mv, row(mem_q_norm[l]), batch, seq, mem_len)
        xa, cls = _mix_and_route(
            ro, fo, mo, proj, x2, w_o_ret[l].astype(BF16), w_o_fox[l].astype(BF16),
            w_o_mem[l].astype(BF16), w_out[l].astype(BF16), row(norm_ffn[l]), rw_pad, rb_pad)
        pos, offs, sizes = _positions(cls)
        pos = pos.reshape(t)
        expert_a, expert_b, n_used = _tile_experts(offs, sizes, n_tiles)
        sorted_rows = _scatter_rows(pos, xa, sorted_rows)
        ys = _experts(sorted_rows, row(norm_ffn[l]), w_gate[l].astype(BF16), w_up[l].astype(BF16),
                      w_down[l].astype(BF16), expert_a, expert_b, n_used)
        x2 = _gather_rows(pos, ys)
    return x2.reshape(batch, seq, d)
```

```python
import jax
import jax.numpy as jnp
from jax import lax
from jax.experimental import pallas as pl
from jax.experimental.pallas import tpu as pltpu

F32 = jnp.float32
BF16 = jnp.bfloat16

D_MODEL = 1024
EPS = 1e-6
RET_HEADS = 4
RET_HEAD_DIM = 256
RET_CHUNK = 128
ROPE_BASE = 10000.0
FOX_HEADS = 16
FOX_HEAD_DIM = 64
MEM_HEADS = 4
MEM_HEAD_DIM = 256
N_GROUPS = 4
EXPERTS_PER_GROUP = 4
D_FF = 512

LANES = 128
SUBLANES = 8
VMEM_LIMIT = 56 * 1024 * 1024
NEG_BIG = -1e30

NT_DIMS = (((1,), (1,)), ((), ()))


def _params(*sem):
    return pltpu.CompilerParams(dimension_semantics=sem, vmem_limit_bytes=VMEM_LIMIT)


def _rms_rows(x, gain_row):
    ms = jnp.mean(x * x, axis=-1, keepdims=True)
    return x * lax.rsqrt(ms + EPS) * gain_row


IN_TM = 2048
IN_BLOCKS = 11


def _inproj_kernel(x_ref, g_ref, w_ref, wff_ref, out_ref, ff_ref, hn_ref):
    @pl.when(pl.program_id(1) == 0)
    def _():
        hb = _rms_rows(x_ref[...], g_ref[...]).astype(BF16)
        hn_ref[...] = hb
        ff_ref[...] = jnp.dot(hb, wff_ref[...], preferred_element_type=F32)

    out_ref[...] = jnp.dot(hn_ref[...], w_ref[...], preferred_element_type=F32).astype(BF16)


def _in_projection(x2, gain, w_main, w_ff):
    t = x2.shape[0]
    return pl.pallas_call(
        _inproj_kernel,
        grid=(t // IN_TM, IN_BLOCKS),
        in_specs=[
            pl.BlockSpec((IN_TM, D_MODEL), lambda i, j: (i, 0)),
            pl.BlockSpec((1, D_MODEL), lambda i, j: (0, 0)),
            pl.BlockSpec((D_MODEL, D_MODEL), lambda i, j: (0, j)),
            pl.BlockSpec((D_MODEL, LANES), lambda i, j: (0, 0)),
        ],
        out_specs=[
            pl.BlockSpec((IN_TM, D_MODEL), lambda i, j: (i, j)),
            pl.BlockSpec((IN_TM, LANES), lambda i, j: (i, 0)),
        ],
        out_shape=[
            jax.ShapeDtypeStruct((t, IN_BLOCKS * D_MODEL), BF16),
            jax.ShapeDtypeStruct((t, LANES), F32),
        ],
        scratch_shapes=[pltpu.VMEM((IN_TM, D_MODEL), BF16)],
        compiler_params=_params("parallel", "arbitrary"),
        name="in_projection",
    )(x2, gain, w_main, w_ff)


CUM_BLOCK = 512
AUG_SPLIT = 3


def _split3(x):
    hi = x.astype(BF16)
    r1 = x - hi.astype(F32)
    mid = r1.astype(BF16)
    lo = (r1 - mid.astype(F32)).astype(BF16)
    return hi, mid, lo


def _cumsum_kernel(ff_ref, b_ref, out_ref, split_ref):
    s = ff_ref.shape[0]
    r = lax.broadcasted_iota(jnp.int32, (CUM_BLOCK, CUM_BLOCK), 0)
    c = lax.broadcasted_iota(jnp.int32, (CUM_BLOCK, CUM_BLOCK), 1)
    lower = (c <= r).astype(BF16)
    carry = jnp.zeros((1, LANES), F32)
    for blk in range(s // CUM_BLOCK):
        rows = slice(blk * CUM_BLOCK, (blk + 1) * CUM_BLOCK)
        z = ff_ref[rows, :] + b_ref[...]
        log_f = jnp.minimum(z, 0.0) - jnp.log1p(jnp.exp(-jnp.abs(z)))
        cs = sum(jnp.dot(lower, part, preferred_element_type=F32) for part in _split3(log_f)) + carry
        out_ref[rows, :] = cs
        for k, part in enumerate(_split3(cs)):
            split_ref[k, rows, :] = part
        carry = cs[CUM_BLOCK - 1:CUM_BLOCK, :]


def _forget_cumsum(ff, b_pad, batch, seq):
    return pl.pallas_call(
        _cumsum_kernel,
        grid=(batch,),
        in_specs=[
            pl.BlockSpec((seq, LANES), lambda b: (b, 0)),
            pl.BlockSpec((1, LANES), lambda b: (0, 0)),
        ],
        out_specs=[
            pl.BlockSpec((seq, LANES), lambda b: (b, 0)),
            pl.BlockSpec((AUG_SPLIT, seq, LANES), lambda b: (0, b, 0)),
        ],
        out_shape=[
            jax.ShapeDtypeStruct((batch * seq, LANES), F32),
            jax.ShapeDtypeStruct((AUG_SPLIT, batch * seq, LANES), BF16),
        ],
        compiler_params=_params("parallel"),
        name="forget_cumsum",
    )(ff, b_pad)


RET_ROWS = 1024
RET_GROUP = 4


def _retention_kernel(q_ref, k_ref, v_ref, g_ref, cos_ref, sin_ref, intra_ref, qd_ref, kd_ref,
                      cd_ref, out_ref, state_ref):
    hd = RET_HEAD_DIM
    half = hd // 2

    @pl.when(pl.program_id(2) == 0)
    def _():
        state_ref[...] = jnp.zeros_like(state_ref)

    def chunk(c, carry):
        r0 = pl.multiple_of(c * RET_CHUNK, RET_CHUNK)
        rows = pl.ds(r0, RET_CHUNK)
        cs = cos_ref[rows, :]
        sn = sin_ref[rows, :]

        def rot(x):
            x1 = x[:, :half]
            x2 = x[:, half:]
            return jnp.concatenate([x1 * cs - x2 * sn, x2 * cs + x1 * sn], axis=-1)

        for h in range(RET_GROUP):
            cols = slice(h * hd, (h + 1) * hd)
            qr = rot(q_ref[rows, cols].astype(F32))
            kr = rot(k_ref[rows, cols].astype(F32))
            v = v_ref[rows, cols]
            scores = lax.dot_general(qr.astype(BF16), kr.astype(BF16), NT_DIMS,
                                     preferred_element_type=F32) * intra_ref[h]
            st = state_ref[h]
            o = (jnp.dot(scores.astype(BF16), v, preferred_element_type=F32)
                 + jnp.dot((qr * qd_ref[h]).astype(BF16), st.astype(BF16),
                           preferred_element_type=F32))
            kd_t = (kr * kd_ref[h]).T.astype(BF16)
            state_ref[h] = st * cd_ref[h] + jnp.dot(kd_t, v, preferred_element_type=F32)
            on = o * lax.rsqrt(jnp.mean(o * o, axis=-1, keepdims=True) + EPS)
            g = g_ref[rows, cols].astype(F32)
            out_ref[rows, cols] = (on * (g * jax.nn.sigmoid(g))).astype(BF16)
        return carry

    lax.fori_loop(0, RET_ROWS // RET_CHUNK, chunk, 0)


def _retention(proj, cos, sin, intra, qd, kd, cd, batch, seq):
    t = batch * seq
    hd = RET_HEAD_DIM
    width = RET_GROUP * hd
    groups = RET_HEADS // RET_GROUP
    steps = seq // RET_ROWS
    col = lambda off: (lambda b, p, s: (b * steps + s, off + p))
    tab = lambda b, p, s: (p, 0, 0)
    angle = pl.BlockSpec((RET_ROWS, hd // 2), lambda b, p, s: (s, 0))
    return pl.pallas_call(
        _retention_kernel,
        grid=(batch, groups, steps),
        in_specs=[
            pl.BlockSpec((RET_ROWS, width), col(0)),
            pl.BlockSpec((RET_ROWS, width), col(groups)),
            pl.BlockSpec((RET_ROWS, width), col(2 * groups)),
            pl.BlockSpec((RET_ROWS, width), col(3 * groups)),
            angle, angle,
            pl.BlockSpec((RET_GROUP, RET_CHUNK, RET_CHUNK), tab),
            pl.BlockSpec((RET_GROUP, RET_CHUNK, hd), tab),
            pl.BlockSpec((RET_GROUP, RET_CHUNK, hd), tab),
            pl.BlockSpec((RET_GROUP, 1, hd), tab),
        ],
        out_specs=pl.BlockSpec((RET_ROWS, width), lambda b, p, s: (b * steps + s, p)),
        out_shape=jax.ShapeDtypeStruct((t, RET_HEADS * hd), BF16),
        scratch_shapes=[pltpu.VMEM((RET_GROUP, hd, hd), F32)],
        compiler_params=_params("parallel", "parallel", "arbitrary"),
        name="retention",
    )(proj, proj, proj, proj, cos, sin, intra, qd, kd, cd)


def _retention_tables(seq):
    h = jnp.arange(RET_HEADS, dtype=F32)
    log_gamma = jnp.log1p(-(2.0 ** (-5.0 - h)))
    idx = jnp.arange(RET_CHUNK, dtype=F32)
    diff = idx[:, None] - idx[None, :]
    scale = RET_HEAD_DIM ** -0.5
    intra = jnp.where(diff >= 0, jnp.exp(log_gamma[:, None, None] * jnp.maximum(diff, 0.0)), 0.0) * scale
    q_decay = jnp.exp(log_gamma[:, None] * (idx + 1.0))
    k_decay = jnp.exp(log_gamma[:, None] * (RET_CHUNK - 1.0 - idx)) * scale
    chunk_decay = jnp.exp(log_gamma * RET_CHUNK)
    qd = jnp.broadcast_to(q_decay[:, :, None], (RET_HEADS, RET_CHUNK, RET_HEAD_DIM))
    kd = jnp.broadcast_to(k_decay[:, :, None], (RET_HEADS, RET_CHUNK, RET_HEAD_DIM))
    cd = jnp.broadcast_to(chunk_decay[:, None, None], (RET_HEADS, 1, RET_HEAD_DIM))
    inv_freq = ROPE_BASE ** (-jnp.arange(0, RET_HEAD_DIM, 2, dtype=F32) / RET_HEAD_DIM)
    ang = jnp.arange(seq, dtype=F32)[:, None] * inv_freq[None, :]
    return jnp.cos(ang), jnp.sin(ang), intra, qd, kd, cd


FOX_TQ = 512
FOX_TK = 512
assert FOX_TQ == FOX_TK
FOX_QSTEP = 8


def _pair_rms(x, lo_mask, gain_row):
    x2 = x * x
    s_all = jnp.sum(x2, axis=-1, keepdims=True)
    s_lo = jnp.sum(jnp.where(lo_mask, x2, 0.0), axis=-1, keepdims=True)
    ms = jnp.where(lo_mask, s_lo, s_all - s_lo) * (1.0 / FOX_HEAD_DIM)
    return x * lax.rsqrt(ms + EPS) * gain_row


AUG_CQ = 0
AUG_CK = 3
AUG_M = 6
FOX_EXACT_ABOVE = 30.0
FOX_BOUND_SCALE = 8.0 * 1.02
FOX_UNDERFLOW_LOG = -88.0


def _lanes_in(lane, start, count=AUG_SPLIT):
    return (lane >= start) & (lane < start + count)


def _fox_kernel(bound_ref, first_ref, q_ref, k_ref, v_ref, c_ref, qg_ref, kg_ref, out_ref,
                kaug_ref, caq_ref, vat_ref, qb_ref, qat_ref, acc_ref, m_ref):
    seq = k_ref.shape[0]
    hp = pl.program_id(1)
    step = pl.program_id(2)
    lane = lax.broadcasted_iota(jnp.int32, (1, LANES), 1)
    lo = lane < FOX_HEAD_DIM
    head_lanes = (lo, jnp.logical_not(lo))
    spare = (FOX_HEAD_DIM, 0)
    bound = bound_ref[0]

    @pl.when(step == 0)
    def _prepare_keys_values():
        r = lax.broadcasted_iota(jnp.int32, (LANES, LANES), 0)
        c = lax.broadcasted_iota(jnp.int32, (LANES, LANES), 1)
        place = []
        for k in range(AUG_SPLIT):
            pk = jnp.zeros((LANES, LANES), F32)
            for h in range(2):
                src = r == 2 * hp + h
                pk = pk + jnp.where(src & (c == spare[h] + AUG_CQ + k), 1.0, 0.0)
                pk = pk - jnp.where(src & (c == spare[h] + AUG_CK + k), 1.0, 0.0)
            place.append(pk.astype(BF16))
        bound_parts = [p.astype(F32) for p in _split3(jnp.full((1, LANES), -bound, F32))]

        def prep(i, carry):
            rows = pl.ds(pl.multiple_of(i * FOX_TK, FOX_TK), FOX_TK)
            e = sum(jnp.dot(c_ref[k, rows, :], place[k], preferred_element_type=F32)
                    for k in range(AUG_SPLIT))
            kn = _pair_rms(k_ref[rows, :].astype(F32), lo, kg_ref[...])
            v = v_ref[rows, :]
            one = jnp.ones_like(v)
            for h in range(2):
                ones_k = jnp.where(_lanes_in(lane, spare[h] + AUG_CQ)
                                   | _lanes_in(lane, spare[h] + AUG_M), 1.0, 0.0)
                kaug_ref[h, rows, :] = jnp.where(
                    head_lanes[h], kn,
                    jnp.where(_lanes_in(lane, spare[h] + AUG_CK), e, ones_k)).astype(BF16)
                const_q = jnp.where(_lanes_in(lane, spare[h] + AUG_CK), 1.0, 0.0)
                for k in range(AUG_SPLIT):
                    const_q = jnp.where(lane == spare[h] + AUG_M + k, bound_parts[k], const_q)
                caq_ref[h, rows, :] = jnp.where(
                    _lanes_in(lane, spare[h] + AUG_CQ), e, const_q).astype(BF16)
                va = jnp.where(head_lanes[h], v, one)
                vat_ref[h, i] = va.T
            return carry
        lax.fori_loop(0, seq // FOX_TK, prep, 0)

    def query_block(sub):
        qi = step * FOX_QSTEP + sub
        first = first_ref[(pl.program_id(0) * pl.num_programs(1) + hp)
                          * (pl.num_programs(2) * FOX_QSTEP) + qi]
        local_rows = pl.ds(sub * FOX_TQ, FOX_TQ)
        t0 = pl.multiple_of(qi * FOX_TQ, FOX_TQ)
        q_rows = pl.ds(t0, FOX_TQ)
        qn = _pair_rms(q_ref[local_rows, :].astype(F32), lo, qg_ref[...]) * (FOX_HEAD_DIM ** -0.5)
        kq = lax.broadcasted_iota(jnp.int32, (FOX_TK, FOX_TQ), 0)
        qq = lax.broadcasted_iota(jnp.int32, (FOX_TK, FOX_TQ), 1)

        q_aug = [jnp.where(head_lanes[h], qn, caq_ref[h, q_rows, :].astype(F32)) for h in range(2)]
        for h in range(2):
            qat_ref[h] = q_aug[h].astype(BF16).T

        @pl.when(bound > FOX_EXACT_ABOVE)
        def _exact_row_max():
            for h in range(2):
                qb_ref[h] = jnp.where(_lanes_in(lane, spare[h] + AUG_M), 0.0, q_aug[h]).astype(BF16)
                m_ref[h] = jnp.full((FOX_TQ, 1), NEG_BIG, F32)

            def scan(j, masked):
                keys = pl.ds(pl.multiple_of(j * FOX_TK, FOX_TK), FOX_TK)
                for h in range(2):
                    s = lax.dot_general(qb_ref[h], kaug_ref[h, keys, :], NT_DIMS,
                                        preferred_element_type=F32)
                    if masked:
                        s = jnp.where(qq <= kq, s, NEG_BIG)
                    m_ref[h] = jnp.maximum(m_ref[h], jnp.max(s, axis=-1, keepdims=True))

            def scan_body(j, carry):
                scan(j, False)
                return carry
            lax.fori_loop(first, qi, scan_body, 0)
            scan(qi, True)
            for h in range(2):
                m_parts = _split3(-m_ref[h])
                qa = q_aug[h]
                for k in range(AUG_SPLIT):
                    qa = jnp.where(lane == spare[h] + AUG_M + k, m_parts[k].astype(F32), qa)
                qat_ref[h] = qa.T.astype(BF16)

        acc_ref[...] = jnp.zeros_like(acc_ref)

        def probs(h, keys, q_cols=slice(None)):
            st = jnp.dot(kaug_ref[h, keys, :], qat_ref[h, :, q_cols], preferred_element_type=F32)
            return jnp.exp(st)

        def kv_blocks(blocks):
            for h in range(2):
                total = None
                for j in blocks:
                    keys = pl.ds(pl.multiple_of(j * FOX_TK, FOX_TK), FOX_TK)
                    o = jnp.dot(vat_ref[h, j], probs(h, keys).astype(BF16),
                                preferred_element_type=F32)
                    total = o if total is None else total + o
                acc_ref[h] += total

        def pair(i, carry):
            kv_blocks((first + 2 * i, first + 2 * i + 1))
            return carry

        full_blocks = qi - first
        lax.fori_loop(0, lax.shift_right_logical(full_blocks, 1), pair, 0)

        @pl.when((full_blocks & 1) == 1)
        def _odd_block():
            kv_blocks((qi - 1,))

        half = FOX_TK // 2
        lower, upper = slice(0, half), slice(half, FOX_TK)
        causal_a = (lax.broadcasted_iota(jnp.int32, (half, FOX_TQ), 0)
                    <= lax.broadcasted_iota(jnp.int32, (half, FOX_TQ), 1))
        causal_b = (lax.broadcasted_iota(jnp.int32, (half, half), 0)
                    <= lax.broadcasted_iota(jnp.int32, (half, half), 1))
        for h in range(2):
            p_a = jnp.where(causal_a, probs(h, pl.ds(t0, half)), 0.0)
            p_b = jnp.where(causal_b, probs(h, pl.ds(t0 + half, half), upper), 0.0)
            acc_ref[h] += jnp.dot(vat_ref[h, qi, :, lower], p_a.astype(BF16),
                                  preferred_element_type=F32)
            acc_ref[h, :, upper] += jnp.dot(vat_ref[h, qi, :, upper], p_b.astype(BF16),
                                            preferred_element_type=F32)

        o0 = acc_ref[0]
        o1 = acc_ref[1]
        l0 = o0[FOX_HEAD_DIM:FOX_HEAD_DIM + 1, :]
        l1 = o1[0:1, :]
        row = lax.broadcasted_iota(jnp.int32, (LANES, 1), 0)
        out_ref[local_rows, :] = jnp.where(row < FOX_HEAD_DIM, o0 / l0, o1 / l1).T.astype(BF16)

    for sub in range(FOX_QSTEP):
        query_block(sub)


def _fox_first_blocks(cum, bound, batch, seq):
    blocks = seq // FOX_TK
    c = cum.reshape(batch, seq, LANES)[:, :, :FOX_HEADS]
    c_start = c[:, 0::FOX_TQ, :]
    c_end = c[:, FOX_TK - 1::FOX_TK, :]
    negligible = (c_start[:, :, None, :] - c_end[:, None, :, :] + 2.0 * bound) < FOX_UNDERFLOW_LOG
    idx = jnp.arange(blocks)
    earlier = (idx[None, :] < idx[:, None])[None, :, :, None]
    count = jnp.sum(negligible & earlier, axis=2)
    first = jnp.min(count.reshape(batch, blocks, FOX_HEADS // 2, 2), axis=-1)
    return jnp.transpose(first, (0, 2, 1)).reshape(-1).astype(jnp.int32)


def _fox_attention(proj, cum_split, bound, first, qg, kg, batch, seq):
    t = batch * seq
    nq = seq // (FOX_QSTEP * FOX_TQ)
    pairs = FOX_HEADS // 2
    base = 4 * (D_MODEL // LANES)
    return pl.pallas_call(
        _fox_kernel,
        grid=(batch, pairs, nq),
        in_specs=[
            pl.BlockSpec(memory_space=pltpu.SMEM),
            pl.BlockSpec(memory_space=pltpu.SMEM),
            pl.BlockSpec((FOX_QSTEP * FOX_TQ, LANES), lambda b, p, i: (b * nq + i, base + p)),
            pl.BlockSpec((seq, LANES), lambda b, p, i: (b, base + pairs + p)),
            pl.BlockSpec((seq, LANES), lambda b, p, i: (b, base + 2 * pairs + p)),
            pl.BlockSpec((AUG_SPLIT, seq, LANES), lambda b, p, i: (0, b, 0)),
            pl.BlockSpec((1, LANES), lambda b, p, i: (0, 0)),
            pl.BlockSpec((1, LANES), lambda b, p, i: (0, 0)),
        ],
        out_specs=pl.BlockSpec((FOX_QSTEP * FOX_TQ, LANES), lambda b, p, i: (b * nq + i, p)),
        out_shape=jax.ShapeDtypeStruct((t, FOX_HEADS * FOX_HEAD_DIM), BF16),
        scratch_shapes=[
            pltpu.VMEM((2, seq, LANES), BF16),
            pltpu.VMEM((2, seq, LANES), BF16),
            pltpu.VMEM((2, seq // FOX_TK, LANES, FOX_TK), BF16),
            pltpu.VMEM((2, FOX_TQ, LANES), BF16),
            pltpu.VMEM((2, LANES, FOX_TQ), BF16),
            pltpu.VMEM((2, LANES, FOX_TQ), F32),
            pltpu.VMEM((2, FOX_TQ, 1), F32),
        ],
        compiler_params=_params("parallel", "parallel", "arbitrary"),
        name="fox_attention",
    )(bound, first, proj, proj, proj, cum_split, qg, kg)


MEMKV_TM = 512
MEMATT_TQ = 1024


def _mem_kv_kernel(mem_ref, g_ref, w_ref, kg_ref, k_out, v_out):
    hb = _rms_rows(mem_ref[...], g_ref[...]).astype(BF16)
    kv = jnp.dot(hb, w_ref[...], preferred_element_type=F32)
    width = MEM_HEADS * MEM_HEAD_DIM
    for h in range(MEM_HEADS):
        cols = slice(h * MEM_HEAD_DIM, (h + 1) * MEM_HEAD_DIM)
        k_out[:, cols] = _rms_rows(kv[:, cols], kg_ref[...]).astype(BF16)
    v_out[...] = kv[:, width:].astype(BF16)


def _mem_kv(mem2, gain, w_kv, k_gain):
    rows = mem2.shape[0]
    width = MEM_HEADS * MEM_HEAD_DIM
    return pl.pallas_call(
        _mem_kv_kernel,
        grid=(rows // MEMKV_TM,),
        in_specs=[
            pl.BlockSpec((MEMKV_TM, D_MODEL), lambda i: (i, 0)),
            pl.BlockSpec((1, D_MODEL), lambda i: (0, 0)),
            pl.BlockSpec((D_MODEL, 2 * width), lambda i: (0, 0)),
            pl.BlockSpec((1, MEM_HEAD_DIM), lambda i: (0, 0)),
        ],
        out_specs=[pl.BlockSpec((MEMKV_TM, width), lambda i: (i, 0))] * 2,
        out_shape=[jax.ShapeDtypeStruct((rows, width), BF16)] * 2,
        compiler_params=_params("parallel"),
        name="mem_kv",
    )(mem2, gain, w_kv, k_gain)


def _mem_attn_kernel(q_ref, k_ref, v_ref, qg_ref, out_ref):
    for h in range(MEM_HEADS):
        cols = slice(h * MEM_HEAD_DIM, (h + 1) * MEM_HEAD_DIM)
        qn = _rms_rows(q_ref[:, cols].astype(F32), qg_ref[...]) * (MEM_HEAD_DIM ** -0.5)
        s = lax.dot_general(qn.astype(BF16), k_ref[:, cols], NT_DIMS, preferred_element_type=F32)
        p = jnp.exp(s - jnp.max(s, axis=-1, keepdims=True))
        denom = jnp.sum(p, axis=-1, keepdims=True)
        o = jnp.dot(p.astype(BF16), v_ref[:, cols], preferred_element_type=F32)
        out_ref[:, cols] = (o / denom).astype(BF16)


def _mem_attention(proj, mk, mv, q_gain, batch, seq, mem_len):
    t = batch * seq
    nq = seq // MEMATT_TQ
    width = MEM_HEADS * MEM_HEAD_DIM
    return pl.pallas_call(
        _mem_attn_kernel,
        grid=(batch, nq),
        in_specs=[
            pl.BlockSpec((MEMATT_TQ, width), lambda b, i: (b * nq + i, 7)),
            pl.BlockSpec((mem_len, width), lambda b, i: (b, 0)),
            pl.BlockSpec((mem_len, width), lambda b, i: (b, 0)),
            pl.BlockSpec((1, MEM_HEAD_DIM), lambda b, i: (0, 0)),
        ],
        out_specs=pl.BlockSpec((MEMATT_TQ, width), lambda b, i: (b * nq + i, 0)),
        out_shape=jax.ShapeDtypeStruct((t, width), BF16),
        compiler_params=_params("parallel", "parallel"),
        name="mem_attention",
    )(proj, mk, mv, q_gain)


MIX_TM = 512
ROUTER_ROWS = 32
PAIRS = ((0, 1), (0, 2), (0, 3), (1, 2), (1, 3), (2, 3))
PAIRS_PER_GROUP = len(PAIRS)
N_CLASSES = N_GROUPS * PAIRS_PER_GROUP
ROW_WIDTH = D_MODEL + LANES


def _top2_sum(b0, b1, b2, b3):
    p, q = jnp.maximum(b0, b1), jnp.minimum(b0, b1)
    r, s = jnp.maximum(b2, b3), jnp.minimum(b2, b3)
    return jnp.maximum(p, r) + jnp.maximum(jnp.minimum(p, r), jnp.maximum(q, s))


def _mix_kernel(ro_ref, fo_ref, mo_ref, gr_ref, gf_ref, gm_ref, x_ref,
                wr_ref, wf_ref, wm_ref, wo_ref, nf_ref, rw_ref, rb_ref,
                xa_ref, cls_ref):
    def branch(a_ref, w_ref, g_ref):
        y = jnp.dot(a_ref[...], w_ref[...], preferred_element_type=F32)
        return jax.nn.sigmoid(g_ref[...].astype(F32)) * y

    merged = (branch(ro_ref, wr_ref, gr_ref) + branch(fo_ref, wf_ref, gf_ref)
              + branch(mo_ref, wm_ref, gm_ref))
    xn = x_ref[...] + jnp.dot(merged.astype(BF16), wo_ref[...], preferred_element_type=F32)
    xa_ref[:, :D_MODEL] = xn
    h = _rms_rows(xn, nf_ref[...])
    h_hi = h.astype(BF16)

    h_lo = (h - h_hi.astype(F32)).astype(BF16)
    rw = rw_ref[...]
    rw_hi = rw.astype(BF16)
    rw_lo = (rw - rw_hi.astype(F32)).astype(BF16)
    dot_nt = lambda a, b: lax.dot_general(a, b, NT_DIMS, preferred_element_type=F32)
    logits = dot_nt(rw_hi, h_hi) + dot_nt(rw_hi, h_lo) + dot_nt(rw_lo, h_hi)
    scores = jax.nn.sigmoid(logits)
    biased = scores + rb_ref[...]
    n = EXPERTS_PER_GROUP
    sc = [scores[SUBLANES * j:SUBLANES * (j + 1)] for j in range(n)]
    bi = [biased[SUBLANES * j:SUBLANES * (j + 1)] for j in range(n)]
    group_score = _top2_sum(*bi)
    rows = lax.broadcasted_iota(jnp.int32, group_score.shape, 0)
    best = jnp.max(group_score, axis=0, keepdims=True)
    top_group = jnp.min(jnp.where(group_score == best, rows, SUBLANES), axis=0, keepdims=True)
    in_group = rows == top_group
    picked, chosen = [], []
    for j in range(n):
        rank = jnp.zeros(group_score.shape, jnp.int32)
        for i in range(n):
            if i == j:
                continue
            ahead = (bi[i] > bi[j]) | ((bi[i] == bi[j]) & (i < j))
            rank = rank + ahead.astype(jnp.int32)
        sel = in_group & (rank < 2)
        picked.append(jnp.where(sel, sc[j], 0.0))
        chosen.append(jnp.max(jnp.where(sel, 1.0, 0.0), axis=0, keepdims=True) > 0.5)
    denom = jnp.sum(picked[0] + picked[1] + picked[2] + picked[3], axis=0, keepdims=True)
    gate = [jnp.sum(picked[j], axis=0, keepdims=True) / denom for j in range(n)]

    first = jnp.full(top_group.shape, n, jnp.int32)
    second = jnp.full(top_group.shape, -1, jnp.int32)
    for j in range(n):
        first = jnp.minimum(first, jnp.where(chosen[j], j, n))
        second = jnp.maximum(second, jnp.where(chosen[j], j, -1))
    pair_base = jnp.where(first == 0, 0, jnp.where(first == 1, 3, 5))
    cls = top_group * PAIRS_PER_GROUP + pair_base + second - first - 1
    cls_ref[...] = jnp.clip(cls, 0, N_CLASSES - 1)
    g_first = sum(jnp.where(first == j, gate[j], 0.0) for j in range(n))
    g_second = sum(jnp.where(second == j, gate[j], 0.0) for j in range(n))
    srow = lax.broadcasted_iota(jnp.int32, (SUBLANES, 1), 0)
    gates8 = jnp.where(srow == 0, g_first, jnp.where(srow == 1, g_second, 0.0))
    pad = jnp.zeros((LANES - SUBLANES, gates8.shape[1]), F32)
    xa_ref[:, D_MODEL:] = jnp.concatenate([gates8, pad], axis=0).T


def _mix_and_route(ro, fo, mo, proj, x2, w_r, w_f, w_m, w_o, norm_ffn, rw_pad, rb_pad):
    t = x2.shape[0]
    tok = lambda i: (i, 0)
    const = lambda i: (0, 0)
    act = pl.BlockSpec((MIX_TM, D_MODEL), tok)
    wspec = pl.BlockSpec((D_MODEL, D_MODEL), const)
    return pl.pallas_call(
        _mix_kernel,
        grid=(t // MIX_TM,),
        in_specs=[
            act, act, act,
            pl.BlockSpec((MIX_TM, D_MODEL), lambda i: (i, 8)),
            pl.BlockSpec((MIX_TM, D_MODEL), lambda i: (i, 9)),
            pl.BlockSpec((MIX_TM, D_MODEL), lambda i: (i, 10)),
            act,
            wspec, wspec, wspec, wspec,
            pl.BlockSpec((1, D_MODEL), const),
            pl.BlockSpec((ROUTER_ROWS, D_MODEL), const),
            pl.BlockSpec((ROUTER_ROWS, 1), const),
        ],
        out_specs=[
            pl.BlockSpec((MIX_TM, ROW_WIDTH), tok),
            pl.BlockSpec((1, MIX_TM), lambda i: (0, i)),
        ],
        out_shape=[
            jax.ShapeDtypeStruct((t, ROW_WIDTH), F32),
            jax.ShapeDtypeStruct((1, t), jnp.int32),
        ],
        compiler_params=_params("parallel"),
        name="mix_and_route",
    )(ro, fo, mo, proj, proj, proj, x2, w_r, w_f, w_m, w_o, norm_ffn, rw_pad, rb_pad)


EXP_TR = 512
POS_TM = 2048
POS_SUB = 512
MOVE_TM = 1024
CLASS_ROWS = 32


def _position_kernel(cls_ref, pos_ref, offs_ref, size_ref, counts_ref, running_ref):
    phase = pl.program_id(0)
    i = pl.program_id(1)

    def class_onehot(cls):
        rows = lax.broadcasted_iota(jnp.int32, (CLASS_ROWS, cls.shape[1]), 0)
        return jnp.where(rows == cls, 1.0, 0.0)

    @pl.when(phase == 0)
    def _count():
        @pl.when(i == 0)
        def _():
            counts_ref[...] = jnp.zeros_like(counts_ref)
        counts_ref[...] += jnp.sum(class_onehot(cls_ref[...]), axis=1, keepdims=True)
        pos_ref[...] = jnp.zeros_like(pos_ref)

    @pl.when(phase == 1)
    def _place():
        @pl.when(i == 0)
        def _():
            padded = jnp.floor((counts_ref[...] + (EXP_TR - 1)) * (1.0 / EXP_TR)) * EXP_TR
            r = lax.broadcasted_iota(jnp.int32, (CLASS_ROWS, CLASS_ROWS), 0)
            c = lax.broadcasted_iota(jnp.int32, (CLASS_ROWS, CLASS_ROWS), 1)
            below = jnp.where(c < r, 1.0, 0.0)
            offs = jnp.dot(below, padded, preferred_element_type=F32,
                           precision=lax.Precision.HIGHEST)
            offs_ref[...] = offs
            size_ref[...] = padded
            running_ref[...] = offs

        r = lax.broadcasted_iota(jnp.int32, (POS_SUB, POS_SUB), 0)
        c = lax.broadcasted_iota(jnp.int32, (POS_SUB, POS_SUB), 1)
        earlier = jnp.where(r < c, 1.0, 0.0).astype(BF16)
        running = running_ref[:, 0:1]
        for u in range(POS_TM // POS_SUB):
            cols = slice(u * POS_SUB, (u + 1) * POS_SUB)
            onehot = class_onehot(cls_ref[:, cols])
            prefix = jnp.dot(onehot.astype(BF16), earlier, preferred_element_type=F32)
            pos_ref[:, cols] = jnp.sum(onehot * (prefix + running), axis=0,
                                       keepdims=True).astype(jnp.int32)
            running = running + jnp.sum(onehot, axis=1, keepdims=True)
        running_ref[...] = jnp.broadcast_to(running, running_ref.shape)


def _positions(cls):
    t = cls.shape[1]
    meta = jax.ShapeDtypeStruct((CLASS_ROWS, LANES), F32)
    return pl.pallas_call(
        _position_kernel,
        grid=(2, t // POS_TM),
        in_specs=[pl.BlockSpec((1, POS_TM), lambda p, i: (0, i))],
        out_specs=[
            pl.BlockSpec((1, POS_TM), lambda p, i: (0, i * p)),
            pl.BlockSpec((CLASS_ROWS, LANES), lambda p, i: (0, 0)),
            pl.BlockSpec((CLASS_ROWS, LANES), lambda p, i: (0, 0)),
        ],
        out_shape=[jax.ShapeDtypeStruct((1, t), jnp.int32), meta, meta],
        scratch_shapes=[pltpu.VMEM((CLASS_ROWS, LANES), F32), pltpu.VMEM((CLASS_ROWS, LANES), F32)],
        compiler_params=_params("arbitrary", "arbitrary"),
        name="moe_positions",
    )(cls)


def _move_rows(copy_for_row):
    def start(group, carry):
        for sub in range(SUBLANES):
            copy_for_row(group, sub).start(priority=sub % 2)
        return carry

    def wait(group, carry):
        for sub in range(SUBLANES):
            copy_for_row(group, sub).wait()
        return carry

    lax.fori_loop(0, MOVE_TM // SUBLANES, start, 0)
    lax.fori_loop(0, MOVE_TM // SUBLANES, wait, 0)


def _scatter_rows_kernel(pos_ref, src_ref, init_ref, dst_ref, sem):
    del init_ref
    _move_rows(lambda group, sub: pltpu.make_async_copy(
        src_ref.at[group, pl.ds(sub, 1)],
        dst_ref.at[pl.ds(pos_ref[group * SUBLANES + sub], 1)], sem))


def _scatter_rows(pos, src, init):
    t, width = src.shape
    return pl.pallas_call(
        _scatter_rows_kernel,
        grid=(t // MOVE_TM,),
        in_specs=[
            pl.BlockSpec((MOVE_TM,), lambda i: (i,), memory_space=pltpu.SMEM),
            pl.BlockSpec((MOVE_TM // SUBLANES, SUBLANES, width), lambda i: (i, 0, 0)),
            pl.BlockSpec(memory_space=pl.ANY),
        ],
        out_specs=pl.BlockSpec(memory_space=pl.ANY),
        out_shape=jax.ShapeDtypeStruct(init.shape, init.dtype),
        scratch_shapes=[pltpu.SemaphoreType.DMA(())],
        input_output_aliases={2: 0},
        compiler_params=_params("arbitrary"),
        name="moe_scatter_rows",
    )(pos, src.reshape(t // SUBLANES, SUBLANES, width), init)


def _gather_rows_kernel(pos_ref, src_ref, dst_ref, sem):
    _move_rows(lambda group, sub: pltpu.make_async_copy(
        src_ref.at[pl.ds(pos_ref[group * SUBLANES + sub], 1)],
        dst_ref.at[group, pl.ds(sub, 1)], sem))


def _gather_rows(pos, src):
    t = pos.shape[0]
    width = src.shape[1]
    out = pl.pallas_call(
        _gather_rows_kernel,
        grid=(t // MOVE_TM,),
        in_specs=[
            pl.BlockSpec((MOVE_TM,), lambda i: (i,), memory_space=pltpu.SMEM),
            pl.BlockSpec(memory_space=pl.ANY),
        ],
        out_specs=pl.BlockSpec((MOVE_TM // SUBLANES, SUBLANES, width), lambda i: (i, 0, 0)),
        out_shape=jax.ShapeDtypeStruct((t // SUBLANES, SUBLANES, width), src.dtype),
        scratch_shapes=[pltpu.SemaphoreType.DMA(())],
        compiler_params=_params("arbitrary"),
        name="moe_gather_rows",
    )(pos, src)
    return out.reshape(t, width)


def _experts_kernel(ea_ref, eb_ref, used_ref, xs_ref, nf_ref,
                    wga_ref, wua_ref, wda_ref, wgb_ref, wub_ref, wdb_ref, ys_ref):
    del ea_ref, eb_ref
    k = pl.program_id(0)

    @pl.when(k < used_ref[0])
    def _():
        x = xs_ref[:, :D_MODEL]
        gates = xs_ref[:, D_MODEL:]
        h = _rms_rows(x, nf_ref[...]).astype(BF16)

        def ffn(wg_ref, wu_ref, wd_ref):
            a = jnp.dot(h, wg_ref[...], preferred_element_type=F32)
            u = jnp.dot(h, wu_ref[...], preferred_element_type=F32)
            return jnp.dot((a * jax.nn.sigmoid(a) * u).astype(BF16), wd_ref[...],
                           preferred_element_type=F32)

        moe = (gates[:, 0:1] * ffn(wga_ref, wua_ref, wda_ref)
               + gates[:, 1:2] * ffn(wgb_ref, wub_ref, wdb_ref))
        ys_ref[...] = x + moe

    @pl.when(k >= used_ref[0])
    def _():
        ys_ref[...] = jnp.zeros_like(ys_ref)


def _experts(xs, norm_ffn, w_gate, w_up, w_down, expert_a, expert_b, n_used):
    n_tiles = xs.shape[0] // EXP_TR
    up = lambda sel: pl.BlockSpec((None, D_MODEL, D_FF), lambda k, ea, eb, nu: (sel(ea, eb)[k], 0, 0))
    down = lambda sel: pl.BlockSpec((None, D_FF, D_MODEL), lambda k, ea, eb, nu: (sel(ea, eb)[k], 0, 0))
    first = lambda ea, eb: ea
    second = lambda ea, eb: eb
    grid_spec = pltpu.PrefetchScalarGridSpec(
        num_scalar_prefetch=3,
        grid=(n_tiles,),
        in_specs=[
            pl.BlockSpec((EXP_TR, ROW_WIDTH), lambda k, ea, eb, nu: (k, 0)),
            pl.BlockSpec((1, D_MODEL), lambda k, ea, eb, nu: (0, 0)),
            up(first), up(first), down(first), up(second), up(second), down(second),
        ],
        out_specs=pl.BlockSpec((EXP_TR, D_MODEL), lambda k, ea, eb, nu: (k, 0)),
    )
    return pl.pallas_call(
        _experts_kernel,
        grid_spec=grid_spec,
        out_shape=jax.ShapeDtypeStruct((xs.shape[0], D_MODEL), F32),
        compiler_params=_params("arbitrary"),
        name="experts",
    )(expert_a, expert_b, n_used, xs, norm_ffn, w_gate, w_up, w_down, w_gate, w_up, w_down)


def _tile_experts(offs, sizes, n_tiles):
    ends = (offs[:N_CLASSES, 0] + sizes[:N_CLASSES, 0]).astype(jnp.int32)
    starts = jnp.arange(n_tiles, dtype=jnp.int32) * EXP_TR
    tile_cls = jnp.minimum(jnp.sum(ends[None, :] <= starts[:, None], axis=1), N_CLASSES - 1)
    group = tile_cls // PAIRS_PER_GROUP
    pair = tile_cls % PAIRS_PER_GROUP
    slot_a = jnp.array([a for a, _ in PAIRS], jnp.int32)[pair]
    slot_b = jnp.array([b for _, b in PAIRS], jnp.int32)[pair]
    expert_a = (group * EXPERTS_PER_GROUP + slot_a).astype(jnp.int32)
    expert_b = (group * EXPERTS_PER_GROUP + slot_b).astype(jnp.int32)
    return expert_a, expert_b, (ends[-1] // EXP_TR).reshape(1)


def _router_layout(router_w, router_bias):
    w = router_w.T.reshape(N_GROUPS, EXPERTS_PER_GROUP, D_MODEL)
    w = jnp.transpose(w, (1, 0, 2))
    w = jnp.pad(w, ((0, 0), (0, SUBLANES - N_GROUPS), (0, 0))).reshape(ROUTER_ROWS, D_MODEL)
    b = jnp.transpose(router_bias.reshape(N_GROUPS, EXPERTS_PER_GROUP), (1, 0))
    b = jnp.pad(b, ((0, 0), (0, SUBLANES - N_GROUPS)), constant_values=NEG_BIG)
    return w.astype(F32), b.reshape(ROUTER_ROWS, 1).astype(F32)


def kernel(x, mem, norm_mix, norm_mem, w_in, b_forget, fox_q_norm, fox_k_norm, mem_q_norm,
           mem_k_norm, w_mem_kv, w_o_ret, w_o_fox, w_o_mem, w_out, norm_ffn, router_w,
           router_bias, w_gate, w_up, w_down):
    batch, seq, d = x.shape
    mem_len = mem.shape[1]
    depth = w_in.shape[0]
    t = batch * seq
    ff_lo = 7 * D_MODEL
    ff_hi = ff_lo + FOX_HEADS

    cos, sin, intra, qd, kd, cd = _retention_tables(seq)
    rw_pad, rb_pad = _router_layout(router_w, router_bias)
    row = lambda v: v.reshape(1, -1).astype(F32)

    x2 = x.reshape(t, d)
    mem2 = mem.reshape(batch * mem_len, d)
    n_tiles = t // EXP_TR + N_CLASSES
    sorted_rows = jnp.zeros((n_tiles * EXP_TR, ROW_WIDTH), F32)
    for l in range(depth):
        w_main = jnp.concatenate([w_in[l, :, :ff_lo], w_in[l, :, ff_hi:]], axis=1).astype(BF16)
        w_ff = jnp.pad(w_in[l, :, ff_lo:ff_hi], ((0, 0), (0, LANES - FOX_HEADS))).astype(BF16)
        b_pad = jnp.pad(b_forget[l], (0, LANES - FOX_HEADS)).reshape(1, LANES)

        proj, ff = _in_projection(x2, row(norm_mix[l]), w_main, w_ff)
        cum, cum_split = _forget_cumsum(ff, b_pad, batch, seq)
        ro = _retention(proj, cos, sin, intra, qd, kd, cd, batch, seq)
        bound = (FOX_BOUND_SCALE * jnp.max(jnp.abs(fox_q_norm[l]))
                 * jnp.max(jnp.abs(fox_k_norm[l])))
        first = _fox_first_blocks(cum, bound, batch, seq)
        fo = _fox_attention(proj, cum_split, bound.reshape(1).astype(F32), first,
                            row(jnp.tile(fox_q_norm[l], 2)), row(jnp.tile(fox_k_norm[l], 2)),
                            batch, seq)
        mk, mv = _mem_kv(mem2, row(norm_mem[l]), w_mem_kv[l].astype(BF16), row(mem_k_norm[l]))
        mo = _mem_attention(proj, mk, mv, row(mem_q_norm[l]), batch, seq, mem_len)
        xa, cls = _mix_and_route(
            ro, fo, mo, proj, x2, w_o_ret[l].astype(BF16), w_o_fox[l].astype(BF16),
            w_o_mem[l].astype(BF16), w_out[l].astype(BF16), row(norm_ffn[l]), rw_pad, rb_pad)
        pos, offs, sizes = _positions(cls)
        pos = pos.reshape(t)
        expert_a, expert_b, n_used = _tile_experts(offs, sizes, n_tiles)
        sorted_rows = _scatter_rows(pos, xa, sorted_rows)
        ys = _experts(sorted_rows, row(norm_ffn[l]), w_gate[l].astype(BF16), w_up[l].astype(BF16),
                      w_down[l].astype(BF16), expert_a, expert_b, n_used)
        x2 = _gather_rows(pos, ys)
    return x2.reshape(batch, seq, d)
```

```python
import jax
import jax.numpy as jnp
from jax import lax
from jax.experimental import pallas as pl
from jax.experimental.pallas import tpu as pltpu

F32 = jnp.float32
BF16 = jnp.bfloat16

D_MODEL = 1024
EPS = 1e-6
RET_HEADS = 4
RET_HEAD_DIM = 256
RET_CHUNK = 128
ROPE_BASE = 10000.0
FOX_HEADS = 16
FOX_HEAD_DIM = 64
MEM_HEADS = 4
MEM_HEAD_DIM = 256
N_GROUPS = 4
EXPERTS_PER_GROUP = 4
D_FF = 512

LANES = 128
SUBLANES = 8
VMEM_LIMIT = 56 * 1024 * 1024
NEG_BIG = -1e30

NT_DIMS = (((1,), (1,)), ((), ()))


def _params(*sem):
    return pltpu.CompilerParams(dimension_semantics=sem, vmem_limit_bytes=VMEM_LIMIT)


def _rms_rows(x, gain_row):
    ms = jnp.mean(x * x, axis=-1, keepdims=True)
    return x * lax.rsqrt(ms + EPS) * gain_row


IN_TM = 1024
IN_BLOCKS = 11
IN_TN = 2816


def _inproj_kernel(x_ref, g_ref, w_ref, wff_ref, out_ref, ff_ref, hn_ref):
    @pl.when(pl.program_id(1) == 0)
    def _():
        hb = _rms_rows(x_ref[...], g_ref[...]).astype(BF16)
        hn_ref[...] = hb
        ff_ref[...] = jnp.dot(hb, wff_ref[...], preferred_element_type=F32)

    out_ref[...] = jnp.dot(hn_ref[...], w_ref[...], preferred_element_type=F32).astype(BF16)


def _in_projection(x2, gain, w_main, w_ff):
    t = x2.shape[0]
    return pl.pallas_call(
        _inproj_kernel,
        grid=(t // IN_TM, IN_BLOCKS * D_MODEL // IN_TN),
        in_specs=[
            pl.BlockSpec((IN_TM, D_MODEL), lambda i, j: (i, 0)),
            pl.BlockSpec((1, D_MODEL), lambda i, j: (0, 0)),
            pl.BlockSpec((D_MODEL, IN_TN), lambda i, j: (0, j)),
            pl.BlockSpec((D_MODEL, LANES), lambda i, j: (0, 0)),
        ],
        out_specs=[
            pl.BlockSpec((IN_TM, IN_TN), lambda i, j: (i, j)),
            pl.BlockSpec((IN_TM, LANES), lambda i, j: (i, 0)),
        ],
        out_shape=[
            jax.ShapeDtypeStruct((t, IN_BLOCKS * D_MODEL), BF16),
            jax.ShapeDtypeStruct((t, LANES), F32),
        ],
        scratch_shapes=[pltpu.VMEM((IN_TM, D_MODEL), BF16)],
        compiler_params=_params("parallel", "arbitrary"),
        name="in_projection",
    )(x2, gain, w_main, w_ff)


CUM_BLOCK = 512
AUG_SPLIT = 3


def _split3(x):
    hi = x.astype(BF16)
    r1 = x - hi.astype(F32)
    mid = r1.astype(BF16)
    lo = (r1 - mid.astype(F32)).astype(BF16)
    return hi, mid, lo


def _cumsum_kernel(ff_ref, b_ref, out_ref, split_ref):
    s = ff_ref.shape[0]
    r = lax.broadcasted_iota(jnp.int32, (CUM_BLOCK, CUM_BLOCK), 0)
    c = lax.broadcasted_iota(jnp.int32, (CUM_BLOCK, CUM_BLOCK), 1)
    lower = (c <= r).astype(BF16)
    carry = jnp.zeros((1, LANES), F32)
    for blk in range(s // CUM_BLOCK):
        rows = slice(blk * CUM_BLOCK, (blk + 1) * CUM_BLOCK)
        z = ff_ref[rows, :] + b_ref[...]
        log_f = jnp.minimum(z, 0.0) - jnp.log1p(jnp.exp(-jnp.abs(z)))
        cs = sum(jnp.dot(lower, part, preferred_element_type=F32) for part in _split3(log_f)) + carry
        out_ref[rows, :] = cs
        for k, part in enumerate(_split3(cs)):
            split_ref[k, rows, :] = part
        carry = cs[CUM_BLOCK - 1:CUM_BLOCK, :]


def _forget_cumsum(ff, b_pad, batch, seq):
    return pl.pallas_call(
        _cumsum_kernel,
        grid=(batch,),
        in_specs=[
            pl.BlockSpec((seq, LANES), lambda b: (b, 0)),
            pl.BlockSpec((1, LANES), lambda b: (0, 0)),
        ],
        out_specs=[
            pl.BlockSpec((seq, LANES), lambda b: (b, 0)),
            pl.BlockSpec((AUG_SPLIT, seq, LANES), lambda b: (0, b, 0)),
        ],
        out_shape=[
            jax.ShapeDtypeStruct((batch * seq, LANES), F32),
            jax.ShapeDtypeStruct((AUG_SPLIT, batch * seq, LANES), BF16),
        ],
        compiler_params=_params("parallel"),
        name="forget_cumsum",
    )(ff, b_pad)


RET_ROWS = 1024
RET_GROUP = 4


def _retention_kernel(q_ref, k_ref, v_ref, g_ref, cos_ref, sin_ref, intra_ref, qd_ref, kd_ref,
                      cd_ref, out_ref, state_ref):
    hd = RET_HEAD_DIM
    half = hd // 2

    @pl.when(pl.program_id(2) == 0)
    def _():
        state_ref[...] = jnp.zeros_like(state_ref)

    def chunk(c, carry):
        r0 = pl.multiple_of(c * RET_CHUNK, RET_CHUNK)
        rows = pl.ds(r0, RET_CHUNK)
        cs = cos_ref[rows, :]
        sn = sin_ref[rows, :]

        def rot(x):
            x1 = x[:, :half]
            x2 = x[:, half:]
            return jnp.concatenate([x1 * cs - x2 * sn, x2 * cs + x1 * sn], axis=-1)

        for h in range(RET_GROUP):
            cols = slice(h * hd, (h + 1) * hd)
            qr = rot(q_ref[rows, cols].astype(F32))
            kr = rot(k_ref[rows, cols].astype(F32))
            v = v_ref[rows, cols]
            scores = lax.dot_general(qr.astype(BF16), kr.astype(BF16), NT_DIMS,
                                     preferred_element_type=F32) * intra_ref[h]
            st = state_ref[h]
            o = (jnp.dot(scores.astype(BF16), v, preferred_element_type=F32)
                 + jnp.dot((qr * qd_ref[h]).astype(BF16), st.astype(BF16),
                           preferred_element_type=F32))
            kd_t = (kr * kd_ref[h]).T.astype(BF16)
            state_ref[h] = st * cd_ref[h] + jnp.dot(kd_t, v, preferred_element_type=F32)
            on = o * lax.rsqrt(jnp.mean(o * o, axis=-1, keepdims=True) + EPS)
            g = g_ref[rows, cols].astype(F32)
            out_ref[rows, cols] = (on * (g * jax.nn.sigmoid(g))).astype(BF16)
        return carry

    lax.fori_loop(0, RET_ROWS // RET_CHUNK, chunk, 0)


def _retention(proj, cos, sin, intra, qd, kd, cd, batch, seq):
    t = batch * seq
    hd = RET_HEAD_DIM
    width = RET_GROUP * hd
    groups = RET_HEADS // RET_GROUP
    steps = seq // RET_ROWS
    col = lambda off: (lambda b, p, s: (b * steps + s, off + p))
    tab = lambda b, p, s: (p, 0, 0)
    angle = pl.BlockSpec((RET_ROWS, hd // 2), lambda b, p, s: (s, 0))
    return pl.pallas_call(
        _retention_kernel,
        grid=(batch, groups, steps),
        in_specs=[
            pl.BlockSpec((RET_ROWS, width), col(0)),
            pl.BlockSpec((RET_ROWS, width), col(groups)),
            pl.BlockSpec((RET_ROWS, width), col(2 * groups)),
            pl.BlockSpec((RET_ROWS, width), col(3 * groups)),
            angle, angle,
            pl.BlockSpec((RET_GROUP, RET_CHUNK, RET_CHUNK), tab),
            pl.BlockSpec((RET_GROUP, RET_CHUNK, hd), tab),
            pl.BlockSpec((RET_GROUP, RET_CHUNK, hd), tab),
            pl.BlockSpec((RET_GROUP, 1, hd), tab),
        ],
        out_specs=pl.BlockSpec((RET_ROWS, width), lambda b, p, s: (b * steps + s, p)),
        out_shape=jax.ShapeDtypeStruct((t, RET_HEADS * hd), BF16),
        scratch_shapes=[pltpu.VMEM((RET_GROUP, hd, hd), F32)],
        compiler_params=_params("parallel", "parallel", "arbitrary"),
        name="retention",
    )(proj, proj, proj, proj, cos, sin, intra, qd, kd, cd)


def _retention_tables(seq):
    h = jnp.arange(RET_HEADS, dtype=F32)
    log_gamma = jnp.log1p(-(2.0 ** (-5.0 - h)))
    idx = jnp.arange(RET_CHUNK, dtype=F32)
    diff = idx[:, None] - idx[None, :]
    scale = RET_HEAD_DIM ** -0.5
    intra = jnp.where(diff >= 0, jnp.exp(log_gamma[:, None, None] * jnp.maximum(diff, 0.0)), 0.0) * scale
    q_decay = jnp.exp(log_gamma[:, None] * (idx + 1.0))
    k_decay = jnp.exp(log_gamma[:, None] * (RET_CHUNK - 1.0 - idx)) * scale
    chunk_decay = jnp.exp(log_gamma * RET_CHUNK)
    qd = jnp.broadcast_to(q_decay[:, :, None], (RET_HEADS, RET_CHUNK, RET_HEAD_DIM))
    kd = jnp.broadcast_to(k_decay[:, :, None], (RET_HEADS, RET_CHUNK, RET_HEAD_DIM))
    cd = jnp.broadcast_to(chunk_decay[:, None, None], (RET_HEADS, 1, RET_HEAD_DIM))
    inv_freq = ROPE_BASE ** (-jnp.arange(0, RET_HEAD_DIM, 2, dtype=F32) / RET_HEAD_DIM)
    ang = jnp.arange(seq, dtype=F32)[:, None] * inv_freq[None, :]
    return jnp.cos(ang), jnp.sin(ang), intra, qd, kd, cd


FOX_TQ = 512
FOX_TK = 512
assert FOX_TQ == FOX_TK
FOX_QSTEP = 8


def _pair_rms(x, lo_mask, gain_row):
    x2 = x * x
    s_all = jnp.sum(x2, axis=-1, keepdims=True)
    s_lo = jnp.sum(jnp.where(lo_mask, x2, 0.0), axis=-1, keepdims=True)
    ms = jnp.where(lo_mask, s_lo, s_all - s_lo) * (1.0 / FOX_HEAD_DIM)
    return x * lax.rsqrt(ms + EPS) * gain_row


AUG_CQ = 0
AUG_CK = 3
AUG_M = 6
FOX_EXACT_ABOVE = 30.0
FOX_BOUND_SCALE = 8.0 * 1.02
FOX_UNDERFLOW_LOG = -88.0


def _lanes_in(lane, start, count=AUG_SPLIT):
    return (lane >= start) & (lane < start + count)


def _fox_kernel(bound_ref, first_ref, q_ref, k_ref, v_ref, c_ref, qg_ref, kg_ref, out_ref,
                kaug_ref, caq_ref, vat_ref, qb_ref, qat_ref, acc_ref, m_ref):
    seq = k_ref.shape[0]
    hp = pl.program_id(1)
    step = pl.program_id(2)
    lane = lax.broadcasted_iota(jnp.int32, (1, LANES), 1)
    lo = lane < FOX_HEAD_DIM
    head_lanes = (lo, jnp.logical_not(lo))
    spare = (FOX_HEAD_DIM, 0)
    bound = bound_ref[0]

    @pl.when(step == 0)
    def _prepare_keys_values():
        r = lax.broadcasted_iota(jnp.int32, (LANES, LANES), 0)
        c = lax.broadcasted_iota(jnp.int32, (LANES, LANES), 1)
        place = []
        for k in range(AUG_SPLIT):
            pk = jnp.zeros((LANES, LANES), F32)
            for h in range(2):
                src = r == 2 * hp + h
                pk = pk + jnp.where(src & (c == spare[h] + AUG_CQ + k), 1.0, 0.0)
                pk = pk - jnp.where(src & (c == spare[h] + AUG_CK + k), 1.0, 0.0)
            place.append(pk.astype(BF16))
        bound_parts = [p.astype(F32) for p in _split3(jnp.full((1, LANES), -bound, F32))]

        def prep(i, carry):
            rows = pl.ds(pl.multiple_of(i * FOX_TK, FOX_TK), FOX_TK)
            e = sum(jnp.dot(c_ref[k, rows, :], place[k], preferred_element_type=F32)
                    for k in range(AUG_SPLIT))
            kn = _pair_rms(k_ref[rows, :].astype(F32), lo, kg_ref[...])
            v = v_ref[rows, :]
            one = jnp.ones_like(v)
            for h in range(2):
                ones_k = jnp.where(_lanes_in(lane, spare[h] + AUG_CQ)
                                   | _lanes_in(lane, spare[h] + AUG_M), 1.0, 0.0)
                kaug_ref[h, rows, :] = jnp.where(
                    head_lanes[h], kn,
                    jnp.where(_lanes_in(lane, spare[h] + AUG_CK), e, ones_k)).astype(BF16)
                const_q = jnp.where(_lanes_in(lane, spare[h] + AUG_CK), 1.0, 0.0)
                for k in range(AUG_SPLIT):
                    const_q = jnp.where(lane == spare[h] + AUG_M + k, bound_parts[k], const_q)
                caq_ref[h, rows, :] = jnp.where(
                    _lanes_in(lane, spare[h] + AUG_CQ), e, const_q).astype(BF16)
                va = jnp.where(head_lanes[h], v, one)
                vat_ref[h, i] = va.T
            return carry
        lax.fori_loop(0, seq // FOX_TK, prep, 0)

    def query_block(sub):
        qi = step * FOX_QSTEP + sub
        first = first_ref[(pl.program_id(0) * pl.num_programs(1) + hp)
                          * (pl.num_programs(2) * FOX_QSTEP) + qi]
        local_rows = pl.ds(sub * FOX_TQ, FOX_TQ)
        t0 = pl.multiple_of(qi * FOX_TQ, FOX_TQ)
        q_rows = pl.ds(t0, FOX_TQ)
        qn = _pair_rms(q_ref[local_rows, :].astype(F32), lo, qg_ref[...]) * (FOX_HEAD_DIM ** -0.5)
        kq = lax.broadcasted_iota(jnp.int32, (FOX_TK, FOX_TQ), 0)
        qq = lax.broadcasted_iota(jnp.int32, (FOX_TK, FOX_TQ), 1)

        q_aug = [jnp.where(head_lanes[h], qn, caq_ref[h, q_rows, :].astype(F32)) for h in range(2)]
        for h in range(2):
            qat_ref[h] = q_aug[h].astype(BF16).T

        @pl.when(bound > FOX_EXACT_ABOVE)
        def _exact_row_max():
            for h in range(2):
                qb_ref[h] = jnp.where(_lanes_in(lane, spare[h] + AUG_M), 0.0, q_aug[h]).astype(BF16)
                m_ref[h] = jnp.full((FOX_TQ, 1), NEG_BIG, F32)

            def scan(j, masked):
                keys = pl.ds(pl.multiple_of(j * FOX_TK, FOX_TK), FOX_TK)
                for h in range(2):
                    s = lax.dot_general(qb_ref[h], kaug_ref[h, keys, :], NT_DIMS,
                                        preferred_element_type=F32)
                    if masked:
                        s = jnp.where(qq <= kq, s, NEG_BIG)
                    m_ref[h] = jnp.maximum(m_ref[h], jnp.max(s, axis=-1, keepdims=True))

            def scan_body(j, carry):
                scan(j, False)
                return carry
            lax.fori_loop(first, qi, scan_body, 0)
            scan(qi, True)
            for h in range(2):
                m_parts = _split3(-m_ref[h])
                qa = q_aug[h]
                for k in range(AUG_SPLIT):
                    qa = jnp.where(lane == spare[h] + AUG_M + k, m_parts[k].astype(F32), qa)
                qat_ref[h] = qa.T.astype(BF16)

        acc_ref[...] = jnp.zeros_like(acc_ref)

        def probs(h, keys, q_cols=slice(None)):
            st = jnp.dot(kaug_ref[h, keys, :], qat_ref[h, :, q_cols], preferred_element_type=F32)
            return jnp.exp(st)

        def kv_blocks(blocks):
            for h in range(2):
                total = None
                for j in blocks:
                    keys = pl.ds(pl.multiple_of(j * FOX_TK, FOX_TK), FOX_TK)
                    o = jnp.dot(vat_ref[h, j], probs(h, keys).astype(BF16),
                                preferred_element_type=F32)
                    total = o if total is None else total + o
                acc_ref[h] += total

        def pair(i, carry):
            kv_blocks((first + 2 * i, first + 2 * i + 1))
            return carry

        full_blocks = qi - first
        lax.fori_loop(0, lax.shift_right_logical(full_blocks, 1), pair, 0)

        @pl.when((full_blocks & 1) == 1)
        def _odd_block():
            kv_blocks((qi - 1,))

        half = FOX_TK // 2
        lower, upper = slice(0, half), slice(half, FOX_TK)
        causal_a = (lax.broadcasted_iota(jnp.int32, (half, FOX_TQ), 0)
                    <= lax.broadcasted_iota(jnp.int32, (half, FOX_TQ), 1))
        causal_b = (lax.broadcasted_iota(jnp.int32, (half, half), 0)
                    <= lax.broadcasted_iota(jnp.int32, (half, half), 1))
        for h in range(2):
            p_a = jnp.where(causal_a, probs(h, pl.ds(t0, half)), 0.0)
            p_b = jnp.where(causal_b, probs(h, pl.ds(t0 + half, half), upper), 0.0)
            acc_ref[h] += jnp.dot(vat_ref[h, qi, :, lower], p_a.astype(BF16),
                                  preferred_element_type=F32)
            acc_ref[h, :, upper] += jnp.dot(vat_ref[h, qi, :, upper], p_b.astype(BF16),
                                            preferred_element_type=F32)

        o0 = acc_ref[0]
        o1 = acc_ref[1]
        l0 = o0[FOX_HEAD_DIM:FOX_HEAD_DIM + 1, :]
        l1 = o1[0:1, :]
        row = lax.broadcasted_iota(jnp.int32, (LANES, 1), 0)
        out_ref[local_rows, :] = jnp.where(row < FOX_HEAD_DIM, o0 / l0, o1 / l1).T.astype(BF16)

    for sub in range(FOX_QSTEP):
        query_block(sub)


def _fox_first_blocks(cum, bound, batch, seq):
    blocks = seq // FOX_TK
    c = cum.reshape(batch, seq, LANES)[:, :, :FOX_HEADS]
    c_start = c[:, 0::FOX_TQ, :]
    c_end = c[:, FOX_TK - 1::FOX_TK, :]
    negligible = (c_start[:, :, None, :] - c_end[:, None, :, :] + 2.0 * bound) < FOX_UNDERFLOW_LOG
    idx = jnp.arange(blocks)
    earlier = (idx[None, :] < idx[:, None])[None, :, :, None]
    count = jnp.sum(negligible & earlier, axis=2)
    first = jnp.min(count.reshape(batch, blocks, FOX_HEADS // 2, 2), axis=-1)
    return jnp.transpose(first, (0, 2, 1)).reshape(-1).astype(jnp.int32)


def _fox_attention(proj, cum_split, bound, first, qg, kg, batch, seq):
    t = batch * seq
    nq = seq // (FOX_QSTEP * FOX_TQ)
    pairs = FOX_HEADS // 2
    base = 4 * (D_MODEL // LANES)
    return pl.pallas_call(
        _fox_kernel,
        grid=(batch, pairs, nq),
        in_specs=[
            pl.BlockSpec(memory_space=pltpu.SMEM),
            pl.BlockSpec(memory_space=pltpu.SMEM),
            pl.BlockSpec((FOX_QSTEP * FOX_TQ, LANES), lambda b, p, i: (b * nq + i, base + p)),
            pl.BlockSpec((seq, LANES), lambda b, p, i: (b, base + pairs + p)),
            pl.BlockSpec((seq, LANES), lambda b, p, i: (b, base + 2 * pairs + p)),
            pl.BlockSpec((AUG_SPLIT, seq, LANES), lambda b, p, i: (0, b, 0)),
            pl.BlockSpec((1, LANES), lambda b, p, i: (0, 0)),
            pl.BlockSpec((1, LANES), lambda b, p, i: (0, 0)),
        ],
        out_specs=pl.BlockSpec((FOX_QSTEP * FOX_TQ, LANES), lambda b, p, i: (b * nq + i, p)),
        out_shape=jax.ShapeDtypeStruct((t, FOX_HEADS * FOX_HEAD_DIM), BF16),
        scratch_shapes=[
            pltpu.VMEM((2, seq, LANES), BF16),
            pltpu.VMEM((2, seq, LANES), BF16),
            pltpu.VMEM((2, seq // FOX_TK, LANES, FOX_TK), BF16),
            pltpu.VMEM((2, FOX_TQ, LANES), BF16),
            pltpu.VMEM((2, LANES, FOX_TQ), BF16),
            pltpu.VMEM((2, LANES, FOX_TQ), F32),
            pltpu.VMEM((2, FOX_TQ, 1), F32),
        ],
        compiler_params=_params("parallel", "parallel", "arbitrary"),
        name="fox_attention",
    )(bound, first, proj, proj, proj, cum_split, qg, kg)


MEMKV_TM = 512
MEMATT_TQ = 1024


def _mem_kv_kernel(mem_ref, g_ref, w_ref, kg_ref, k_out, v_out):
    hb = _rms_rows(mem_ref[...], g_ref[...]).astype(BF16)
    kv = jnp.dot(hb, w_ref[...], preferred_element_type=F32)
    width = MEM_HEADS * MEM_HEAD_DIM
    for h in range(MEM_HEADS):
        cols = slice(h * MEM_HEAD_DIM, (h + 1) * MEM_HEAD_DIM)
        k_out[:, cols] = _rms_rows(kv[:, cols], kg_ref[...]).astype(BF16)
    v_out[...] = kv[:, width:].astype(BF16)


def _mem_kv(mem2, gain, w_kv, k_gain):
    rows = mem2.shape[0]
    width = MEM_HEADS * MEM_HEAD_DIM
    return pl.pallas_call(
        _mem_kv_kernel,
        grid=(rows // MEMKV_TM,),
        in_specs=[
            pl.BlockSpec((MEMKV_TM, D_MODEL), lambda i: (i, 0)),
            pl.BlockSpec((1, D_MODEL), lambda i: (0, 0)),
            pl.BlockSpec((D_MODEL, 2 * width), lambda i: (0, 0)),
            pl.BlockSpec((1, MEM_HEAD_DIM), lambda i: (0, 0)),
        ],
        out_specs=[pl.BlockSpec((MEMKV_TM, width), lambda i: (i, 0))] * 2,
        out_shape=[jax.ShapeDtypeStruct((rows, width), BF16)] * 2,
        compiler_params=_params("parallel"),
        name="mem_kv",
    )(mem2, gain, w_kv, k_gain)


def _mem_attn_kernel(q_ref, k_ref, v_ref, qg_ref, out_ref):
    for h in range(MEM_HEADS):
        cols = slice(h * MEM_HEAD_DIM, (h + 1) * MEM_HEAD_DIM)
        qn = _rms_rows(q_ref[:, cols].astype(F32), qg_ref[...]) * (MEM_HEAD_DIM ** -0.5)
        s = lax.dot_general(qn.astype(BF16), k_ref[:, cols], NT_DIMS, preferred_element_type=F32)
        p = jnp.exp(s - jnp.max(s, axis=-1, keepdims=True))
        denom = jnp.sum(p, axis=-1, keepdims=True)
        o = jnp.dot(p.astype(BF16), v_ref[:, cols], preferred_element_type=F32)
        out_ref[:, cols] = (o / denom).astype(BF16)


def _mem_attention(proj, mk, mv, q_gain, batch, seq, mem_len):
    t = batch * seq
    nq = seq // MEMATT_TQ
    width = MEM_HEADS * MEM_HEAD_DIM
    return pl.pallas_call(
        _mem_attn_kernel,
        grid=(batch, nq),
        in_specs=[
            pl.BlockSpec((MEMATT_TQ, width), lambda b, i: (b * nq + i, 7)),
            pl.BlockSpec((mem_len, width), lambda b, i: (b, 0)),
            pl.BlockSpec((mem_len, width), lambda b, i: (b, 0)),
            pl.BlockSpec((1, MEM_HEAD_DIM), lambda b, i: (0, 0)),
        ],
        out_specs=pl.BlockSpec((MEMATT_TQ, width), lambda b, i: (b * nq + i, 0)),
        out_shape=jax.ShapeDtypeStruct((t, width), BF16),
        compiler_params=_params("parallel", "parallel"),
        name="mem_attention",
    )(proj, mk, mv, q_gain)


MIX_TM = 512
ROUTER_ROWS = 32
PAIRS = ((0, 1), (0, 2), (0, 3), (1, 2), (1, 3), (2, 3))
PAIRS_PER_GROUP = len(PAIRS)
N_CLASSES = N_GROUPS * PAIRS_PER_GROUP
ROW_WIDTH = D_MODEL + LANES


def _top2_sum(b0, b1, b2, b3):
    p, q = jnp.maximum(b0, b1), jnp.minimum(b0, b1)
    r, s = jnp.maximum(b2, b3), jnp.minimum(b2, b3)
    return jnp.maximum(p, r) + jnp.maximum(jnp.minimum(p, r), jnp.maximum(q, s))


def _mix_kernel(ro_ref, fo_ref, mo_ref, gr_ref, gf_ref, gm_ref, x_ref,
                wr_ref, wf_ref, wm_ref, wo_ref, nf_ref, rw_ref, rb_ref,
                xa_ref, cls_ref):
    def branch(a_ref, w_ref, g_ref):
        y = jnp.dot(a_ref[...], w_ref[...], preferred_element_type=F32)
        return jax.nn.sigmoid(g_ref[...].astype(F32)) * y

    merged = (branch(ro_ref, wr_ref, gr_ref) + branch(fo_ref, wf_ref, gf_ref)
              + branch(mo_ref, wm_ref, gm_ref))
    xn = x_ref[...] + jnp.dot(merged.astype(BF16), wo_ref[...], preferred_element_type=F32)
    xa_ref[:, :D_MODEL] = xn
    h = _rms_rows(xn, nf_ref[...])
    h_hi = h.astype(BF16)

    h_lo = (h - h_hi.astype(F32)).astype(BF16)
    rw = rw_ref[...]
    rw_hi = rw.astype(BF16)
    rw_lo = (rw - rw_hi.astype(F32)).astype(BF16)
    dot_nt = lambda a, b: lax.dot_general(a, b, NT_DIMS, preferred_element_type=F32)
    logits = dot_nt(rw_hi, h_hi) + dot_nt(rw_hi, h_lo) + dot_nt(rw_lo, h_hi)
    scores = jax.nn.sigmoid(logits)
    biased = scores + rb_ref[...]
    n = EXPERTS_PER_GROUP
    sc = [scores[SUBLANES * j:SUBLANES * (j + 1)] for j in range(n)]
    bi = [biased[SUBLANES * j:SUBLANES * (j + 1)] for j in range(n)]
    group_score = _top2_sum(*bi)
    rows = lax.broadcasted_iota(jnp.int32, group_score.shape, 0)
    best = jnp.max(group_score, axis=0, keepdims=True)
    top_group = jnp.min(jnp.where(group_score == best, rows, SUBLANES), axis=0, keepdims=True)
    in_group = rows == top_group
    picked, chosen = [], []
    for j in range(n):
        rank = jnp.zeros(group_score.shape, jnp.int32)
        for i in range(n):
            if i == j:
                continue
            ahead = (bi[i] > bi[j]) | ((bi[i] == bi[j]) & (i < j))
            rank = rank + ahead.astype(jnp.int32)
        sel = in_group & (rank < 2)
        picked.append(jnp.where(sel, sc[j], 0.0))
        chosen.append(jnp.max(jnp.where(sel, 1.0, 0.0), axis=0, keepdims=True) > 0.5)
    denom = jnp.sum(picked[0] + picked[1] + picked[2] + picked[3], axis=0, keepdims=True)
    gate = [jnp.sum(picked[j], axis=0, keepdims=True) / denom for j in range(n)]

    first = jnp.full(top_group.shape, n, jnp.int32)
    second = jnp.full(top_group.shape, -1, jnp.int32)
    for j in range(n):
        first = jnp.minimum(first, jnp.where(chosen[j], j, n))
        second = jnp.maximum(second, jnp.where(chosen[j], j, -1))
    pair_base = jnp.where(first == 0, 0, jnp.where(first == 1, 3, 5))
    cls = top_group * PAIRS_PER_GROUP + pair_base + second - first - 1
    cls_ref[...] = jnp.clip(cls, 0, N_CLASSES - 1)
    g_first = sum(jnp.where(first == j, gate[j], 0.0) for j in range(n))
    g_second = sum(jnp.where(second == j, gate[j], 0.0) for j in range(n))
    srow = lax.broadcasted_iota(jnp.int32, (SUBLANES, 1), 0)
    gates8 = jnp.where(srow == 0, g_first, jnp.where(srow == 1, g_second, 0.0))
    pad = jnp.zeros((LANES - SUBLANES, gates8.shape[1]), F32)
    xa_ref[:, D_MODEL:] = jnp.concatenate([gates8, pad], axis=0).T


def _mix_and_route(ro, fo, mo, proj, x2, w_r, w_f, w_m, w_o, norm_ffn, rw_pad, rb_pad):
    t = x2.shape[0]
    tok = lambda i: (i, 0)
    const = lambda i: (0, 0)
    act = pl.BlockSpec((MIX_TM, D_MODEL), tok)
    wspec = pl.BlockSpec((D_MODEL, D_MODEL), const)
    return pl.pallas_call(
        _mix_kernel,
        grid=(t // MIX_TM,),
        in_specs=[
            act, act, act,
            pl.BlockSpec((MIX_TM, D_MODEL), lambda i: (i, 8)),
            pl.BlockSpec((MIX_TM, D_MODEL), lambda i: (i, 9)),
            pl.BlockSpec((MIX_TM, D_MODEL), lambda i: (i, 10)),
            act,
            wspec, wspec, wspec, wspec,
            pl.BlockSpec((1, D_MODEL), const),
            pl.BlockSpec((ROUTER_ROWS, D_MODEL), const),
            pl.BlockSpec((ROUTER_ROWS, 1), const),
        ],
        out_specs=[
            pl.BlockSpec((MIX_TM, ROW_WIDTH), tok),
            pl.BlockSpec((1, MIX_TM), lambda i: (0, i)),
        ],
        out_shape=[
            jax.ShapeDtypeStruct((t, ROW_WIDTH), F32),
            jax.ShapeDtypeStruct((1, t), jnp.int32),
        ],
        compiler_params=_params("parallel"),
        name="mix_and_route",
    )(ro, fo, mo, proj, proj, proj, x2, w_r, w_f, w_m, w_o, norm_ffn, rw_pad, rb_pad)


EXP_TR = 512
POS_TM = 2048
POS_SUB = 512
MOVE_TM = 1024
CLASS_ROWS = 32


def _position_kernel(cls_ref, pos_ref, offs_ref, size_ref, counts_ref, running_ref):
    phase = pl.program_id(0)
    i = pl.program_id(1)

    def class_onehot(cls):
        rows = lax.broadcasted_iota(jnp.int32, (CLASS_ROWS, cls.shape[1]), 0)
        return jnp.where(rows == cls, 1.0, 0.0)

    @pl.when(phase == 0)
    def _count():
        @pl.when(i == 0)
        def _():
            counts_ref[...] = jnp.zeros_like(counts_ref)
        counts_ref[...] += jnp.sum(class_onehot(cls_ref[...]), axis=1, keepdims=True)
        pos_ref[...] = jnp.zeros_like(pos_ref)

    @pl.when(phase == 1)
    def _place():
        @pl.when(i == 0)
        def _():
            padded = jnp.floor((counts_ref[...] + (EXP_TR - 1)) * (1.0 / EXP_TR)) * EXP_TR
            r = lax.broadcasted_iota(jnp.int32, (CLASS_ROWS, CLASS_ROWS), 0)
            c = lax.broadcasted_iota(jnp.int32, (CLASS_ROWS, CLASS_ROWS), 1)
            below = jnp.where(c < r, 1.0, 0.0)
            offs = jnp.dot(below, padded, preferred_element_type=F32,
                           precision=lax.Precision.HIGHEST)
            offs_ref[...] = offs
            size_ref[...] = padded
            running_ref[...] = offs

        r = lax.broadcasted_iota(jnp.int32, (POS_SUB, POS_SUB), 0)
        c = lax.broadcasted_iota(jnp.int32, (POS_SUB, POS_SUB), 1)
        earlier = jnp.where(r < c, 1.0, 0.0).astype(BF16)
        running = running_ref[:, 0:1]
        for u in range(POS_TM // POS_SUB):
            cols = slice(u * POS_SUB, (u + 1) * POS_SUB)
            onehot = class_onehot(cls_ref[:, cols])
            prefix = jnp.dot(onehot.astype(BF16), earlier, preferred_element_type=F32)
            pos_ref[:, cols] = jnp.sum(onehot * (prefix + running), axis=0,
                                       keepdims=True).astype(jnp.int32)
            running = running + jnp.sum(onehot, axis=1, keepdims=True)
        running_ref[...] = jnp.broadcast_to(running, running_ref.shape)


def _positions(cls):
    t = cls.shape[1]
    meta = jax.ShapeDtypeStruct((CLASS_ROWS, LANES), F32)
    return pl.pallas_call(
        _position_kernel,
        grid=(2, t // POS_TM),
        in_specs=[pl.BlockSpec((1, POS_TM), lambda p, i: (0, i))],
        out_specs=[
            pl.BlockSpec((1, POS_TM), lambda p, i: (0, i * p)),
            pl.BlockSpec((CLASS_ROWS, LANES), lambda p, i: (0, 0)),
            pl.BlockSpec((CLASS_ROWS, LANES), lambda p, i: (0, 0)),
        ],
        out_shape=[jax.ShapeDtypeStruct((1, t), jnp.int32), meta, meta],
        scratch_shapes=[pltpu.VMEM((CLASS_ROWS, LANES), F32), pltpu.VMEM((CLASS_ROWS, LANES), F32)],
        compiler_params=_params("arbitrary", "arbitrary"),
        name="moe_positions",
    )(cls)


def _move_rows(copy_for_row):
    def start(group, carry):
        for sub in range(SUBLANES):
            copy_for_row(group, sub).start(priority=sub % 2)
        return carry

    def wait(group, carry):
        for sub in range(SUBLANES):
            copy_for_row(group, sub).wait()
        return carry

    lax.fori_loop(0, MOVE_TM // SUBLANES, start, 0)
    lax.fori_loop(0, MOVE_TM // SUBLANES, wait, 0)


def _scatter_rows_kernel(pos_ref, src_ref, init_ref, dst_ref, sem):
    del init_ref
    _move_rows(lambda group, sub: pltpu.make_async_copy(
        src_ref.at[group, pl.ds(sub, 1)],
        dst_ref.at[pl.ds(pos_ref[group * SUBLANES + sub], 1)], sem))


def _scatter_rows(pos, src, init):
    t, width = src.shape
    return pl.pallas_call(
        _scatter_rows_kernel,
        grid=(t // MOVE_TM,),
        in_specs=[
            pl.BlockSpec((MOVE_TM,), lambda i: (i,), memory_space=pltpu.SMEM),
            pl.BlockSpec((MOVE_TM // SUBLANES, SUBLANES, width), lambda i: (i, 0, 0)),
            pl.BlockSpec(memory_space=pl.ANY),
        ],
        out_specs=pl.BlockSpec(memory_space=pl.ANY),
        out_shape=jax.ShapeDtypeStruct(init.shape, init.dtype),
        scratch_shapes=[pltpu.SemaphoreType.DMA(())],
        input_output_aliases={2: 0},
        compiler_params=_params("arbitrary"),
        name="moe_scatter_rows",
    )(pos, src.reshape(t // SUBLANES, SUBLANES, width), init)


def _gather_rows_kernel(pos_ref, src_ref, dst_ref, sem):
    _move_rows(lambda group, sub: pltpu.make_async_copy(
        src_ref.at[pl.ds(pos_ref[group * SUBLANES + sub], 1)],
        dst_ref.at[group, pl.ds(sub, 1)], sem))


def _gather_rows(pos, src):
    t = pos.shape[0]
    width = src.shape[1]
    out = pl.pallas_call(
        _gather_rows_kernel,
        grid=(t // MOVE_TM,),
        in_specs=[
            pl.BlockSpec((MOVE_TM,), lambda i: (i,), memory_space=pltpu.SMEM),
            pl.BlockSpec(memory_space=pl.ANY),
        ],
        out_specs=pl.BlockSpec((MOVE_TM // SUBLANES, SUBLANES, width), lambda i: (i, 0, 0)),
        out_shape=jax.ShapeDtypeStruct((t // SUBLANES, SUBLANES, width), src.dtype),
        scratch_shapes=[pltpu.SemaphoreType.DMA(())],
        compiler_params=_params("arbitrary"),
        name="moe_gather_rows",
    )(pos, src)
    return out.reshape(t, width)


def _experts_kernel(ea_ref, eb_ref, used_ref, xs_ref, nf_ref,
                    wga_ref, wua_ref, wda_ref, wgb_ref, wub_ref, wdb_ref, ys_ref):
    del ea_ref, eb_ref
    k = pl.program_id(0)

    @pl.when(k < used_ref[0])
    def _():
        x = xs_ref[:, :D_MODEL]
        gates = xs_ref[:, D_MODEL:]
        h = _rms_rows(x, nf_ref[...]).astype(BF16)

        def ffn(wg_ref, wu_ref, wd_ref):
            a = jnp.dot(h, wg_ref[...], preferred_element_type=F32)
            u = jnp.dot(h, wu_ref[...], preferred_element_type=F32)
            return jnp.dot((a * jax.nn.sigmoid(a) * u).astype(BF16), wd_ref[...],
                           preferred_element_type=F32)

        moe = (gates[:, 0:1] * ffn(wga_ref, wua_ref, wda_ref)
               + gates[:, 1:2] * ffn(wgb_ref, wub_ref, wdb_ref))
        ys_ref[...] = x + moe

    @pl.when(k >= used_ref[0])
    def _():
        ys_ref[...] = jnp.zeros_like(ys_ref)


def _experts(xs, norm_ffn, w_gate, w_up, w_down, expert_a, expert_b, n_used):
    n_tiles = xs.shape[0] // EXP_TR
    up = lambda sel: pl.BlockSpec((None, D_MODEL, D_FF), lambda k, ea, eb, nu: (sel(ea, eb)[k], 0, 0))
    down = lambda sel: pl.BlockSpec((None, D_FF, D_MODEL), lambda k, ea, eb, nu: (sel(ea, eb)[k], 0, 0))
    first = lambda ea, eb: ea
    second = lambda ea, eb: eb
    grid_spec = pltpu.PrefetchScalarGridSpec(
        num_scalar_prefetch=3,
        grid=(n_tiles,),
        in_specs=[
            pl.BlockSpec((EXP_TR, ROW_WIDTH), lambda k, ea, eb, nu: (k, 0)),
            pl.BlockSpec((1, D_MODEL), lambda k, ea, eb, nu: (0, 0)),
            up(first), up(first), down(first), up(second), up(second), down(second),
        ],
        out_specs=pl.BlockSpec((EXP_TR, D_MODEL), lambda k, ea, eb, nu: (k, 0)),
    )
    return pl.pallas_call(
        _experts_kernel,
        grid_spec=grid_spec,
        out_shape=jax.ShapeDtypeStruct((xs.shape[0], D_MODEL), F32),
        compiler_params=_params("arbitrary"),
        name="experts",
    )(expert_a, expert_b, n_used, xs, norm_ffn, w_gate, w_up, w_down, w_gate, w_up, w_down)


def _tile_experts(offs, sizes, n_tiles):
    ends = (offs[:N_CLASSES, 0] + sizes[:N_CLASSES, 0]).astype(jnp.int32)
    starts = jnp.arange(n_tiles, dtype=jnp.int32) * EXP_TR
    tile_cls = jnp.minimum(jnp.sum(ends[None, :] <= starts[:, None], axis=1), N_CLASSES - 1)
    group = tile_cls // PAIRS_PER_GROUP
    pair = tile_cls % PAIRS_PER_GROUP
    slot_a = jnp.array([a for a, _ in PAIRS], jnp.int32)[pair]
    slot_b = jnp.array([b for _, b in PAIRS], jnp.int32)[pair]
    expert_a = (group * EXPERTS_PER_GROUP + slot_a).astype(jnp.int32)
    expert_b = (group * EXPERTS_PER_GROUP + slot_b).astype(jnp.int32)
    return expert_a, expert_b, (ends[-1] // EXP_TR).reshape(1)


def _router_layout(router_w, router_bias):
    w = router_w.T.reshape(N_GROUPS, EXPERTS_PER_GROUP, D_MODEL)
    w = jnp.transpose(w, (1, 0, 2))
    w = jnp.pad(w, ((0, 0), (0, SUBLANES - N_GROUPS), (0, 0))).reshape(ROUTER_ROWS, D_MODEL)
    b = jnp.transpose(router_bias.reshape(N_GROUPS, EXPERTS_PER_GROUP), (1, 0))
    b = jnp.pad(b, ((0, 0), (0, SUBLANES - N_GROUPS)), constant_values=NEG_BIG)
    return w.astype(F32), b.reshape(ROUTER_ROWS, 1).astype(F32)


def kernel(x, mem, norm_mix, norm_mem, w_in, b_forget, fox_q_norm, fox_k_norm, mem_q_norm,
           mem_k_norm, w_mem_kv, w_o_ret, w_o_fox, w_o_mem, w_out, norm_ffn, router_w,
           router_bias, w_gate, w_up, w_down):
    batch, seq, d = x.shape
    mem_len = mem.shape[1]
    depth = w_in.shape[0]
    t = batch * seq
    ff_lo = 7 * D_MODEL
    ff_hi = ff_lo + FOX_HEADS

    cos, sin, intra, qd, kd, cd = _retention_tables(seq)
    rw_pad, rb_pad = _router_layout(router_w, router_bias)
    row = lambda v: v.reshape(1, -1).astype(F32)

    x2 = x.reshape(t, d)
    mem2 = mem.reshape(batch * mem_len, d)
    n_tiles = t // EXP_TR + N_CLASSES
    sorted_rows = jnp.zeros((n_tiles * EXP_TR, ROW_WIDTH), F32)
    for l in range(depth):
        w_main = jnp.concatenate([w_in[l, :, :ff_lo], w_in[l, :, ff_hi:]], axis=1).astype(BF16)
        w_ff = jnp.pad(w_in[l, :, ff_lo:ff_hi], ((0, 0), (0, LANES - FOX_HEADS))).astype(BF16)
        b_pad = jnp.pad(b_forget[l], (0, LANES - FOX_HEADS)).reshape(1, LANES)

        proj, ff = _in_projection(x2, row(norm_mix[l]), w_main, w_ff)
        cum, cum_split = _forget_cumsum(ff, b_pad, batch, seq)
        ro = _retention(proj, cos, sin, intra, qd, kd, cd, batch, seq)
        bound = (FOX_BOUND_SCALE * jnp.max(jnp.abs(fox_q_norm[l]))
                 * jnp.max(jnp.abs(fox_k_norm[l])))
        first = _fox_first_blocks(cum, bound, batch, seq)
        fo = _fox_attention(proj, cum_split, bound.reshape(1).astype(F32), first,
                            row(jnp.tile(fox_q_norm[l], 2)), row(jnp.tile(fox_k_norm[l], 2)),
                            batch, seq)
        mk, mv = _mem_kv(mem2, row(norm_mem[l]), w_mem_kv[l].astype(BF16), row(mem_k_norm[l]))
        mo = _mem_attention(proj, mk, mv, row(mem_q_norm[l]), batch, seq, mem_len)
        xa, cls = _mix_and_route(
            ro, fo, mo, proj, x2, w_o_ret[l].astype(BF16), w_o_fox[l].astype(BF16),
            w_o_mem[l].astype(BF16), w_out[l].astype(BF16), row(norm_ffn[l]), rw_pad, rb_pad)
        pos, offs, sizes = _positions(cls)
        pos = pos.reshape(t)
        expert_a, expert_b, n_used = _tile_experts(offs, sizes, n_tiles)
        sorted_rows = _scatter_rows(pos, xa, sorted_rows)
        ys = _experts(sorted_rows, row(norm_ffn[l]), w_gate[l].astype(BF16), w_up[l].astype(BF16),
                      w_down[l].astype(BF16), expert_a, expert_b, n_used)
        x2 = _gather_rows(pos, ys)
    return x2.reshape(batch, seq, d)
```

```python
import jax
import jax.numpy as jnp
from jax import lax
from jax.experimental import pallas as pl
from jax.experimental.pallas import tpu as pltpu

F32 = jnp.float32
BF16 = jnp.bfloat16

D_MODEL = 1024
EPS = 1e-6
RET_HEADS = 4
RET_HEAD_DIM = 256
RET_CHUNK = 128
ROPE_BASE = 10000.0
FOX_HEADS = 16
FOX_HEAD_DIM = 64
MEM_HEADS = 4
MEM_HEAD_DIM = 256
N_GROUPS = 4
EXPERTS_PER_GROUP = 4
D_FF = 512

LANES = 128
SUBLANES = 8
VMEM_LIMIT = 56 * 1024 * 1024
NEG_BIG = -1e30

NT_DIMS = (((1,), (1,)), ((), ()))


def _params(*sem):
    return pltpu.CompilerParams(dimension_semantics=sem, vmem_limit_bytes=VMEM_LIMIT)


def _rms_rows(x, gain_row):
    ms = jnp.mean(x * x, axis=-1, keepdims=True)
    return x * lax.rsqrt(ms + EPS) * gain_row


IN_TM = 1024
IN_BLOCKS = 11
IN_TN = 2816


def _inproj_kernel(x_ref, g_ref, w_ref, wff_ref, out_ref, ff_ref, hn_ref):
    @pl.when(pl.program_id(1) == 0)
    def _():
        hb = _rms_rows(x_ref[...], g_ref[...]).astype(BF16)
        hn_ref[...] = hb
        ff_ref[...] = jnp.dot(hb, wff_ref[...], preferred_element_type=F32)

    out_ref[...] = jnp.dot(hn_ref[...], w_ref[...], preferred_element_type=F32).astype(BF16)


def _in_projection(x2, gain, w_main, w_ff):
    t = x2.shape[0]
    return pl.pallas_call(
        _inproj_kernel,
        grid=(t // IN_TM, IN_BLOCKS * D_MODEL // IN_TN),
        in_specs=[
            pl.BlockSpec((IN_TM, D_MODEL), lambda i, j: (i, 0)),
            pl.BlockSpec((1, D_MODEL), lambda i, j: (0, 0)),
            pl.BlockSpec((D_MODEL, IN_TN), lambda i, j: (0, j)),
            pl.BlockSpec((D_MODEL, LANES), lambda i, j: (0, 0)),
        ],
        out_specs=[
            pl.BlockSpec((IN_TM, IN_TN), lambda i, j: (i, j)),
            pl.BlockSpec((IN_TM, LANES), lambda i, j: (i, 0)),
        ],
        out_shape=[
            jax.ShapeDtypeStruct((t, IN_BLOCKS * D_MODEL), BF16),
            jax.ShapeDtypeStruct((t, LANES), F32),
        ],
        scratch_shapes=[pltpu.VMEM((IN_TM, D_MODEL), BF16)],
        compiler_params=_params("parallel", "arbitrary"),
        name="in_projection",
    )(x2, gain, w_main, w_ff)


CUM_BLOCK = 512
AUG_SPLIT = 3


def _split3(x):
    hi = x.astype(BF16)
    r1 = x - hi.astype(F32)
    mid = r1.astype(BF16)
    lo = (r1 - mid.astype(F32)).astype(BF16)
    return hi, mid, lo


def _cumsum_kernel(ff_ref, b_ref, out_ref, split_ref):
    s = ff_ref.shape[0]
    r = lax.broadcasted_iota(jnp.int32, (CUM_BLOCK, CUM_BLOCK), 0)
    c = lax.broadcasted_iota(jnp.int32, (CUM_BLOCK, CUM_BLOCK), 1)
    lower = (c <= r).astype(BF16)
    carry = jnp.zeros((1, LANES), F32)
    for blk in range(s // CUM_BLOCK):
        rows = slice(blk * CUM_BLOCK, (blk + 1) * CUM_BLOCK)
        z = ff_ref[rows, :] + b_ref[...]
        log_f = jnp.minimum(z, 0.0) - jnp.log1p(jnp.exp(-jnp.abs(z)))
        cs = sum(jnp.dot(lower, part, preferred_element_type=F32) for part in _split3(log_f)) + carry
        out_ref[rows, :] = cs
        for k, part in enumerate(_split3(cs)):
            split_ref[k, rows, :] = part
        carry = cs[CUM_BLOCK - 1:CUM_BLOCK, :]


def _forget_cumsum(ff, b_pad, batch, seq):
    return pl.pallas_call(
        _cumsum_kernel,
        grid=(batch,),
        in_specs=[
            pl.BlockSpec((seq, LANES), lambda b: (b, 0)),
            pl.BlockSpec((1, LANES), lambda b: (0, 0)),
        ],
        out_specs=[
            pl.BlockSpec((seq, LANES), lambda b: (b, 0)),
            pl.BlockSpec((AUG_SPLIT, seq, LANES), lambda b: (0, b, 0)),
        ],
        out_shape=[
            jax.ShapeDtypeStruct((batch * seq, LANES), F32),
            jax.ShapeDtypeStruct((AUG_SPLIT, batch * seq, LANES), BF16),
        ],
        compiler_params=_params("parallel"),
        name="forget_cumsum",
    )(ff, b_pad)


RET_ROWS = 1024
RET_GROUP = 4


def _retention_kernel(q_ref, k_ref, v_ref, g_ref, cos_ref, sin_ref, intra_ref, qd_ref, kd_ref,
                      cd_ref, out_ref, state_ref):
    hd = RET_HEAD_DIM
    half = hd // 2

    @pl.when(pl.program_id(2) == 0)
    def _():
        state_ref[...] = jnp.zeros_like(state_ref)

    def chunk(c, carry):
        r0 = pl.multiple_of(c * RET_CHUNK, RET_CHUNK)
        rows = pl.ds(r0, RET_CHUNK)
        cs = cos_ref[rows, :]
        sn = sin_ref[rows, :]

        def rot(x):
            x1 = x[:, :half]
            x2 = x[:, half:]
            return jnp.concatenate([x1 * cs - x2 * sn, x2 * cs + x1 * sn], axis=-1)

        for h in range(RET_GROUP):
            cols = slice(h * hd, (h + 1) * hd)
            qr = rot(q_ref[rows, cols].astype(F32))
            kr = rot(k_ref[rows, cols].astype(F32))
            v = v_ref[rows, cols]
            scores = lax.dot_general(qr.astype(BF16), kr.astype(BF16), NT_DIMS,
                                     preferred_element_type=F32) * intra_ref[h]
            st = state_ref[h]
            o = (jnp.dot(scores.astype(BF16), v, preferred_element_type=F32)
                 + jnp.dot((qr * qd_ref[h]).astype(BF16), st.astype(BF16),
                           preferred_element_type=F32))
            kd_t = (kr * kd_ref[h]).T.astype(BF16)
            state_ref[h] = st * cd_ref[h] + jnp.dot(kd_t, v, preferred_element_type=F32)
            on = o * lax.rsqrt(jnp.mean(o * o, axis=-1, keepdims=True) + EPS)
            g = g_ref[rows, cols].astype(F32)
            out_ref[rows, cols] = (on * (g * jax.nn.sigmoid(g))).astype(BF16)
        return carry

    lax.fori_loop(0, RET_ROWS // RET_CHUNK, chunk, 0)


def _retention(proj, cos, sin, intra, qd, kd, cd, batch, seq):
    t = batch * seq
    hd = RET_HEAD_DIM
    width = RET_GROUP * hd
    groups = RET_HEADS // RET_GROUP
    steps = seq // RET_ROWS
    col = lambda off: (lambda b, p, s: (b * steps + s, off + p))
    tab = lambda b, p, s: (p, 0, 0)
    angle = pl.BlockSpec((RET_ROWS, hd // 2), lambda b, p, s: (s, 0))
    return pl.pallas_call(
        _retention_kernel,
        grid=(batch, groups, steps),
        in_specs=[
            pl.BlockSpec((RET_ROWS, width), col(0)),
            pl.BlockSpec((RET_ROWS, width), col(groups)),
            pl.BlockSpec((RET_ROWS, width), col(2 * groups)),
            pl.BlockSpec((RET_ROWS, width), col(3 * groups)),
            angle, angle,
            pl.BlockSpec((RET_GROUP, RET_CHUNK, RET_CHUNK), tab),
            pl.BlockSpec((RET_GROUP, RET_CHUNK, hd), tab),
            pl.BlockSpec((RET_GROUP, RET_CHUNK, hd), tab),
            pl.BlockSpec((RET_GROUP, 1, hd), tab),
        ],
        out_specs=pl.BlockSpec((RET_ROWS, width), lambda b, p, s: (b * steps + s, p)),
        out_shape=jax.ShapeDtypeStruct((t, RET_HEADS * hd), BF16),
        scratch_shapes=[pltpu.VMEM((RET_GROUP, hd, hd), F32)],
        compiler_params=_params("parallel", "parallel", "arbitrary"),
        name="retention",
    )(proj, proj, proj, proj, cos, sin, intra, qd, kd, cd)


def _retention_tables(seq):
    h = jnp.arange(RET_HEADS, dtype=F32)
    log_gamma = jnp.log1p(-(2.0 ** (-5.0 - h)))
    idx = jnp.arange(RET_CHUNK, dtype=F32)
    diff = idx[:, None] - idx[None, :]
    scale = RET_HEAD_DIM ** -0.5
    intra = jnp.where(diff >= 0, jnp.exp(log_gamma[:, None, None] * jnp.maximum(diff, 0.0)), 0.0) * scale
    q_decay = jnp.exp(log_gamma[:, None] * (idx + 1.0))
    k_decay = jnp.exp(log_gamma[:, None] * (RET_CHUNK - 1.0 - idx)) * scale
    chunk_decay = jnp.exp(log_gamma * RET_CHUNK)
    qd = jnp.broadcast_to(q_decay[:, :, None], (RET_HEADS, RET_CHUNK, RET_HEAD_DIM))
    kd = jnp.broadcast_to(k_decay[:, :, None], (RET_HEADS, RET_CHUNK, RET_HEAD_DIM))
    cd = jnp.broadcast_to(chunk_decay[:, None, None], (RET_HEADS, 1, RET_HEAD_DIM))
    inv_freq = ROPE_BASE ** (-jnp.arange(0, RET_HEAD_DIM, 2, dtype=F32) / RET_HEAD_DIM)
    ang = jnp.arange(seq, dtype=F32)[:, None] * inv_freq[None, :]
    return jnp.cos(ang), jnp.sin(ang), intra, qd, kd, cd


FOX_TQ = 512
FOX_TK = 512
assert FOX_TQ == FOX_TK
FOX_QSTEP = 8


def _pair_rms(x, lo_mask, gain_row):
    x2 = x * x
    s_all = jnp.sum(x2, axis=-1, keepdims=True)
    s_lo = jnp.sum(jnp.where(lo_mask, x2, 0.0), axis=-1, keepdims=True)
    ms = jnp.where(lo_mask, s_lo, s_all - s_lo) * (1.0 / FOX_HEAD_DIM)
    return x * lax.rsqrt(ms + EPS) * gain_row


AUG_CQ = 0
AUG_CK = 3
AUG_M = 6
FOX_EXACT_ABOVE = 30.0
FOX_BOUND_SCALE = 8.0 * 1.02
FOX_UNDERFLOW_LOG = -88.0


def _lanes_in(lane, start, count=AUG_SPLIT):
    return (lane >= start) & (lane < start + count)


def _fox_kernel(bound_ref, first_ref, q_ref, k_ref, v_ref, c_ref, qg_ref, kg_ref, out_ref,
                kaug_ref, caq_ref, vat_ref, qb_ref, qat_ref, acc_ref, m_ref):
    seq = k_ref.shape[0]
    hp = pl.program_id(1)
    step = pl.program_id(2)
    lane = lax.broadcasted_iota(jnp.int32, (1, LANES), 1)
    lo = lane < FOX_HEAD_DIM
    head_lanes = (lo, jnp.logical_not(lo))
    spare = (FOX_HEAD_DIM, 0)
    bound = bound_ref[0]

    @pl.when(step == 0)
    def _prepare_keys_values():
        r = lax.broadcasted_iota(jnp.int32, (LANES, LANES), 0)
        c = lax.broadcasted_iota(jnp.int32, (LANES, LANES), 1)
        place = []
        for k in range(AUG_SPLIT):
            pk = jnp.zeros((LANES, LANES), F32)
            for h in range(2):
                src = r == 2 * hp + h
                pk = pk + jnp.where(src & (c == spare[h] + AUG_CQ + k), 1.0, 0.0)
                pk = pk - jnp.where(src & (c == spare[h] + AUG_CK + k), 1.0, 0.0)
            place.append(pk.astype(BF16))
        bound_parts = [p.astype(F32) for p in _split3(jnp.full((1, LANES), -bound, F32))]

        def prep(i, carry):
            rows = pl.ds(pl.multiple_of(i * FOX_TK, FOX_TK), FOX_TK)
            e = sum(jnp.dot(c_ref[k, rows, :], place[k], preferred_element_type=F32)
                    for k in range(AUG_SPLIT))
            kn = _pair_rms(k_ref[rows, :].astype(F32), lo, kg_ref[...])
            v = v_ref[rows, :]
            one = jnp.ones_like(v)
            for h in range(2):
                ones_k = jnp.where(_lanes_in(lane, spare[h] + AUG_CQ)
                                   | _lanes_in(lane, spare[h] + AUG_M), 1.0, 0.0)
                kaug_ref[h, rows, :] = jnp.where(
                    head_lanes[h], kn,
                    jnp.where(_lanes_in(lane, spare[h] + AUG_CK), e, ones_k)).astype(BF16)
                const_q = jnp.where(_lanes_in(lane, spare[h] + AUG_CK), 1.0, 0.0)
                for k in range(AUG_SPLIT):
                    const_q = jnp.where(lane == spare[h] + AUG_M + k, bound_parts[k], const_q)
                caq_ref[h, rows, :] = jnp.where(
                    _lanes_in(lane, spare[h] + AUG_CQ), e, const_q).astype(BF16)
                va = jnp.where(head_lanes[h], v, one)
                vat_ref[h, i] = va.T
            return carry
        lax.fori_loop(0, seq // FOX_TK, prep, 0)

    def query_block(sub):
        qi = step * FOX_QSTEP + sub
        first = first_ref[(pl.program_id(0) * pl.num_programs(1) + hp)
                          * (pl.num_programs(2) * FOX_QSTEP) + qi]
        local_rows = pl.ds(sub * FOX_TQ, FOX_TQ)
        t0 = pl.multiple_of(qi * FOX_TQ, FOX_TQ)
        q_rows = pl.ds(t0, FOX_TQ)
        qn = _pair_rms(q_ref[local_rows, :].astype(F32), lo, qg_ref[...]) * (FOX_HEAD_DIM ** -0.5)
        kq = lax.broadcasted_iota(jnp.int32, (FOX_TK, FOX_TQ), 0)
        qq = lax.broadcasted_iota(jnp.int32, (FOX_TK, FOX_TQ), 1)

        q_aug = [jnp.where(head_lanes[h], qn, caq_ref[h, q_rows, :].astype(F32)) for h in range(2)]
        for h in range(2):
            qat_ref[h] = q_aug[h].astype(BF16).T

        @pl.when(bound > FOX_EXACT_ABOVE)
        def _exact_row_max():
            for h in range(2):
                qb_ref[h] = jnp.where(_lanes_in(lane, spare[h] + AUG_M), 0.0, q_aug[h]).astype(BF16)
                m_ref[h] = jnp.full((FOX_TQ, 1), NEG_BIG, F32)

            def scan(j, masked):
                keys = pl.ds(pl.multiple_of(j * FOX_TK, FOX_TK), FOX_TK)
                for h in range(2):
                    s = lax.dot_general(qb_ref[h], kaug_ref[h, keys, :], NT_DIMS,
                                        preferred_element_type=F32)
                    if masked:
                        s = jnp.where(qq <= kq, s, NEG_BIG)
                    m_ref[h] = jnp.maximum(m_ref[h], jnp.max(s, axis=-1, keepdims=True))

            def scan_body(j, carry):
                scan(j, False)
                return carry
            lax.fori_loop(first, qi, scan_body, 0)
            scan(qi, True)
            for h in range(2):
                m_parts = _split3(-m_ref[h])
                qa = q_aug[h]
                for k in range(AUG_SPLIT):
                    qa = jnp.where(lane == spare[h] + AUG_M + k, m_parts[k].astype(F32), qa)
                qat_ref[h] = qa.T.astype(BF16)

        acc_ref[...] = jnp.zeros_like(acc_ref)

        def probs(h, keys, q_cols=slice(None)):
            st = jnp.dot(kaug_ref[h, keys, :], qat_ref[h, :, q_cols], preferred_element_type=F32)
            return jnp.exp(st)

        def kv_blocks(blocks):
            for h in range(2):
                total = None
                for j in blocks:
                    keys = pl.ds(pl.multiple_of(j * FOX_TK, FOX_TK), FOX_TK)
                    o = jnp.dot(vat_ref[h, j], probs(h, keys).astype(BF16),
                                preferred_element_type=F32)
                    total = o if total is None else total + o
                acc_ref[h] += total

        def pair(i, carry):
            kv_blocks((first + 2 * i, first + 2 * i + 1))
            return carry

        full_blocks = qi - first
        lax.fori_loop(0, lax.shift_right_logical(full_blocks, 1), pair, 0)

        @pl.when((full_blocks & 1) == 1)
        def _odd_block():
            kv_blocks((qi - 1,))

        half = FOX_TK // 2
        lower, upper = slice(0, half), slice(half, FOX_TK)
        causal_a = (lax.broadcasted_iota(jnp.int32, (half, FOX_TQ), 0)
                    <= lax.broadcasted_iota(jnp.int32, (half, FOX_TQ), 1))
        causal_b = (lax.broadcasted_iota(jnp.int32, (half, half), 0)
                    <= lax.broadcasted_iota(jnp.int32, (half, half), 1))
        for h in range(2):
            p_a = jnp.where(causal_a, probs(h, pl.ds(t0, half)), 0.0)
            p_b = jnp.where(causal_b, probs(h, pl.ds(t0 + half, half), upper), 0.0)
            acc_ref[h] += jnp.dot(vat_ref[h, qi, :, lower], p_a.astype(BF16),
                                  preferred_element_type=F32)
            acc_ref[h, :, upper] += jnp.dot(vat_ref[h, qi, :, upper], p_b.astype(BF16),
                                            preferred_element_type=F32)

        o0 = acc_ref[0]
        o1 = acc_ref[1]
        l0 = o0[FOX_HEAD_DIM:FOX_HEAD_DIM + 1, :]
        l1 = o1[0:1, :]
        row = lax.broadcasted_iota(jnp.int32, (LANES, 1), 0)
        out_ref[local_rows, :] = jnp.where(row < FOX_HEAD_DIM, o0 / l0, o1 / l1).T.astype(BF16)

    for sub in range(FOX_QSTEP):
        query_block(sub)


def _fox_first_blocks(cum, bound, batch, seq):
    blocks = seq // FOX_TK
    c = cum.reshape(batch, seq, LANES)[:, :, :FOX_HEADS]
    c_start = c[:, 0::FOX_TQ, :]
    c_end = c[:, FOX_TK - 1::FOX_TK, :]
    negligible = (c_start[:, :, None, :] - c_end[:, None, :, :] + 2.0 * bound) < FOX_UNDERFLOW_LOG
    idx = jnp.arange(blocks)
    earlier = (idx[None, :] < idx[:, None])[None, :, :, None]
    count = jnp.sum(negligible & earlier, axis=2)
    first = jnp.min(count.reshape(batch, blocks, FOX_HEADS // 2, 2), axis=-1)
    return jnp.transpose(first, (0, 2, 1)).reshape(-1).astype(jnp.int32)


def _fox_attention(proj, cum_split, bound, first, qg, kg, batch, seq):
    t = batch * seq
    nq = seq // (FOX_QSTEP * FOX_TQ)
    pairs = FOX_HEADS // 2
    base = 4 * (D_MODEL // LANES)
    return pl.pallas_call(
        _fox_kernel,
        grid=(batch, pairs, nq),
        in_specs=[
            pl.BlockSpec(memory_space=pltpu.SMEM),
            pl.BlockSpec(memory_space=pltpu.SMEM),
            pl.BlockSpec((FOX_QSTEP * FOX_TQ, LANES), lambda b, p, i: (b * nq + i, base + p)),
            pl.BlockSpec((seq, LANES), lambda b, p, i: (b, base + pairs + p)),
            pl.BlockSpec((seq, LANES), lambda b, p, i: (b, base + 2 * pairs + p)),
            pl.BlockSpec((AUG_SPLIT, seq, LANES), lambda b, p, i: (0, b, 0)),
            pl.BlockSpec((1, LANES), lambda b, p, i: (0, 0)),
            pl.BlockSpec((1, LANES), lambda b, p, i: (0, 0)),
        ],
        out_specs=pl.BlockSpec((FOX_QSTEP * FOX_TQ, LANES), lambda b, p, i: (b * nq + i, p)),
        out_shape=jax.ShapeDtypeStruct((t, FOX_HEADS * FOX_HEAD_DIM), BF16),
        scratch_shapes=[
            pltpu.VMEM((2, seq, LANES), BF16),
            pltpu.VMEM((2, seq, LANES), BF16),
            pltpu.VMEM((2, seq // FOX_TK, LANES, FOX_TK), BF16),
            pltpu.VMEM((2, FOX_TQ, LANES), BF16),
            pltpu.VMEM((2, LANES, FOX_TQ), BF16),
            pltpu.VMEM((2, LANES, FOX_TQ), F32),
            pltpu.VMEM((2, FOX_TQ, 1), F32),
        ],
        compiler_params=_params("parallel", "parallel", "arbitrary"),
        name="fox_attention",
    )(bound, first, proj, proj, proj, cum_split, qg, kg)


MEMKV_TM = 512


def _mem_kv_kernel(mem_ref, g_ref, w_ref, kg_ref, k_out, v_out):
    hb = _rms_rows(mem_ref[...], g_ref[...]).astype(BF16)
    kv = jnp.dot(hb, w_ref[...], preferred_element_type=F32)
    width = MEM_HEADS * MEM_HEAD_DIM
    for h in range(MEM_HEADS):
        cols = slice(h * MEM_HEAD_DIM, (h + 1) * MEM_HEAD_DIM)
        k_out[:, cols] = _rms_rows(kv[:, cols], kg_ref[...]).astype(BF16)
    v_out[...] = kv[:, width:].astype(BF16)


def _mem_kv(mem2, gain, w_kv, k_gain):
    rows = mem2.shape[0]
    width = MEM_HEADS * MEM_HEAD_DIM
    return pl.pallas_call(
        _mem_kv_kernel,
        grid=(rows // MEMKV_TM,),
        in_specs=[
            pl.BlockSpec((MEMKV_TM, D_MODEL), lambda i: (i, 0)),
            pl.BlockSpec((1, D_MODEL), lambda i: (0, 0)),
            pl.BlockSpec((D_MODEL, 2 * width), lambda i: (0, 0)),
            pl.BlockSpec((1, MEM_HEAD_DIM), lambda i: (0, 0)),
        ],
        out_specs=[pl.BlockSpec((MEMKV_TM, width), lambda i: (i, 0))] * 2,
        out_shape=[jax.ShapeDtypeStruct((rows, width), BF16)] * 2,
        compiler_params=_params("parallel"),
        name="mem_kv",
    )(mem2, gain, w_kv, k_gain)


MIX_TM = 512
ROUTER_ROWS = 32
PAIRS = ((0, 1), (0, 2), (0, 3), (1, 2), (1, 3), (2, 3))
PAIRS_PER_GROUP = len(PAIRS)
N_CLASSES = N_GROUPS * PAIRS_PER_GROUP
ROW_WIDTH = D_MODEL + LANES


def _top2_sum(b0, b1, b2, b3):
    p, q = jnp.maximum(b0, b1), jnp.minimum(b0, b1)
    r, s = jnp.maximum(b2, b3), jnp.minimum(b2, b3)
    return jnp.maximum(p, r) + jnp.maximum(jnp.minimum(p, r), jnp.maximum(q, s))


def _mix_kernel(ro_ref, fo_ref, mq_ref, mk_ref, mv_ref, mqg_ref, gr_ref, gf_ref, gm_ref, x_ref,
                wr_ref, wf_ref, wm_ref, wo_ref, nf_ref, rw_ref, rb_ref,
                xa_ref, cls_ref):
    def branch(a, w_ref, g_ref):
        y = jnp.dot(a, w_ref[...], preferred_element_type=F32)
        return jax.nn.sigmoid(g_ref[...].astype(F32)) * y

    heads = []
    for h in range(MEM_HEADS):
        cols = slice(h * MEM_HEAD_DIM, (h + 1) * MEM_HEAD_DIM)
        qn = _rms_rows(mq_ref[:, cols].astype(F32), mqg_ref[...]) * (MEM_HEAD_DIM ** -0.5)
        s = lax.dot_general(qn.astype(BF16), mk_ref[:, cols], NT_DIMS, preferred_element_type=F32)
        p = jnp.exp(s - jnp.max(s, axis=-1, keepdims=True))
        denom = jnp.sum(p, axis=-1, keepdims=True)
        o = jnp.dot(p.astype(BF16), mv_ref[:, cols], preferred_element_type=F32)
        heads.append((o / denom).astype(BF16))
    mo = jnp.concatenate(heads, axis=-1)

    merged = (branch(ro_ref[...], wr_ref, gr_ref) + branch(fo_ref[...], wf_ref, gf_ref)
              + branch(mo, wm_ref, gm_ref))
    xn = x_ref[...] + jnp.dot(merged.astype(BF16), wo_ref[...], preferred_element_type=F32)
    xa_ref[:, :D_MODEL] = xn
    h = _rms_rows(xn, nf_ref[...])
    h_hi = h.astype(BF16)

    h_lo = (h - h_hi.astype(F32)).astype(BF16)
    rw = rw_ref[...]
    rw_hi = rw.astype(BF16)
    rw_lo = (rw - rw_hi.astype(F32)).astype(BF16)
    dot_nt = lambda a, b: lax.dot_general(a, b, NT_DIMS, preferred_element_type=F32)
    logits = dot_nt(rw_hi, h_hi) + dot_nt(rw_hi, h_lo) + dot_nt(rw_lo, h_hi)
    scores = jax.nn.sigmoid(logits)
    biased = scores + rb_ref[...]
    n = EXPERTS_PER_GROUP
    sc = [scores[SUBLANES * j:SUBLANES * (j + 1)] for j in range(n)]
    bi = [biased[SUBLANES * j:SUBLANES * (j + 1)] for j in range(n)]
    group_score = _top2_sum(*bi)
    rows = lax.broadcasted_iota(jnp.int32, group_score.shape, 0)
    best = jnp.max(group_score, axis=0, keepdims=True)
    top_group = jnp.min(jnp.where(group_score == best, rows, SUBLANES), axis=0, keepdims=True)
    in_group = rows == top_group
    picked, chosen = [], []
    for j in range(n):
        rank = jnp.zeros(group_score.shape, jnp.int32)
        for i in range(n):
            if i == j:
                continue
            ahead = (bi[i] > bi[j]) | ((bi[i] == bi[j]) & (i < j))
            rank = rank + ahead.astype(jnp.int32)
        sel = in_group & (rank < 2)
        picked.append(jnp.where(sel, sc[j], 0.0))
        chosen.append(jnp.max(jnp.where(sel, 1.0, 0.0), axis=0, keepdims=True) > 0.5)
    denom = jnp.sum(picked[0] + picked[1] + picked[2] + picked[3], axis=0, keepdims=True)
    gate = [jnp.sum(picked[j], axis=0, keepdims=True) / denom for j in range(n)]

    first = jnp.full(top_group.shape, n, jnp.int32)
    second = jnp.full(top_group.shape, -1, jnp.int32)
    for j in range(n):
        first = jnp.minimum(first, jnp.where(chosen[j], j, n))
        second = jnp.maximum(second, jnp.where(chosen[j], j, -1))
    pair_base = jnp.where(first == 0, 0, jnp.where(first == 1, 3, 5))
    cls = top_group * PAIRS_PER_GROUP + pair_base + second - first - 1
    cls_ref[...] = jnp.clip(cls, 0, N_CLASSES - 1)
    g_first = sum(jnp.where(first == j, gate[j], 0.0) for j in range(n))
    g_second = sum(jnp.where(second == j, gate[j], 0.0) for j in range(n))
    srow = lax.broadcasted_iota(jnp.int32, (SUBLANES, 1), 0)
    gates8 = jnp.where(srow == 0, g_first, jnp.where(srow == 1, g_second, 0.0))
    pad = jnp.zeros((LANES - SUBLANES, gates8.shape[1]), F32)
    xa_ref[:, D_MODEL:] = jnp.concatenate([gates8, pad], axis=0).T


def _mix_and_route(ro, fo, mk, mv, mem_q_gain, proj, x2, w_r, w_f, w_m, w_o, norm_ffn, rw_pad, rb_pad,
                   seq, mem_len):
    t = x2.shape[0]
    tiles_per_batch = seq // MIX_TM
    mem = pl.BlockSpec((mem_len, D_MODEL), lambda i: (i // tiles_per_batch, 0))
    tok = lambda i: (i, 0)
    const = lambda i: (0, 0)
    act = pl.BlockSpec((MIX_TM, D_MODEL), tok)
    wspec = pl.BlockSpec((D_MODEL, D_MODEL), const)
    return pl.pallas_call(
        _mix_kernel,
        grid=(t // MIX_TM,),
        in_specs=[
            act, act,
            pl.BlockSpec((MIX_TM, D_MODEL), lambda i: (i, 7)),
            mem, mem,
            pl.BlockSpec((1, MEM_HEAD_DIM), const),
            pl.BlockSpec((MIX_TM, D_MODEL), lambda i: (i, 8)),
            pl.BlockSpec((MIX_TM, D_MODEL), lambda i: (i, 9)),
            pl.BlockSpec((MIX_TM, D_MODEL), lambda i: (i, 10)),
            act,
            wspec, wspec, wspec, wspec,
            pl.BlockSpec((1, D_MODEL), const),
            pl.BlockSpec((ROUTER_ROWS, D_MODEL), const),
            pl.BlockSpec((ROUTER_ROWS, 1), const),
        ],
        out_specs=[
            pl.BlockSpec((MIX_TM, ROW_WIDTH), tok),
            pl.BlockSpec((1, MIX_TM), lambda i: (0, i)),
        ],
        out_shape=[
            jax.ShapeDtypeStruct((t, ROW_WIDTH), F32),
            jax.ShapeDtypeStruct((1, t), jnp.int32),
        ],
        compiler_params=_params("parallel"),
        name="mix_and_route",
    )(ro, fo, proj, mk, mv, mem_q_gain, proj, proj, proj, x2, w_r, w_f, w_m, w_o, norm_ffn, rw_pad, rb_pad)


EXP_TR = 512
POS_TM = 2048
POS_SUB = 512
MOVE_TM = 1024
CLASS_ROWS = 32


def _position_kernel(cls_ref, pos_ref, offs_ref, size_ref, counts_ref, running_ref):
    phase = pl.program_id(0)
    i = pl.program_id(1)

    def class_onehot(cls):
        rows = lax.broadcasted_iota(jnp.int32, (CLASS_ROWS, cls.shape[1]), 0)
        return jnp.where(rows == cls, 1.0, 0.0)

    @pl.when(phase == 0)
    def _count():
        @pl.when(i == 0)
        def _():
            counts_ref[...] = jnp.zeros_like(counts_ref)
        counts_ref[...] += jnp.sum(class_onehot(cls_ref[...]), axis=1, keepdims=True)
        pos_ref[...] = jnp.zeros_like(pos_ref)

    @pl.when(phase == 1)
    def _place():
        @pl.when(i == 0)
        def _():
            padded = jnp.floor((counts_ref[...] + (EXP_TR - 1)) * (1.0 / EXP_TR)) * EXP_TR
            r = lax.broadcasted_iota(jnp.int32, (CLASS_ROWS, CLASS_ROWS), 0)
            c = lax.broadcasted_iota(jnp.int32, (CLASS_ROWS, CLASS_ROWS), 1)
            below = jnp.where(c < r, 1.0, 0.0)
            offs = jnp.dot(below, padded, preferred_element_type=F32,
                           precision=lax.Precision.HIGHEST)
            offs_ref[...] = offs
            size_ref[...] = padded
            running_ref[...] = offs

        r = lax.broadcasted_iota(jnp.int32, (POS_SUB, POS_SUB), 0)
        c = lax.broadcasted_iota(jnp.int32, (POS_SUB, POS_SUB), 1)
        earlier = jnp.where(r < c, 1.0, 0.0).astype(BF16)
        running = running_ref[:, 0:1]
        for u in range(POS_TM // POS_SUB):
            cols = slice(u * POS_SUB, (u + 1) * POS_SUB)
            onehot = class_onehot(cls_ref[:, cols])
            prefix = jnp.dot(onehot.astype(BF16), earlier, preferred_element_type=F32)
            pos_ref[:, cols] = jnp.sum(onehot * (prefix + running), axis=0,
                                       keepdims=True).astype(jnp.int32)
            running = running + jnp.sum(onehot, axis=1, keepdims=True)
        running_ref[...] = jnp.broadcast_to(running, running_ref.shape)


def _positions(cls):
    t = cls.shape[1]
    meta = jax.ShapeDtypeStruct((CLASS_ROWS, LANES), F32)
    return pl.pallas_call(
        _position_kernel,
        grid=(2, t // POS_TM),
        in_specs=[pl.BlockSpec((1, POS_TM), lambda p, i: (0, i))],
        out_specs=[
            pl.BlockSpec((1, POS_TM), lambda p, i: (0, i * p)),
            pl.BlockSpec((CLASS_ROWS, LANES), lambda p, i: (0, 0)),
            pl.BlockSpec((CLASS_ROWS, LANES), lambda p, i: (0, 0)),
        ],
        out_shape=[jax.ShapeDtypeStruct((1, t), jnp.int32), meta, meta],
        scratch_shapes=[pltpu.VMEM((CLASS_ROWS, LANES), F32), pltpu.VMEM((CLASS_ROWS, LANES), F32)],
        compiler_params=_params("arbitrary", "arbitrary"),
        name="moe_positions",
    )(cls)


def _move_rows(copy_for_row):
    def start(group, carry):
        for sub in range(SUBLANES):
            copy_for_row(group, sub).start(priority=sub % 2)
        return carry

    def wait(group, carry):
        for sub in range(SUBLANES):
            copy_for_row(group, sub).wait()
        return carry

    lax.fori_loop(0, MOVE_TM // SUBLANES, start, 0)
    lax.fori_loop(0, MOVE_TM // SUBLANES, wait, 0)


def _scatter_rows_kernel(pos_ref, src_ref, init_ref, dst_ref, sem):
    del init_ref
    _move_rows(lambda group, sub: pltpu.make_async_copy(
        src_ref.at[group, pl.ds(sub, 1)],
        dst_ref.at[pl.ds(pos_ref[group * SUBLANES + sub], 1)], sem))


def _scatter_rows(pos, src, init):
    t, width = src.shape
    return pl.pallas_call(
        _scatter_rows_kernel,
        grid=(t // MOVE_TM,),
        in_specs=[
            pl.BlockSpec((MOVE_TM,), lambda i: (i,), memory_space=pltpu.SMEM),
            pl.BlockSpec((MOVE_TM // SUBLANES, SUBLANES, width), lambda i: (i, 0, 0)),
            pl.BlockSpec(memory_space=pl.ANY),
        ],
        out_specs=pl.BlockSpec(memory_space=pl.ANY),
        out_shape=jax.ShapeDtypeStruct(init.shape, init.dtype),
        scratch_shapes=[pltpu.SemaphoreType.DMA(())],
        input_output_aliases={2: 0},
        compiler_params=_params("arbitrary"),
        name="moe_scatter_rows",
    )(pos, src.reshape(t // SUBLANES, SUBLANES, width), init)


def _gather_rows_kernel(pos_ref, src_ref, dst_ref, sem):
    _move_rows(lambda group, sub: pltpu.make_async_copy(
        src_ref.at[pl.ds(pos_ref[group * SUBLANES + sub], 1)],
        dst_ref.at[group, pl.ds(sub, 1)], sem))


def _gather_rows(pos, src):
    t = pos.shape[0]
    width = src.shape[1]
    out = pl.pallas_call(
        _gather_rows_kernel,
        grid=(t // MOVE_TM,),
        in_specs=[
            pl.BlockSpec((MOVE_TM,), lambda i: (i,), memory_space=pltpu.SMEM),
            pl.BlockSpec(memory_space=pl.ANY),
        ],
        out_specs=pl.BlockSpec((MOVE_TM // SUBLANES, SUBLANES, width), lambda i: (i, 0, 0)),
        out_shape=jax.ShapeDtypeStruct((t // SUBLANES, SUBLANES, width), src.dtype),
        scratch_shapes=[pltpu.SemaphoreType.DMA(())],
        compiler_params=_params("arbitrary"),
        name="moe_gather_rows",
    )(pos, src)
    return out.reshape(t, width)


def _experts_kernel(ea_ref, eb_ref, used_ref, xs_ref, nf_ref,
                    wga_ref, wua_ref, wda_ref, wgb_ref, wub_ref, wdb_ref, ys_ref):
    del ea_ref, eb_ref
    k = pl.program_id(0)

    @pl.when(k < used_ref[0])
    def _():
        x = xs_ref[:, :D_MODEL]
        gates = xs_ref[:, D_MODEL:]
        h = _rms_rows(x, nf_ref[...]).astype(BF16)

        def ffn(wg_ref, wu_ref, wd_ref):
            a = jnp.dot(h, wg_ref[...], preferred_element_type=F32)
            u = jnp.dot(h, wu_ref[...], preferred_element_type=F32)
            return jnp.dot((a * jax.nn.sigmoid(a) * u).astype(BF16), wd_ref[...],
                           preferred_element_type=F32)

        moe = (gates[:, 0:1] * ffn(wga_ref, wua_ref, wda_ref)
               + gates[:, 1:2] * ffn(wgb_ref, wub_ref, wdb_ref))
        ys_ref[...] = x + moe

    @pl.when(k >= used_ref[0])
    def _():
        ys_ref[...] = jnp.zeros_like(ys_ref)


def _experts(xs, norm_ffn, w_gate, w_up, w_down, expert_a, expert_b, n_used):
    n_tiles = xs.shape[0] // EXP_TR
    up = lambda sel: pl.BlockSpec((None, D_MODEL, D_FF), lambda k, ea, eb, nu: (sel(ea, eb)[k], 0, 0))
    down = lambda sel: pl.BlockSpec((None, D_FF, D_MODEL), lambda k, ea, eb, nu: (sel(ea, eb)[k], 0, 0))
    first = lambda ea, eb: ea
    second = lambda ea, eb: eb
    grid_spec = pltpu.PrefetchScalarGridSpec(
        num_scalar_prefetch=3,
        grid=(n_tiles,),
        in_specs=[
            pl.BlockSpec((EXP_TR, ROW_WIDTH), lambda k, ea, eb, nu: (k, 0)),
            pl.BlockSpec((1, D_MODEL), lambda k, ea, eb, nu: (0, 0)),
            up(first), up(first), down(first), up(second), up(second), down(second),
        ],
        out_specs=pl.BlockSpec((EXP_TR, D_MODEL), lambda k, ea, eb, nu: (k, 0)),
    )
    return pl.pallas_call(
        _experts_kernel,
        grid_spec=grid_spec,
        out_shape=jax.ShapeDtypeStruct((xs.shape[0], D_MODEL), F32),
        compiler_params=_params("arbitrary"),
        name="experts",
    )(expert_a, expert_b, n_used, xs, norm_ffn, w_gate, w_up, w_down, w_gate, w_up, w_down)


def _tile_experts(offs, sizes, n_tiles):
    ends = (offs[:N_CLASSES, 0] + sizes[:N_CLASSES, 0]).astype(jnp.int32)
    starts = jnp.arange(n_tiles, dtype=jnp.int32) * EXP_TR
    tile_cls = jnp.minimum(jnp.sum(ends[None, :] <= starts[:, None], axis=1), N_CLASSES - 1)
    group = tile_cls // PAIRS_PER_GROUP
    pair = tile_cls % PAIRS_PER_GROUP
    slot_a = jnp.array([a for a, _ in PAIRS], jnp.int32)[pair]
    slot_b = jnp.array([b for _, b in PAIRS], jnp.int32)[pair]
    expert_a = (group * EXPERTS_PER_GROUP + slot_a).astype(jnp.int32)
    expert_b = (group * EXPERTS_PER_GROUP + slot_b).astype(jnp.int32)
    return expert_a, expert_b, (ends[-1] // EXP_TR).reshape(1)


def _router_layout(router_w, router_bias):
    w = router_w.T.reshape(N_GROUPS, EXPERTS_PER_GROUP, D_MODEL)
    w = jnp.transpose(w, (1, 0, 2))
    w = jnp.pad(w, ((0, 0), (0, SUBLANES - N_GROUPS), (0, 0))).reshape(ROUTER_ROWS, D_MODEL)
    b = jnp.transpose(router_bias.reshape(N_GROUPS, EXPERTS_PER_GROUP), (1, 0))
    b = jnp.pad(b, ((0, 0), (0, SUBLANES - N_GROUPS)), constant_values=NEG_BIG)
    return w.astype(F32), b.reshape(ROUTER_ROWS, 1).astype(F32)


def kernel(x, mem, norm_mix, norm_mem, w_in, b_forget, fox_q_norm, fox_k_norm, mem_q_norm,
           mem_k_norm, w_mem_kv, w_o_ret, w_o_fox, w_o_mem, w_out, norm_ffn, router_w,
           router_bias, w_gate, w_up, w_down):
    batch, seq, d = x.shape
    mem_len = mem.shape[1]
    depth = w_in.shape[0]
    t = batch * seq
    ff_lo = 7 * D_MODEL
    ff_hi = ff_lo + FOX_HEADS

    cos, sin, intra, qd, kd, cd = _retention_tables(seq)
    rw_pad, rb_pad = _router_layout(router_w, router_bias)
    row = lambda v: v.reshape(1, -1).astype(F32)

    x2 = x.reshape(t, d)
    mem2 = mem.reshape(batch * mem_len, d)
    n_tiles = t // EXP_TR + N_CLASSES
    sorted_rows = jnp.zeros((n_tiles * EXP_TR, ROW_WIDTH), F32)
    for l in range(depth):
        w_main = jnp.concatenate([w_in[l, :, :ff_lo], w_in[l, :, ff_hi:]], axis=1).astype(BF16)
        w_ff = jnp.pad(w_in[l, :, ff_lo:ff_hi], ((0, 0), (0, LANES - FOX_HEADS))).astype(BF16)
        b_pad = jnp.pad(b_forget[l], (0, LANES - FOX_HEADS)).reshape(1, LANES)

        proj, ff = _in_projection(x2, row(norm_mix[l]), w_main, w_ff)
        cum, cum_split = _forget_cumsum(ff, b_pad, batch, seq)
        ro = _retention(proj, cos, sin, intra, qd, kd, cd, batch, seq)
        bound = (FOX_BOUND_SCALE * jnp.max(jnp.abs(fox_q_norm[l]))
                 * jnp.max(jnp.abs(fox_k_norm[l])))
        first = _fox_first_blocks(cum, bound, batch, seq)
        fo = _fox_attention(proj, cum_split, bound.reshape(1).astype(F32), first,
                            row(jnp.tile(fox_q_norm[l], 2)), row(jnp.tile(fox_k_norm[l], 2)),
                            batch, seq)
        mk, mv = _mem_kv(mem2, row(norm_mem[l]), w_mem_kv[l].astype(BF16), row(mem_k_norm[l]))
        xa, cls = _mix_and_route(
            ro, fo, mk, mv, row(mem_q_norm[l]), proj, x2, w_o_ret[l].astype(BF16), w_o_fox[l].astype(BF16),
            w_o_mem[l].astype(BF16), w_out[l].astype(BF16), row(norm_ffn[l]), rw_pad, rb_pad,
            seq, mem_len)
        pos, offs, sizes = _positions(cls)
        pos = pos.reshape(t)
        expert_a, expert_b, n_used = _tile_experts(offs, sizes, n_tiles)
        sorted_rows = _scatter_rows(pos, xa, sorted_rows)
        ys = _experts(sorted_rows, row(norm_ffn[l]), w_gate[l].astype(BF16), w_up[l].astype(BF16),
                      w_down[l].astype(BF16), expert_a, expert_b, n_used)
        x2 = _gather_rows(pos, ys)
    return x2.reshape(batch, seq, d)
```
